```python
import math
import jax, jax.numpy as jnp
from jax import lax
import numpy as np

D_MODEL = 1024
BATCH = 8
SEQ = 2048
DEPTH = 2
DEC_BATCH = 128
DEC_SEQ = 4
PAST_LEN = 2048
PAGE_SIZE = 128

HEAD_DIM = 64
N_MIX_HEADS = D_MODEL // HEAD_DIM
A_HEADS = N_MIX_HEADS // 2
A_KV_HEADS = 2
A_GROUP = A_HEADS // A_KV_HEADS
IDX_HEADS = 4
IDX_DIM = 64
TOPK_MAX = 256
B_HEADS = N_MIX_HEADS // 4
B_VDIM = 2 * HEAD_DIM
C_HEADS = N_MIX_HEADS
FORGET_BIAS_INIT = 2.0
N_BUCKETS = 32
MAX_DISTANCE = 128
N_BIAS_HEADS = A_HEADS + B_HEADS
D_FF = 2816
CONV_W = 3
QUERY_BLOCK = 128
EPS = 1e-6
N_EVEN = (DEPTH + 1) // 2
N_ODD = DEPTH // 2

EVEN_SPLIT = (A_HEADS * HEAD_DIM, A_KV_HEADS * HEAD_DIM, A_KV_HEADS * HEAD_DIM,
              IDX_HEADS * IDX_DIM, IDX_DIM, IDX_HEADS,
              B_HEADS * 2 * HEAD_DIM, B_HEADS * 2 * HEAD_DIM, B_HEADS * B_VDIM)
EVEN_IN = sum(EVEN_SPLIT)
EVEN_OUT = A_HEADS * HEAD_DIM + B_HEADS * B_VDIM
ODD_SPLIT = (C_HEADS * HEAD_DIM, C_HEADS * HEAD_DIM, C_HEADS * HEAD_DIM, C_HEADS)
ODD_IN = sum(ODD_SPLIT)
ODD_OUT = C_HEADS * HEAD_DIM

kernel_name = 'hybrid_dsa_diff_fox_convffn_step'


def _rms_norm(x, g):
    xf = x.astype(jnp.float32)
    y = xf * lax.rsqrt(jnp.mean(xf * xf, axis=-1, keepdims=True) + EPS)
    return (y * g.astype(jnp.float32)).astype(x.dtype)


def _split_cols(a, sizes):
    return jnp.split(a, np.cumsum(sizes)[:-1].tolist(), axis=-1)


def _t5_bucket(rel):
    n = jnp.maximum(rel, 0)
    exact = N_BUCKETS // 2
    nf = jnp.maximum(n, 1).astype(jnp.float32)
    log_b = exact + (jnp.log(nf / exact) / math.log(MAX_DISTANCE / exact) * (N_BUCKETS - exact)).astype(jnp.int32)
    return jnp.where(n < exact, n, jnp.minimum(log_b, N_BUCKETS - 1))


def _gather_pages(pool, layer, page_table):
    g = pool[layer, page_table]
    return g.reshape(g.shape[0], g.shape[1] * g.shape[2], *g.shape[3:])


def _sweep_queries(fn, qpos, *q_arrays):
    t = qpos.shape[0]
    qb = QUERY_BLOCK if t % QUERY_BLOCK == 0 else t
    nb = t // qb
    blocks = tuple(jnp.moveaxis(a.reshape(a.shape[0], nb, qb, *a.shape[2:]), 1, 0) for a in q_arrays)
    out = lax.map(lambda args: fn(*args), (qpos.reshape(nb, qb),) + blocks)
    out = jnp.moveaxis(out, 0, 1)
    return out.reshape(out.shape[0], t, *out.shape[3:])


def _even_mixer(h, qpos, kpos, pools, layer, page_table, lam_init,
                w_in, w_out, lq1, lk1, lq2, lk2, subln, bias_table):
    b, t, _ = h.shape
    q_a, k_a, v_a, q_i, k_i, w_i, q_b, k_b, v_b = _split_cols(h @ w_in, EVEN_SPLIT)
    q_a = q_a.reshape(b, t, A_KV_HEADS, A_GROUP, HEAD_DIM)
    k_a = k_a.reshape(b, t, A_KV_HEADS, HEAD_DIM)
    v_a = v_a.reshape(b, t, A_KV_HEADS, HEAD_DIM)
    q_i = q_i.reshape(b, t, IDX_HEADS, IDX_DIM)
    q_b = q_b.reshape(b, t, B_HEADS, 2, HEAD_DIM)
    k_b = k_b.reshape(b, t, B_HEADS, 2 * HEAD_DIM)
    v_b = v_b.reshape(b, t, B_HEADS, B_VDIM)
    rows = (k_a, v_a, k_i, k_b, v_b)
    if pools is None:
        ka, va, ki, kb, vb = rows
    else:
        ka, va, ki, kb, vb = tuple(jnp.concatenate([_gather_pages(p, layer, page_table), r], axis=1)
                                   for p, r in zip(pools, rows))
    L = ka.shape[1]
    n_sel = min(TOPK_MAX, L // 4)
    kb = kb.reshape(b, L, B_HEADS, 2, HEAD_DIM)
    tab_a = bias_table[:, :A_HEADS].astype(jnp.float32)
    tab_b = bias_table[:, A_HEADS:].astype(jnp.float32)
    lam = (jnp.exp(jnp.sum(lq1.astype(jnp.float32) * lk1.astype(jnp.float32)))
           - jnp.exp(jnp.sum(lq2.astype(jnp.float32) * lk2.astype(jnp.float32))) + lam_init)

    def dsa_block(qp, qa, qi, wi):
        s_idx = jnp.einsum('bthd,bsd->bths', qi, ki).astype(jnp.float32) * IDX_DIM ** -0.5
        score = jnp.einsum('bths,bth->bts', jax.nn.relu(s_idx), wi.astype(jnp.float32)) * IDX_HEADS ** -0.5
        causal = kpos[None, :] <= qp[:, None]
        score = jnp.where(causal[None], score, -jnp.inf)
        _, sel = lax.top_k(score, n_sel)
        k_sel = jax.vmap(lambda kk, ii: kk[ii])(ka, sel)
        v_sel = jax.vmap(lambda vv, ii: vv[ii])(va, sel)
        rel = qp[None, :, None] - kpos[sel]
        bias = jnp.moveaxis(tab_a[_t5_bucket(rel)], -1, 2).reshape(b, qp.shape[0], A_KV_HEADS, A_GROUP, n_sel)
        logits = jnp.einsum('btgrd,btkgd->btgrk', qa, k_sel).astype(jnp.float32) * HEAD_DIM ** -0.5 + bias
        logits = jnp.where((rel >= 0)[:, :, None, None, :], logits, -jnp.inf)
        p = jax.nn.softmax(logits, axis=-1).astype(va.dtype)
        o = jnp.einsum('btgrk,btkgd->btgrd', p, v_sel)
        return o.reshape(b, qp.shape[0], A_HEADS * HEAD_DIM)

    def diff_block(qp, qb):
        logits = jnp.einsum('btnmd,bsnmd->bnmts', qb, kb).astype(jnp.float32) * HEAD_DIM ** -0.5
        rel = qp[:, None] - kpos[None, :]
        bias = jnp.moveaxis(tab_b[_t5_bucket(rel)], -1, 0)
        logits = jnp.where(rel >= 0, logits + bias[None, :, None], -jnp.inf)
        p = jax.nn.softmax(logits, axis=-1)
        a = p[:, :, 0] - lam * p[:, :, 1]
        o = jnp.einsum('bnts,bsne->btne', a.astype(vb.dtype), vb)
        o = _rms_norm(o, subln) * (1.0 - lam_init)
        return o.reshape(b, qp.shape[0], B_HEADS * B_VDIM)

    out_a = _sweep_queries(dsa_block, qpos, q_a, q_i, w_i)
    out_b = _sweep_queries(diff_block, qpos, q_b)
    out = jnp.concatenate([out_a, out_b], axis=-1) @ w_out
    return out, rows


def _odd_mixer(h, qpos, kpos, pools, layer, page_table, w_in, b_f, w_out):
    b, t, _ = h.shape
    q, k, v, f = _split_cols(h @ w_in, ODD_SPLIT)
    q = q.reshape(b, t, C_HEADS, HEAD_DIM)
    k = k.reshape(b, t, C_HEADS, HEAD_DIM)
    v = v.reshape(b, t, C_HEADS, HEAD_DIM)
    logf = jax.nn.log_sigmoid((f + b_f).astype(jnp.float32))
    rows = (k, v, logf.astype(h.dtype))
    if pools is None:
        kc, vc, lf = k, v, logf
    else:
        kc = jnp.concatenate([_gather_pages(pools[0], layer, page_table), k], axis=1)
        vc = jnp.concatenate([_gather_pages(pools[1], layer, page_table), v], axis=1)
        lf = jnp.concatenate([_gather_pages(pools[2], layer, page_table).astype(jnp.float32), logf], axis=1)
    cum = jnp.cumsum(lf, axis=1)
    cq = cum[:, -t:]
    ck = jnp.moveaxis(cum, 1, 2)

    def fox_block(qp, qq, cqq):
        logits = jnp.einsum('bthd,bshd->bhts', qq, kc).astype(jnp.float32) * HEAD_DIM ** -0.5
        logits = logits + jnp.moveaxis(cqq, 1, 2)[..., None] - ck[:, :, None, :]
        causal = kpos[None, :] <= qp[:, None]
        logits = jnp.where(causal, logits, -jnp.inf)
        p = jax.nn.softmax(logits, axis=-1).astype(vc.dtype)
        o = jnp.einsum('bhts,bshd->bthd', p, vc)
        return o.reshape(b, qp.shape[0], C_HEADS * HEAD_DIM)

    out = _sweep_queries(fox_block, qpos, q, cq) @ w_out
    return out, rows


def _conv_ffn(h, conv_state, w_gu, w_conv, b_conv, w_down):
    b, t, _ = h.shape
    g, u = jnp.split(h @ w_gu, 2, axis=-1)
    prev = jnp.zeros((b, CONV_W - 1, D_FF), g.dtype) if conv_state is None else conv_state
    g_ext = jnp.concatenate([prev, g], axis=1)
    g_conv = b_conv
    for j in range(CONV_W):
        g_conv = g_conv + w_conv[j] * g_ext[:, j:j + t]
    a = jax.nn.gelu(g_conv, approximate=True) * u
    return a @ w_down, g_ext[:, -(CONV_W - 1):]


def _trunk(x, qpos, kpos, caches, page_table, conv_state, P):
    even_rows, odd_rows, conv_rows = [], [], []
    for l in range(DEPTH):
        h = _rms_norm(x, P['norm_mix_pre'][l])
        if l % 2 == 0:
            e = l // 2
            pools = None if caches is None else caches[:5]
            lam_init = 0.8 - 0.6 * math.exp(-0.3 * l)
            m, rows = _even_mixer(h, qpos, kpos, pools, e, page_table, lam_init,
                                  P['w_in_even'][e], P['w_out_even'][e], P['lambda_q1'][e], P['lambda_k1'][e],
                                  P['lambda_q2'][e], P['lambda_k2'][e], P['diff_subln'][e], P['rel_bias_table'])
            even_rows.append(rows)
        else:
            o = l // 2
            pools = None if caches is None else caches[5:]
            m, rows = _odd_mixer(h, qpos, kpos, pools, o, page_table,
                                 P['w_in_odd'][o], P['b_forget'][o], P['w_out_odd'][o])
            odd_rows.append(rows)
        x = x + _rms_norm(m, P['norm_mix_post'][l])
        h = _rms_norm(x, P['norm_ffn_pre'][l])
        f, cs = _conv_ffn(h, None if conv_state is None else conv_state[l],
                          P['w_gate_up'][l], P['w_conv'][l], P['b_conv'][l], P['w_down'][l])
        conv_rows.append(cs)
        x = x + _rms_norm(f, P['norm_ffn_post'][l])
    ev = [jnp.stack([r[i] for r in even_rows]) for i in range(5)]
    od = [jnp.stack([r[i] for r in odd_rows]) for i in range(3)]
    return (x, ev[0], ev[1], ev[2], ev[3], ev[4], od[0], od[1], od[2], jnp.stack(conv_rows))


def setup_inputs(seed: int = 0) -> dict:
    key = jax.random.key(seed)
    ks = jax.random.split(key, 32)
    n_pages = PAST_LEN // PAGE_SIZE
    n_pool = (5 * DEC_BATCH * n_pages + 3) // 4

    def nrm(k, shape, scale=1.0):
        return scale * jax.random.normal(k, shape, jnp.float32)

    page_table = jax.random.permutation(ks[11], n_pool)[: DEC_BATCH * n_pages]
    page_table = page_table.reshape(DEC_BATCH, n_pages).astype(jnp.int32)
    return dict(
        x_prompt=nrm(ks[0], (BATCH, SEQ, D_MODEL)),
        x_sample=nrm(ks[1], (DEC_BATCH, DEC_SEQ, D_MODEL)),
        cache_a_k=nrm(ks[2], (N_EVEN, n_pool, PAGE_SIZE, A_KV_HEADS, HEAD_DIM)),
        cache_a_v=nrm(ks[3], (N_EVEN, n_pool, PAGE_SIZE, A_KV_HEADS, HEAD_DIM)),
        cache_a_kidx=nrm(ks[4], (N_EVEN, n_pool, PAGE_SIZE, IDX_DIM)),
        cache_b_k=nrm(ks[5], (N_EVEN, n_pool, PAGE_SIZE, B_HEADS, 2 * HEAD_DIM)),
        cache_b_v=nrm(ks[6], (N_EVEN, n_pool, PAGE_SIZE, B_HEADS, B_VDIM)),
        cache_c_k=nrm(ks[7], (N_ODD, n_pool, PAGE_SIZE, C_HEADS, HEAD_DIM)),
        cache_c_v=nrm(ks[8], (N_ODD, n_pool, PAGE_SIZE, C_HEADS, HEAD_DIM)),
        cache_c_logf=jax.nn.log_sigmoid(nrm(ks[9], (N_ODD, n_pool, PAGE_SIZE, C_HEADS)) + FORGET_BIAS_INIT),
        state_conv=nrm(ks[10], (DEPTH, DEC_BATCH, CONV_W - 1, D_FF)),
        page_table=page_table,
        rel_bias_table=nrm(ks[12], (N_BUCKETS, N_BIAS_HEADS), 0.5),
        w_in_even=nrm(ks[13], (N_EVEN, D_MODEL, EVEN_IN), D_MODEL ** -0.5),
        w_out_even=nrm(ks[14], (N_EVEN, EVEN_OUT, D_MODEL), EVEN_OUT ** -0.5),
        lambda_q1=nrm(ks[15], (N_EVEN, HEAD_DIM), 0.1),
        lambda_k1=nrm(ks[16], (N_EVEN, HEAD_DIM), 0.1),
        lambda_q2=nrm(ks[17], (N_EVEN, HEAD_DIM), 0.1),
        lambda_k2=nrm(ks[18], (N_EVEN, HEAD_DIM), 0.1),
        diff_subln=1.0 + nrm(ks[19], (N_EVEN, B_VDIM), 0.02),
        w_in_odd=nrm(ks[20], (N_ODD, D_MODEL, ODD_IN), D_MODEL ** -0.5),
        b_forget=FORGET_BIAS_INIT + nrm(ks[21], (N_ODD, C_HEADS), 0.1),
        w_out_odd=nrm(ks[22], (N_ODD, ODD_OUT, D_MODEL), ODD_OUT ** -0.5),
        norm_mix_pre=1.0 + nrm(ks[23], (DEPTH, D_MODEL), 0.02),
        norm_mix_post=1.0 + nrm(ks[24], (DEPTH, D_MODEL), 0.02),
        norm_ffn_pre=1.0 + nrm(ks[25], (DEPTH, D_MODEL), 0.02),
        norm_ffn_post=1.0 + nrm(ks[26], (DEPTH, D_MODEL), 0.02),
        w_gate_up=nrm(ks[27], (DEPTH, D_MODEL, 2 * D_FF), D_MODEL ** -0.5),
        w_conv=nrm(ks[28], (DEPTH, CONV_W, D_FF), CONV_W ** -0.5),
        b_conv=nrm(ks[29], (DEPTH, D_FF), 0.01),
        w_down=nrm(ks[30], (DEPTH, D_FF, D_MODEL), D_FF ** -0.5),
    )


def reference(x_prompt, x_sample, cache_a_k, cache_a_v, cache_a_kidx, cache_b_k, cache_b_v,
              cache_c_k, cache_c_v, cache_c_logf, state_conv, page_table, rel_bias_table,
              w_in_even, w_out_even, lambda_q1, lambda_k1, lambda_q2, lambda_k2, diff_subln,
              w_in_odd, b_forget, w_out_odd, norm_mix_pre, norm_mix_post, norm_ffn_pre, norm_ffn_post,
              w_gate_up, w_conv, b_conv, w_down):
    P = dict(rel_bias_table=rel_bias_table, w_in_even=w_in_even, w_out_even=w_out_even,
             lambda_q1=lambda_q1, lambda_k1=lambda_k1, lambda_q2=lambda_q2, lambda_k2=lambda_k2,
             diff_subln=diff_subln, w_in_odd=w_in_odd, b_forget=b_forget, w_out_odd=w_out_odd,
             norm_mix_pre=norm_mix_pre, norm_mix_post=norm_mix_post, norm_ffn_pre=norm_ffn_pre,
             norm_ffn_post=norm_ffn_post, w_gate_up=w_gate_up, w_conv=w_conv, b_conv=b_conv, w_down=w_down)
    pos_p = jnp.arange(x_prompt.shape[1], dtype=jnp.int32)
    (y_prompt, a_k_p, a_v_p, a_kidx_p, b_k_p, b_v_p,
     c_k_p, c_v_p, c_logf_p, conv_p) = _trunk(x_prompt, pos_p, pos_p, None, None, None, P)
    past_len = page_table.shape[1] * cache_a_k.shape[2]
    kpos_s = jnp.arange(past_len + x_sample.shape[1], dtype=jnp.int32)
    qpos_s = kpos_s[past_len:]
    caches = (cache_a_k, cache_a_v, cache_a_kidx, cache_b_k, cache_b_v, cache_c_k, cache_c_v, cache_c_logf)
    (y_sample, a_k_s, a_v_s, a_kidx_s, b_k_s, b_v_s,
     c_k_s, c_v_s, c_logf_s, conv_s) = _trunk(x_sample, qpos_s, kpos_s, caches, page_table, state_conv, P)
    return (y_prompt, y_sample, a_k_p, a_k_s, a_v_p, a_v_s, a_kidx_p, a_kidx_s, b_k_p, b_k_s, b_v_p, b_v_s,
            c_k_p, c_k_s, c_v_p, c_v_s, c_logf_p, c_logf_s, conv_p, conv_s)
```

```python
import functools
import math

import jax
import jax.numpy as jnp
import numpy as np
from jax import lax
from jax.experimental import pallas as pl
from jax.experimental.pallas import tpu as pltpu

F32 = jnp.float32
BF16 = jnp.bfloat16
HIGHEST = lax.Precision.HIGHEST

D_MODEL = 1024
HEAD_DIM = 64
A_HEADS = 8
A_KV_HEADS = 2
IDX_HEADS = 4
IDX_DIM = 64
TOPK_MAX = 256
B_HEADS = 4
B_VDIM = 128
C_HEADS = 16
N_BUCKETS = 32
MAX_DISTANCE = 128
D_FF = 2816
CONV_W = 3
EPS = 1e-6
PAGE = 128

LANES = 128
NEG = -1e30
INT_MIN = -2147483648
VMEM_LIMIT = 48 * 1024 * 1024

EVEN_SPLIT = (512, 128, 128, 256, 64, 4, 512, 512, 512)
ODD_SPLIT = (1024, 1024, 1024, 16)

NT_DIMS = (((1,), (1,)), ((), ()))


def _cparams(sem):
    return pltpu.CompilerParams(dimension_semantics=sem, vmem_limit_bytes=VMEM_LIMIT)


def _rms(x, g):
    return x * lax.rsqrt(jnp.mean(x * x, axis=-1, keepdims=True) + EPS) * g


def _log_sigmoid(x):
    return -(jnp.maximum(-x, 0.0) + jnp.log1p(jnp.exp(-jnp.abs(x))))


def _gelu_tanh(x):
    c = math.sqrt(2.0 / math.pi)
    return x * (0.5 * (1.0 + jnp.tanh(c * (x + 0.044715 * (x * x * x)))))


def _dot(a, b):
    return jnp.dot(a, b, preferred_element_type=F32)


def _dot_nt(a, b):
    return lax.dot_general(a, b, NT_DIMS, preferred_element_type=F32)


def _dot_hi(a, b):
    return jnp.dot(a, b, preferred_element_type=F32, precision=HIGHEST)


def _softmax_step(z, m_old, l_old, acc_old, v):
    m_new = jnp.maximum(m_old, jnp.max(z, axis=1, keepdims=True))
    alpha = jnp.exp(m_old - m_new)
    p = jnp.exp(z - m_new)
    l_new = alpha * l_old + jnp.sum(p, axis=1, keepdims=True)
    acc_new = alpha * acc_old + _dot(p.astype(BF16), v)
    return m_new, l_new, acc_new


def _sort_key(s):
    s = jnp.where(s == 0.0, 0.0, s)
    bits = pltpu.bitcast(s, jnp.int32)
    return jnp.where(bits < 0, bits ^ jnp.int32(0x7FFFFFFF), bits)


def _rms_proj_kernel(x_ref, g_ref, *refs, n_w, logsig_last):
    w_refs = refs[:n_w]
    if logsig_last:
        bias_ref = refs[n_w]
        o_refs = refs[n_w + 1:]
    else:
        o_refs = refs[n_w:]
    h = _rms(x_ref[...], g_ref[...]).astype(BF16)
    for idx in range(n_w):
        y = _dot(h, w_refs[idx][...])
        if logsig_last and idx == n_w - 1:
            y = _log_sigmoid(y + bias_ref[...])
        o_refs[idx][...] = y


def _rms_proj(x2d, g, w_list, tm, logsig_bias=None):
    m, d = x2d.shape
    n_w = len(w_list)
    in_specs = [pl.BlockSpec((tm, d), lambda i: (i, 0)), pl.BlockSpec((1, d), lambda i: (0, 0))]
    in_specs += [pl.BlockSpec(w.shape, lambda i: (0, 0)) for w in w_list]
    args = [x2d, g.reshape(1, d)] + list(w_list)
    if logsig_bias is not None:
        in_specs.append(pl.BlockSpec(logsig_bias.shape, lambda i: (0, 0)))
        args.append(logsig_bias)
    out_shape = [jax.ShapeDtypeStruct((m, w.shape[1]), F32) for w in w_list]
    out_specs = [pl.BlockSpec((tm, w.shape[1]), lambda i: (i, 0)) for w in w_list]
    return pl.pallas_call(
        functools.partial(_rms_proj_kernel, n_w=n_w, logsig_last=logsig_bias is not None),
        grid=(m // tm,), in_specs=in_specs, out_specs=out_specs, out_shape=out_shape,
        compiler_params=_cparams(("parallel",)), name="rms_proj")(*args)


def _proj_post_kernel(x_ref, g_ref, *refs, n_in):
    o_refs = refs[:n_in]
    w_refs = refs[n_in:2 * n_in]
    out_ref = refs[2 * n_in]
    acc = None
    for o, w in zip(o_refs, w_refs):
        t = _dot(o[...].astype(BF16), w[...])
        acc = t if acc is None else acc + t
    out_ref[...] = x_ref[...] + _rms(acc, g_ref[...])


def _proj_post(x2d, g, o_list, w_list, tm):
    m, d = x2d.shape
    n_in = len(o_list)
    in_specs = [pl.BlockSpec((tm, d), lambda i: (i, 0)), pl.BlockSpec((1, d), lambda i: (0, 0))]
    in_specs += [pl.BlockSpec((tm, o.shape[1]), lambda i: (i, 0)) for o in o_list]
    in_specs += [pl.BlockSpec(w.shape, lambda i: (0, 0)) for w in w_list]
    return pl.pallas_call(
        functools.partial(_proj_post_kernel, n_in=n_in),
        grid=(m // tm,), in_specs=in_specs, out_specs=pl.BlockSpec((tm, d), lambda i: (i, 0)),
        out_shape=jax.ShapeDtypeStruct((m, d), F32),
        compiler_params=_cparams(("parallel",)), name="proj_post")(x2d, g.reshape(1, d), *o_list, *w_list)


def _ffn_kernel(*refs, tm, tiles_per_seq, sample):
    if sample:
        (x_ref, gpre_ref, gpost_ref, wg_ref, wu_ref, wc_ref, bc_ref, wd_ref, p1_ref, p2_ref,
         y_ref, gout_ref, h_ref, acc_ref, gs_ref) = refs
    else:
        (x_ref, gpre_ref, gpost_ref, wg_ref, wu_ref, wc_ref, bc_ref, wd_ref,
         y_ref, gout_ref, h_ref, acc_ref, gs_ref, halo_ref) = refs
    i = pl.program_id(0)
    c = pl.program_id(1)

    @pl.when(c == 0)
    def _():
        h_ref[...] = _rms(x_ref[...], gpre_ref[...]).astype(BF16)
        acc_ref[...] = jnp.zeros_like(acc_ref)

    h = h_ref[...]
    g = _dot(h, wg_ref[...])
    u = _dot(h, wu_ref[...])
    if sample:
        gs_ref[0:8, :] = jnp.zeros((8, g.shape[1]), F32)
    else:
        first = (i % tiles_per_seq) == 0

        @pl.when(first)
        def _():
            gs_ref[0:8, :] = jnp.zeros((8, g.shape[1]), F32)

        @pl.when(jnp.logical_not(first))
        def _():
            gs_ref[0:8, :] = halo_ref[c]

    gs_ref[8:tm + 8, :] = g
    g1 = gs_ref[7:tm + 7, :]
    g2 = gs_ref[6:tm + 6, :]
    if sample:
        t = lax.broadcasted_iota(jnp.int32, (tm, 1), 0) % tiles_per_seq
        g1 = jnp.where(t >= 1, g1, 0.0) + p1_ref[...]
        g2 = jnp.where(t >= 2, g2, 0.0) + p2_ref[...]
        gout_ref[...] = g
    else:
        tail = gs_ref[tm:tm + 8, :]
        halo_ref[c] = tail
        gout_ref[0] = tail
    w = wc_ref[...]
    gc = bc_ref[...] + w[0:1, :] * g2
    gc = gc + w[1:2, :] * g1
    gc = gc + w[2:3, :] * g
    a = _gelu_tanh(gc) * u
    acc_ref[...] += _dot(a.astype(BF16), wd_ref[...])

    @pl.when(c == pl.num_programs(1) - 1)
    def _():
        y_ref[...] = x_ref[...] + _rms(acc_ref[...], gpost_ref[...])


def _ffn(x2d, gpre, gpost, wg, wu, wc, bc, wd, tm, tf, tiles_per_seq, prev=None):
    m, d = x2d.shape
    dff = wg.shape[1]
    nc = dff // tf
    sample = prev is not None
    in_specs = [
        pl.BlockSpec((tm, d), lambda i, c: (i, 0)),
        pl.BlockSpec((1, d), lambda i, c: (0, 0)),
        pl.BlockSpec((1, d), lambda i, c: (0, 0)),
        pl.BlockSpec((d, tf), lambda i, c: (0, c)),
        pl.BlockSpec((d, tf), lambda i, c: (0, c)),
        pl.BlockSpec((CONV_W, tf), lambda i, c: (0, c)),
        pl.BlockSpec((1, tf), lambda i, c: (0, c)),
        pl.BlockSpec((tf, d), lambda i, c: (c, 0)),
    ]
    args = [x2d, gpre.reshape(1, d), gpost.reshape(1, d), wg, wu, wc, bc.reshape(1, dff), wd]
    scratch = [pltpu.VMEM((tm, d), BF16), pltpu.VMEM((tm, d), F32), pltpu.VMEM((tm + 8, tf), F32)]
    if sample:
        in_specs += [pl.BlockSpec((tm, tf), lambda i, c: (i, c))] * 2
        args += list(prev)
        gout_shape = jax.ShapeDtypeStruct((m, dff), F32)
        gout_spec = pl.BlockSpec((tm, tf), lambda i, c: (i, c))
    else:
        nseq = m // (tm * tiles_per_seq)
        gout_shape = jax.ShapeDtypeStruct((nseq, 8, dff), F32)
        gout_spec = pl.BlockSpec((1, 8, tf), lambda i, c: (i // tiles_per_seq, 0, c))
        scratch.append(pltpu.VMEM((nc, 8, tf), F32))
    return pl.pallas_call(
        functools.partial(_ffn_kernel, tm=tm, tiles_per_seq=tiles_per_seq, sample=sample),
        grid=(m // tm, nc), in_specs=in_specs,
        out_specs=[pl.BlockSpec((tm, d), lambda i, c: (i, 0)), gout_spec],
        out_shape=[jax.ShapeDtypeStruct((m, d), F32), gout_shape],
        scratch_shapes=scratch,
        compiler_params=_cparams(("arbitrary", "arbitrary")), name="conv_ffn")(*args)


def _cumsum_kernel(lf_ref, cum_ref, cumt_ref, *, tb, nchunk):
    row = lax.broadcasted_iota(jnp.int32, (tb, tb), 0)
    col = lax.broadcasted_iota(jnp.int32, (tb, tb), 1)
    lower = jnp.where(col <= row, 1.0, 0.0).astype(F32)
    carry = jnp.zeros((1, LANES), F32)
    for c in range(nchunk):
        x = lf_ref[0, c * tb:(c + 1) * tb, :]
        ct = _dot_hi(lower, x) + carry
        cumt_ref[0, c * tb:(c + 1) * tb, :] = ct
        carry = ct[tb - 1:tb, :]
        cum_ref[0, c] = ct.T[0:C_HEADS, :]


def _cumsum(lf3d, tb):
    b, t, _ = lf3d.shape
    nchunk = t // tb
    return pl.pallas_call(
        functools.partial(_cumsum_kernel, tb=tb, nchunk=nchunk),
        grid=(b,), in_specs=[pl.BlockSpec((1, t, LANES), lambda i: (i, 0, 0))],
        out_specs=[pl.BlockSpec((1, nchunk, C_HEADS, tb), lambda i: (i, 0, 0, 0)),
                   pl.BlockSpec((1, t, LANES), lambda i: (i, 0, 0))],
        out_shape=[jax.ShapeDtypeStruct((b, nchunk, C_HEADS, tb), F32),
                   jax.ShapeDtypeStruct((b, t, LANES), F32)],
        compiler_params=_cparams(("parallel",)), name="logf_cumsum")(lf3d)


def _flash2_kernel(*refs, mode, tb, lam_init):
    if mode == "fox":
        q_ref, k_ref, v_ref, cum_ref, cumt_ref, o_ref, m_ref, l_ref, acc_ref = refs
    else:
        q_ref, k_ref, v_ref, bias_ref, lamp_ref, subln_ref, o_ref, m_ref, l_ref, acc_ref = refs
    g = pl.program_id(1)
    qi = pl.program_id(2)
    lane = lax.broadcasted_iota(jnp.int32, (tb, LANES), 1)
    row = lax.broadcasted_iota(jnp.int32, (tb, tb), 0)
    col = lax.broadcasted_iota(jnp.int32, (tb, tb), 1)
    tri = col <= row
    q = q_ref[0] * (HEAD_DIM ** -0.5)
    qa = [jnp.where(lane < HEAD_DIM, q, 0.0).astype(BF16), jnp.where(lane >= HEAD_DIM, q, 0.0).astype(BF16)]
    if mode == "fox":
        ct = cumt_ref[0]
        cq = [jnp.sum(jnp.where(lane == 2 * g + a, ct, 0.0), axis=1, keepdims=True) for a in range(2)]
    m_ref[...] = jnp.full(m_ref.shape, NEG, F32)
    l_ref[...] = jnp.zeros_like(l_ref)
    acc_ref[...] = jnp.zeros_like(acc_ref)

    def body(c, carry):
        off = pl.multiple_of(c * tb, tb)
        kc = k_ref[0, pl.ds(off, tb), :].astype(BF16)
        vc = v_ref[0, pl.ds(off, tb), :].astype(BF16)
        valid = jnp.logical_or(c < qi, tri)
        if mode == "diff":
            bt = bias_ref[0, jnp.minimum(qi - c, 2)]
        for a in range(2):
            z = _dot_nt(qa[a], kc)
            if mode == "fox":
                z = z + cq[a] - cum_ref[0, c, pl.ds(2 * g + a, 1), :]
            else:
                z = z + bt
            z = jnp.where(valid, z, NEG)
            m_new, l_new, acc_new = _softmax_step(z, m_ref[a], l_ref[a], acc_ref[a], vc)
            m_ref[a] = m_new
            l_ref[a] = l_new
            acc_ref[a] = acc_new
        return carry

    lax.fori_loop(0, qi + 1, body, 0)
    o0 = acc_ref[0] / l_ref[0]
    o1 = acc_ref[1] / l_ref[1]
    if mode == "fox":
        o_ref[0] = jnp.where(lane < HEAD_DIM, o0, o1)
    else:
        lp = lamp_ref[...]
        lam = (jnp.exp(jnp.sum(lp[0:1] * lp[1:2], axis=1, keepdims=True))
               - jnp.exp(jnp.sum(lp[2:3] * lp[3:4], axis=1, keepdims=True)) + lam_init)
        o_ref[0] = _rms(o0 - lam * o1, subln_ref[...]) * (1.0 - lam_init)


def _flash2(mode, q, k, v, extra, tb, lam_init=0.0):
    b, t, width = q.shape
    ng = width // LANES
    nq = t // tb
    in_specs = [pl.BlockSpec((1, tb, LANES), lambda i, g, j: (i, j, g)),
                pl.BlockSpec((1, t, LANES), lambda i, g, j: (i, 0, g)),
                pl.BlockSpec((1, t, LANES), lambda i, g, j: (i, 0, g))]
    if mode == "fox":
        cum, cumt = extra
        in_specs += [pl.BlockSpec((1, nq, C_HEADS, tb), lambda i, g, j: (i, 0, 0, 0)),
                     pl.BlockSpec((1, tb, LANES), lambda i, g, j: (i, j, 0))]
    else:
        bias, lamp, subln = extra
        in_specs += [pl.BlockSpec((1, 3, tb, tb), lambda i, g, j: (g, 0, 0, 0)),
                     pl.BlockSpec(lamp.shape, lambda i, g, j: (0, 0)),
                     pl.BlockSpec(subln.shape, lambda i, g, j: (0, 0))]
    return pl.pallas_call(
        functools.partial(_flash2_kernel, mode=mode, tb=tb, lam_init=lam_init),
        grid=(b, ng, nq), in_specs=in_specs,
        out_specs=pl.BlockSpec((1, tb, LANES), lambda i, g, j: (i, j, g)),
        out_shape=jax.ShapeDtypeStruct((b, t, width), F32),
        scratch_shapes=[pltpu.VMEM((2, tb, 1), F32), pltpu.VMEM((2, tb, 1), F32),
                        pltpu.VMEM((2, tb, LANES), F32)],
        compiler_params=_cparams(("parallel", "parallel", "arbitrary")),
        name="attn_" + mode)(q, k, v, *extra)


def _kth_largest(count_ge, n_sel, shape):
    def step(it, ans):
        bit = lax.shift_left(jnp.int32(1), jnp.int32(31) - it)
        trial = ans | bit
        cnt = count_ge(trial ^ jnp.int32(INT_MIN))
        return jnp.where(cnt >= n_sel, trial, ans)

    ans = lax.fori_loop(0, 32, step, jnp.zeros(shape, jnp.int32))
    return ans ^ jnp.int32(INT_MIN)


def _dsa_p_kernel(qa_ref, ka_ref, va_ref, qi_ref, ki_ref, wi_ref, bias_ref, o_ref,
                  key_ref, am_ref, qm_ref, m_ref, l_ref, acc_ref, *, n_sel):
    tb = LANES
    qi = pl.program_id(1)
    row = lax.broadcasted_iota(jnp.int32, (tb, tb), 0)
    col = lax.broadcasted_iota(jnp.int32, (tb, tb), 1)
    tri = col <= row
    nchunks = qi + 1

    qidx = qi_ref[0]
    qh = [qidx[:, h * IDX_DIM:(h + 1) * IDX_DIM].astype(BF16) for h in range(IDX_HEADS)]
    w = wi_ref[0] * (IDX_DIM ** -0.5 * IDX_HEADS ** -0.5)
    wcol = [w[:, h:h + 1] for h in range(IDX_HEADS)]

    def p1(c, carry):
        off = pl.multiple_of(c * tb, tb)
        kc = ki_ref[0, pl.ds(off, tb), :].astype(BF16)
        s = jnp.zeros((tb, tb), F32)
        for h in range(IDX_HEADS):
            s = s + jnp.maximum(_dot_nt(qh[h], kc), 0.0) * wcol[h]
        s = jnp.where(jnp.logical_or(c < qi, tri), s, -jnp.inf)
        key_ref[c] = _sort_key(s)
        return carry

    lax.fori_loop(0, nchunks, p1, 0)

    def count(pred):
        def cb(c, acc):
            return acc + jnp.where(pred(key_ref[c]), 1.0, 0.0)
        acc = lax.fori_loop(0, nchunks, cb, jnp.zeros((tb, tb), F32))
        return jnp.sum(acc, axis=1, keepdims=True)

    thr = _kth_largest(lambda trial: count(lambda kk: kk >= trial), n_sel, (tb, 1))
    need = n_sel - count(lambda kk: kk > thr)
    ustrict = jnp.where(row < col, 1.0, 0.0).astype(BF16)

    def p3(c, carry):
        kk = key_ref[c]
        eq = kk == thr
        eqf = jnp.where(eq, 1.0, 0.0)
        prefix = _dot(eqf.astype(BF16), ustrict) + carry
        sel = jnp.logical_or(kk > thr, jnp.logical_and(eq, prefix < need))
        sel = jnp.logical_and(sel, jnp.logical_or(c < qi, tri))
        am_ref[c] = jnp.where(sel, 0.0, NEG)
        return carry + jnp.sum(eqf, axis=1, keepdims=True)

    lax.fori_loop(0, nchunks, p3, jnp.zeros((tb, 1), F32))

    lane = col
    qa = qa_ref[0] * (HEAD_DIM ** -0.5)
    for j in range(A_HEADS):
        blk = qa[:, (j // 2) * LANES:(j // 2 + 1) * LANES]
        grp = j // (A_HEADS // A_KV_HEADS)
        if j % 2 != grp:
            blk = pltpu.roll(blk, HEAD_DIM, axis=1)
        keep = (lane < HEAD_DIM) if grp == 0 else (lane >= HEAD_DIM)
        qm_ref[j] = jnp.where(keep, blk, 0.0).astype(BF16)
    m_ref[...] = jnp.full(m_ref.shape, NEG, F32)
    l_ref[...] = jnp.zeros_like(l_ref)
    acc_ref[...] = jnp.zeros_like(acc_ref)

    def p4(c, carry):
        off = pl.multiple_of(c * tb, tb)
        kc = ka_ref[0, pl.ds(off, tb), :].astype(BF16)
        vc = va_ref[0, pl.ds(off, tb), :].astype(BF16)
        am = am_ref[c]
        bidx = jnp.minimum(qi - c, 2)
        for j in range(A_HEADS):
            z = _dot_nt(qm_ref[j], kc) + bias_ref[j, bidx] + am
            m_new, l_new, acc_new = _softmax_step(z, m_ref[j], l_ref[j], acc_ref[j], vc)
            m_ref[j] = m_new
            l_ref[j] = l_new
            acc_ref[j] = acc_new
        return carry

    lax.fori_loop(0, nchunks, p4, 0)
    for c2 in range(A_HEADS // 2):
        grp = (2 * c2) // (A_HEADS // A_KV_HEADS)
        lo = acc_ref[2 * c2] / l_ref[2 * c2]
        hi = acc_ref[2 * c2 + 1] / l_ref[2 * c2 + 1]
        if grp == 0:
            hi = pltpu.roll(hi, HEAD_DIM, axis=1)
        else:
            lo = pltpu.roll(lo, HEAD_DIM, axis=1)
        o_ref[0, :, c2 * LANES:(c2 + 1) * LANES] = jnp.where(lane < HEAD_DIM, lo, hi)


def _dsa_p(q_a, k_a, v_a, q_i, k_i, w_i, bias, n_sel):
    b, t, _ = q_a.shape
    tb = LANES
    nq = t // tb
    in_specs = [pl.BlockSpec((1, tb, q_a.shape[2]), lambda i, j: (i, j, 0)),
                pl.BlockSpec((1, t, LANES), lambda i, j: (i, 0, 0)),
                pl.BlockSpec((1, t, LANES), lambda i, j: (i, 0, 0)),
                pl.BlockSpec((1, tb, q_i.shape[2]), lambda i, j: (i, j, 0)),
                pl.BlockSpec((1, t, IDX_DIM), lambda i, j: (i, 0, 0)),
                pl.BlockSpec((1, tb, LANES), lambda i, j: (i, j, 0)),
                pl.BlockSpec(bias.shape, lambda i, j: (0, 0, 0, 0))]
    return pl.pallas_call(
        functools.partial(_dsa_p_kernel, n_sel=n_sel),
        grid=(b, nq), in_specs=in_specs,
        out_specs=pl.BlockSpec((1, tb, q_a.shape[2]), lambda i, j: (i, j, 0)),
        out_shape=jax.ShapeDtypeStruct(q_a.shape, F32),
        scratch_shapes=[pltpu.VMEM((nq, tb, tb), jnp.int32), pltpu.VMEM((nq, tb, tb), F32),
                        pltpu.VMEM((A_HEADS, tb, LANES), BF16),
                        pltpu.VMEM((A_HEADS, tb, 1), F32), pltpu.VMEM((A_HEADS, tb, 1), F32),
                        pltpu.VMEM((A_HEADS, tb, LANES), F32)],
        compiler_params=_cparams(("parallel", "arbitrary")), name="attn_dsa")(q_a, k_a, v_a, q_i, k_i, w_i, bias)


def _fox_s_kernel(pt_ref, q_ref, kn_ref, vn_ref, lfn_ref, kp_ref, vp_ref, lfp_ref, o_ref,
                  qbd_ref, kpad_ref, vpad_ref, lfpad_ref, m_ref, l_ref, acc_ref, carry_ref, cq_ref, *, ts):
    b = pl.program_id(0)
    p = pl.program_id(1)
    nrow = ts * C_HEADS
    width = C_HEADS * HEAD_DIM
    row = lax.broadcasted_iota(jnp.int32, (PAGE, PAGE), 0)
    col = lax.broadcasted_iota(jnp.int32, (PAGE, PAGE), 1)

    @pl.when(jnp.logical_and(b == 0, p == 0))
    def _():
        kpad_ref[...] = jnp.zeros_like(kpad_ref)
        vpad_ref[...] = jnp.zeros_like(vpad_ref)

    def step(k, v, is_new):
        z = _dot_nt(qbd_ref[...].astype(BF16), k.astype(BF16))
        lft = lfpad_ref[...].T[0:C_HEADS, :]
        if is_new:
            incl = jnp.where(row <= col, 1.0, 0.0).astype(F32)
            cnew = _dot_hi(lft, incl)
            for i in range(ts):
                cq_ref[i * C_HEADS:(i + 1) * C_HEADS, :] = jnp.broadcast_to(cnew[:, i:i + 1], (C_HEADS, PAGE))
            suf = -cnew
        else:
            later = jnp.where(row > col, 1.0, 0.0).astype(F32)
            suf = _dot_hi(lft, later) + carry_ref[...]
            carry_ref[...] += jnp.sum(lft, axis=1, keepdims=True)
        z = z + cq_ref[...] + jnp.concatenate([suf] * ts, axis=0)
        if is_new:
            rr = lax.broadcasted_iota(jnp.int32, (nrow, PAGE), 0)
            cc = lax.broadcasted_iota(jnp.int32, (nrow, PAGE), 1)
            z = jnp.where(cc * C_HEADS <= rr, z, NEG)
        m_new, l_new, acc_new = _softmax_step(z, m_ref[...], l_ref[...], acc_ref[...], v.astype(BF16))
        m_ref[...] = m_new
        l_ref[...] = l_new
        acc_ref[...] = acc_new

    @pl.when(p == 0)
    def _():
        hmask = (lax.broadcasted_iota(jnp.int32, (C_HEADS, width), 1) // HEAD_DIM
                 == lax.broadcasted_iota(jnp.int32, (C_HEADS, width), 0))
        q = q_ref[0] * (HEAD_DIM ** -0.5)
        for i in range(ts):
            qbd_ref[i * C_HEADS:(i + 1) * C_HEADS, :] = jnp.where(
                hmask, jnp.broadcast_to(q[i:i + 1, :], (C_HEADS, width)), 0.0)
        kpad_ref[0:ts, :] = kn_ref[0]
        vpad_ref[0:ts, :] = vn_ref[0]
        lfpad_ref[...] = jnp.zeros_like(lfpad_ref)
        lfpad_ref[0:ts, :] = lfn_ref[0]
        m_ref[...] = jnp.full(m_ref.shape, NEG, F32)
        l_ref[...] = jnp.zeros_like(l_ref)
        acc_ref[...] = jnp.zeros_like(acc_ref)
        carry_ref[...] = jnp.zeros_like(carry_ref)
        step(kpad_ref[...], vpad_ref[...], True)

    @pl.when(p > 0)
    def _():
        lfpad_ref[:, 0:C_HEADS] = lfp_ref[...]
        step(kp_ref[...], vp_ref[...], False)

    @pl.when(p == pl.num_programs(1) - 1)
    def _():
        hmask = (lax.broadcasted_iota(jnp.int32, (C_HEADS, width), 1) // HEAD_DIM
                 == lax.broadcasted_iota(jnp.int32, (C_HEADS, width), 0))
        o = acc_ref[...] / l_ref[...]
        for i in range(ts):
            blk = jnp.where(hmask, o[i * C_HEADS:(i + 1) * C_HEADS, :], 0.0)
            o_ref[0, i:i + 1, :] = jnp.sum(blk, axis=0, keepdims=True)


def _fox_s(q, kn, vn, lfn, cache_k, cache_v, cache_lf, layer, page_table):
    b, ts, width = q.shape
    npg = page_table.shape[1]
    nrow = ts * C_HEADS

    def page_idx(i, p, pt):
        return (layer, pt[i, npg - jnp.maximum(p, 1)], 0, 0)

    def seq_idx(i, p, pt):
        return (i, 0, 0)

    grid_spec = pltpu.PrefetchScalarGridSpec(
        num_scalar_prefetch=1, grid=(b, npg + 1),
        in_specs=[pl.BlockSpec((1, ts, width), seq_idx), pl.BlockSpec((1, ts, width), seq_idx),
                  pl.BlockSpec((1, ts, width), seq_idx), pl.BlockSpec((1, ts, LANES), seq_idx),
                  pl.BlockSpec((None, None, PAGE, width), page_idx),
                  pl.BlockSpec((None, None, PAGE, width), page_idx),
                  pl.BlockSpec((None, None, PAGE, C_HEADS), page_idx)],
        out_specs=pl.BlockSpec((1, ts, width), seq_idx),
        scratch_shapes=[pltpu.VMEM((nrow, width), F32), pltpu.VMEM((PAGE, width), F32),
                        pltpu.VMEM((PAGE, width), F32), pltpu.VMEM((PAGE, LANES), F32),
                        pltpu.VMEM((nrow, 1), F32), pltpu.VMEM((nrow, 1), F32),
                        pltpu.VMEM((nrow, width), F32), pltpu.VMEM((C_HEADS, PAGE), F32),
                        pltpu.VMEM((nrow, PAGE), F32)])
    return pl.pallas_call(
        functools.partial(_fox_s_kernel, ts=ts), grid_spec=grid_spec,
        out_shape=jax.ShapeDtypeStruct((b, ts, width), F32),
        compiler_params=_cparams(("arbitrary", "arbitrary")), name="attn_fox_sample")(
            page_table, q, kn, vn, lfn, cache_k, cache_v, cache_lf)


def _diff_s_kernel(pt_ref, q_ref, kn_ref, vn_ref, kp_ref, vp_ref, bias_ref, lamp_ref, subln_ref, o_ref,
                   qbd_ref, kpad_ref, vpad_ref, m_ref, l_ref, acc_ref, *, ts, lam_init):
    b = pl.program_id(0)
    p = pl.program_id(1)
    last = pl.num_programs(1) - 1
    ne = 2 * B_HEADS
    nrow = ts * ne
    width = ne * HEAD_DIM

    @pl.when(jnp.logical_and(b == 0, p == 0))
    def _():
        kpad_ref[...] = jnp.zeros_like(kpad_ref)
        vpad_ref[...] = jnp.zeros_like(vpad_ref)

    @pl.when(p == 0)
    def _():
        emask = (lax.broadcasted_iota(jnp.int32, (ne, width), 1) // HEAD_DIM
                 == lax.broadcasted_iota(jnp.int32, (ne, width), 0))
        q = q_ref[0] * (HEAD_DIM ** -0.5)
        for i in range(ts):
            qbd_ref[i * ne:(i + 1) * ne, :] = jnp.where(emask, jnp.broadcast_to(q[i:i + 1, :], (ne, width)), 0.0)
        m_ref[...] = jnp.full(m_ref.shape, NEG, F32)
        l_ref[...] = jnp.zeros_like(l_ref)
        acc_ref[...] = jnp.zeros_like(acc_ref)

    def step(k, v, is_new):
        z = _dot_nt(qbd_ref[...].astype(BF16), k.astype(BF16)) + bias_ref[0]
        if is_new:
            rr = lax.broadcasted_iota(jnp.int32, (nrow, PAGE), 0)
            cc = lax.broadcasted_iota(jnp.int32, (nrow, PAGE), 1)
            z = jnp.where(cc * ne <= rr, z, NEG)
        m_new, l_new, acc_new = _softmax_step(z, m_ref[...], l_ref[...], acc_ref[...], v.astype(BF16))
        m_ref[...] = m_new
        l_ref[...] = l_new
        acc_ref[...] = acc_new

    @pl.when(p < last)
    def _():
        step(kp_ref[...], vp_ref[...], False)

    @pl.when(p == last)
    def _():
        kpad_ref[0:ts, :] = kn_ref[0]
        vpad_ref[0:ts, :] = vn_ref[0]
        step(kpad_ref[...], vpad_ref[...], True)
        lp = lamp_ref[...]
        lam = (jnp.exp(jnp.sum(lp[0:1] * lp[1:2], axis=1, keepdims=True))
               - jnp.exp(jnp.sum(lp[2:3] * lp[3:4], axis=1, keepdims=True)) + lam_init)
        o = acc_ref[...] / l_ref[...]
        r8 = lax.broadcasted_iota(jnp.int32, (ne, width), 0)
        c8 = lax.broadcasted_iota(jnp.int32, (ne, width), 1)
        coef = jnp.where(r8 % 2 == 0, 1.0, -lam)
        coef = jnp.where(c8 // B_VDIM == r8 // 2, coef, 0.0)
        for i in range(ts):
            orow = jnp.sum(o[i * ne:(i + 1) * ne, :] * coef, axis=0, keepdims=True)
            for n in range(B_HEADS):
                blk = orow[:, n * B_VDIM:(n + 1) * B_VDIM]
                o_ref[0, i:i + 1, n * B_VDIM:(n + 1) * B_VDIM] = _rms(blk, subln_ref[...]) * (1.0 - lam_init)


def _diff_s(q, kn, vn, cache_k, cache_v, layer, page_table, bias, lamp, subln, lam_init):
    b, ts, width = q.shape
    npg = page_table.shape[1]
    nrow = ts * 2 * B_HEADS

    def page_idx(i, p, pt):
        return (layer, pt[i, jnp.minimum(p, npg - 1)], 0, 0)

    def seq_idx(i, p, pt):
        return (i, 0, 0)

    grid_spec = pltpu.PrefetchScalarGridSpec(
        num_scalar_prefetch=1, grid=(b, npg + 1),
        in_specs=[pl.BlockSpec((1, ts, width), seq_idx), pl.BlockSpec((1, ts, width), seq_idx),
                  pl.BlockSpec((1, ts, width), seq_idx),
                  pl.BlockSpec((None, None, PAGE, width), page_idx),
                  pl.BlockSpec((None, None, PAGE, width), page_idx),
                  pl.BlockSpec((1, nrow, PAGE), lambda i, p, pt: (p, 0, 0)),
                  pl.BlockSpec(lamp.shape, lambda i, p, pt: (0, 0)),
                  pl.BlockSpec(subln.shape, lambda i, p, pt: (0, 0))],
        out_specs=pl.BlockSpec((1, ts, width), seq_idx),
        scratch_shapes=[pltpu.VMEM((nrow, width), F32), pltpu.VMEM((PAGE, width), F32),
                        pltpu.VMEM((PAGE, width), F32), pltpu.VMEM((nrow, 1), F32),
                        pltpu.VMEM((nrow, 1), F32), pltpu.VMEM((nrow, width), F32)])
    return pl.pallas_call(
        functools.partial(_diff_s_kernel, ts=ts, lam_init=lam_init), grid_spec=grid_spec,
        out_shape=jax.ShapeDtypeStruct((b, ts, width), F32),
        compiler_params=_cparams(("arbitrary", "arbitrary")), name="attn_diff_sample")(
            page_table, q, kn, vn, cache_k, cache_v, bias, lamp, subln)


def _dsa_s_kernel(pt_ref, qa_ref, qi_ref, wi_ref, kn_ref, vn_ref, kin_ref, kp_ref, vp_ref, kip_ref, bias_ref,
                  o_ref, kst_ref, vst_ref, kipad_ref, key_ref, am_ref, *, ts, n_sel, nchunk):
    b = pl.program_id(0)
    p = pl.program_id(1)
    last = nchunk - 1
    rpad = 8

    @pl.when(jnp.logical_and(b == 0, p == 0))
    def _():
        kst_ref[last] = jnp.zeros((PAGE, LANES), F32)
        vst_ref[last] = jnp.zeros((PAGE, LANES), F32)
        kipad_ref[...] = jnp.zeros_like(kipad_ref)

    def index_scores(kidx):
        s32 = jnp.maximum(_dot_nt(qi_ref[0].astype(BF16), kidx.astype(BF16)), 0.0) * wi_ref[0]
        s = s32[0:rpad]
        for h in range(1, IDX_HEADS):
            s = s + s32[h * rpad:(h + 1) * rpad]
        return s

    @pl.when(p < last)
    def _():
        kst_ref[p] = kp_ref[...]
        vst_ref[p] = vp_ref[...]
        key_ref[p] = _sort_key(index_scores(kip_ref[...]))

    @pl.when(p == last)
    def _():
        kst_ref[last, 0:ts, :] = kn_ref[0]
        vst_ref[last, 0:ts, :] = vn_ref[0]
        kipad_ref[0:ts, :] = kin_ref[0]
        rr = lax.broadcasted_iota(jnp.int32, (rpad, PAGE), 0)
        cc = lax.broadcasted_iota(jnp.int32, (rpad, PAGE), 1)
        causal = jnp.logical_and(cc <= rr, cc < ts)
        key_ref[last] = _sort_key(jnp.where(causal, index_scores(kipad_ref[...]), -jnp.inf))

        def count(pred):
            acc = jnp.zeros((rpad, PAGE), F32)
            for c in range(nchunk):
                acc = acc + jnp.where(pred(key_ref[c]), 1.0, 0.0)
            return jnp.sum(acc, axis=1, keepdims=True)

        thr = _kth_largest(lambda trial: count(lambda kk: kk >= trial), n_sel, (rpad, 1))
        need = n_sel - count(lambda kk: kk > thr)
        r2 = lax.broadcasted_iota(jnp.int32, (PAGE, PAGE), 0)
        c2 = lax.broadcasted_iota(jnp.int32, (PAGE, PAGE), 1)
        ustrict = jnp.where(r2 < c2, 1.0, 0.0).astype(BF16)
        carry = jnp.zeros((rpad, 1), F32)
        for c in range(nchunk):
            kk = key_ref[c]
            eq = kk == thr
            eqf = jnp.where(eq, 1.0, 0.0)
            prefix = _dot(eqf.astype(BF16), ustrict) + carry
            sel = jnp.logical_or(kk > thr, jnp.logical_and(eq, prefix < need))
            if c == last:
                sel = jnp.logical_and(sel, causal)
            am_ref[c] = jnp.where(sel, 0.0, NEG)
            carry = carry + jnp.sum(eqf, axis=1, keepdims=True)

        nrow = A_HEADS * rpad
        qbd = qa_ref[0].astype(BF16)

        def attend(c, st):
            m_old, l_old, acc_old = st
            am = am_ref[c]
            z = _dot_nt(qbd, kst_ref[c].astype(BF16)) + bias_ref[c] + jnp.concatenate([am] * A_HEADS, axis=0)
            return _softmax_step(z, m_old, l_old, acc_old, vst_ref[c].astype(BF16))

        st = (jnp.full((nrow, 1), NEG, F32), jnp.zeros((nrow, 1), F32), jnp.zeros((nrow, LANES), F32))
        _, l_fin, acc_fin = lax.fori_loop(0, nchunk, attend, st)
        o = acc_fin / l_fin
        lane = lax.broadcasted_iota(jnp.int32, (rpad, LANES), 1)
        for blk in range(A_HEADS // 2):
            grp = (2 * blk) // (A_HEADS // A_KV_HEADS)
            lo = o[(2 * blk) * rpad:(2 * blk + 1) * rpad]
            hi = o[(2 * blk + 1) * rpad:(2 * blk + 2) * rpad]
            if grp == 0:
                hi = pltpu.roll(hi, HEAD_DIM, axis=1)
            else:
                lo = pltpu.roll(lo, HEAD_DIM, axis=1)
            res = jnp.where(lane < HEAD_DIM, lo, hi)
            o_ref[0, :, blk * LANES:(blk + 1) * LANES] = res[0:ts]


def _dsa_s(qa_bd, qi32, w32, kn, vn, kin, cache_k, cache_v, cache_ki, layer, page_table, bias, n_sel, ts):
    b = qa_bd.shape[0]
    npg = page_table.shape[1]
    nchunk = npg + 1
    width = A_HEADS * HEAD_DIM

    def page_idx(i, p, pt):
        return (layer, pt[i, jnp.minimum(p, npg - 1)], 0, 0)

    def seq_idx(i, p, pt):
        return (i, 0, 0)

    grid_spec = pltpu.PrefetchScalarGridSpec(
        num_scalar_prefetch=1, grid=(b, nchunk),
        in_specs=[pl.BlockSpec((1,) + qa_bd.shape[1:], seq_idx), pl.BlockSpec((1,) + qi32.shape[1:], seq_idx),
                  pl.BlockSpec((1,) + w32.shape[1:], seq_idx),
                  pl.BlockSpec((1, ts, LANES), seq_idx), pl.BlockSpec((1, ts, LANES), seq_idx),
                  pl.BlockSpec((1, ts, IDX_DIM), seq_idx),
                  pl.BlockSpec((None, None, PAGE, LANES), page_idx),
                  pl.BlockSpec((None, None, PAGE, LANES), page_idx),
                  pl.BlockSpec((None, None, PAGE, IDX_DIM), page_idx),
                  pl.BlockSpec(bias.shape, lambda i, p, pt: (0, 0, 0))],
        out_specs=pl.BlockSpec((1, ts, width), seq_idx),
        scratch_shapes=[pltpu.VMEM((nchunk, PAGE, LANES), F32), pltpu.VMEM((nchunk, PAGE, LANES), F32),
                        pltpu.VMEM((PAGE, IDX_DIM), F32),
                        pltpu.VMEM((nchunk, 8, PAGE), jnp.int32), pltpu.VMEM((nchunk, 8, PAGE), F32)])
    return pl.pallas_call(
        functools.partial(_dsa_s_kernel, ts=ts, n_sel=n_sel, nchunk=nchunk), grid_spec=grid_spec,
        out_shape=jax.ShapeDtypeStruct((b, ts, width), F32),
        compiler_params=_cparams(("arbitrary", "arbitrary")), name="attn_dsa_sample")(
            page_table, qa_bd, qi32, w32, kn, vn, kin, cache_k, cache_v, cache_ki, bias)


def _t5_bucket(rel):
    n = jnp.maximum(rel, 0)
    exact = N_BUCKETS // 2
    nf = jnp.maximum(n, 1).astype(F32)
    log_b = exact + (jnp.log(nf / exact) / math.log(MAX_DISTANCE / exact) * (N_BUCKETS - exact)).astype(jnp.int32)
    return jnp.where(n < exact, n, jnp.minimum(log_b, N_BUCKETS - 1))


def _bias_tiles(tab, tb):
    i = jnp.arange(tb, dtype=jnp.int32)[:, None]
    j = jnp.arange(tb, dtype=jnp.int32)[None, :]
    rel = jnp.stack([d * tb + i - j for d in range(3)])
    return jnp.moveaxis(tab.astype(F32)[_t5_bucket(rel)], -1, 0)


def _bias_rows(tab, qpos, kpos_pad):
    rel = qpos[:, None] - kpos_pad[None, :]
    return jnp.moveaxis(tab.astype(F32)[_t5_bucket(rel)], -1, 0)


def _split(w, sizes):
    offs = np.cumsum((0,) + tuple(sizes))
    return [w[:, int(offs[i]):int(offs[i + 1])] for i in range(len(sizes))]


def _pad_cols(w, n):
    return jnp.pad(w, ((0, 0), (0, n - w.shape[1])))


def _row_tile(m, want):
    t = min(m, want)
    while m % t:
        t //= 2
    return t


def kernel(x_prompt, x_sample, cache_a_k, cache_a_v, cache_a_kidx, cache_b_k, cache_b_v, cache_c_k, cache_c_v,
           cache_c_logf, state_conv, page_table, rel_bias_table, w_in_even, w_out_even, lambda_q1, lambda_k1,
           lambda_q2, lambda_k2, diff_subln, w_in_odd, b_forget, w_out_odd, norm_mix_pre, norm_mix_post,
           norm_ffn_pre, norm_ffn_post, w_gate_up, w_conv, b_conv, w_down):
    bp, tp, d = x_prompt.shape
    bs, ts, _ = x_sample.shape
    depth = w_gate_up.shape[0]
    n_pool = cache_a_k.shape[1]
    npg = page_table.shape[1]
    past = npg * PAGE
    mp, ms = bp * tp, bs * ts
    tb_dsa = LANES
    tb_attn = min(256, tp)
    nsel_p = min(TOPK_MAX, tp // 4)
    nsel_s = min(TOPK_MAX, (past + ts) // 4)
    tm_p = _row_tile(mp, 512)
    tm_s = _row_tile(ms, 512)
    tf = D_FF // 2
    dff = D_FF

    xp = x_prompt.reshape(mp, d)
    xs = x_sample.reshape(ms, d)
    tab_a = rel_bias_table[:, :A_HEADS]
    tab_b = rel_bias_table[:, A_HEADS:]
    bias_a_p = _bias_tiles(tab_a, tb_dsa)
    bias_b_p = _bias_tiles(tab_b, tb_attn)
    qpos_s = past + jnp.arange(ts, dtype=jnp.int32)
    kpos_s = jnp.arange(past + PAGE, dtype=jnp.int32)
    rows_a = _bias_rows(tab_a, qpos_s, kpos_s)
    rows_b = _bias_rows(tab_b, qpos_s, kpos_s)
    bias_a_s = jnp.pad(rows_a, ((0, 0), (0, 8 - ts), (0, 0))).reshape(A_HEADS * 8, npg + 1, PAGE)
    bias_a_s = jnp.moveaxis(bias_a_s, 1, 0)
    bias_b_s = jnp.repeat(jnp.moveaxis(rows_b, 0, 1), 2, axis=1).reshape(ts * 2 * B_HEADS, npg + 1, PAGE)
    bias_b_s = jnp.moveaxis(bias_b_s, 1, 0)

    even_rows_p, even_rows_s, odd_rows_p, odd_rows_s, conv_p, conv_s = [], [], [], [], [], []
    for l in range(depth):
        if l % 2 == 0:
            e = l // 2
            lam_init = 0.8 - 0.6 * math.exp(-0.3 * l)
            ws = _split(w_in_even[e], EVEN_SPLIT)
            ws[5] = _pad_cols(ws[5], LANES)
            ws = [w.astype(BF16) for w in ws]
            wo = w_out_even[e].astype(BF16)
            wo_parts = [wo[:A_HEADS * HEAD_DIM], wo[A_HEADS * HEAD_DIM:]]
            lamp = jnp.stack([lambda_q1[e], lambda_k1[e], lambda_q2[e], lambda_k2[e]]).astype(F32)
            subln = diff_subln[e].reshape(1, B_VDIM)

            q_a, k_a, v_a, q_i, k_i, w_i, q_b, k_b, v_b = _rms_proj(xp, norm_mix_pre[l], ws, tm_p)
            r3 = lambda a: a.reshape(bp, tp, a.shape[1])
            o_a = _dsa_p(r3(q_a), r3(k_a), r3(v_a), r3(q_i), r3(k_i), r3(w_i), bias_a_p, nsel_p)
            o_b = _flash2("diff", r3(q_b), r3(k_b), r3(v_b), (bias_b_p, lamp, subln), tb_attn, lam_init)
            xp = _proj_post(xp, norm_mix_post[l], [o_a.reshape(mp, -1), o_b.reshape(mp, -1)], wo_parts, tm_p)
            even_rows_p.append((k_a.reshape(bp, tp, A_KV_HEADS, HEAD_DIM), v_a.reshape(bp, tp, A_KV_HEADS, HEAD_DIM),
                                k_i.reshape(bp, tp, IDX_DIM), k_b.reshape(bp, tp, B_HEADS, 2 * HEAD_DIM),
                                v_b.reshape(bp, tp, B_HEADS, B_VDIM)))

            q_a, k_a, v_a, q_i, k_i, w_i, q_b, k_b, v_b = _rms_proj(xs, norm_mix_pre[l], ws, tm_s)
            s3 = lambda a: a.reshape(bs, ts, a.shape[1])
            grp = A_HEADS // A_KV_HEADS
            qa4 = (q_a * (HEAD_DIM ** -0.5)).reshape(bs, ts, A_HEADS, HEAD_DIM)
            qa4 = jnp.pad(jnp.moveaxis(qa4, 1, 2), ((0, 0), (0, 0), (0, 8 - ts), (0, 0)))
            zeros = jnp.zeros_like(qa4)
            qa_bd = jnp.concatenate([jnp.concatenate([qa4[:, :grp], zeros[:, :grp]], axis=-1),
                                     jnp.concatenate([zeros[:, grp:], qa4[:, grp:]], axis=-1)], axis=1)
            qa_bd = qa_bd.reshape(bs, A_HEADS * 8, LANES)
            qi4 = jnp.moveaxis(q_i.reshape(bs, ts, IDX_HEADS, IDX_DIM), 1, 2)
            qi32 = jnp.pad(qi4, ((0, 0), (0, 0), (0, 8 - ts), (0, 0))).reshape(bs, IDX_HEADS * 8, IDX_DIM)
            w4 = jnp.moveaxis(w_i[:, :IDX_HEADS].reshape(bs, ts, IDX_HEADS), 1, 2) * (IDX_DIM ** -0.5 * IDX_HEADS ** -0.5)
            w32 = jnp.pad(w4, ((0, 0), (0, 0), (0, 8 - ts))).reshape(bs, IDX_HEADS * 8, 1)
            w32 = jnp.broadcast_to(w32, (bs, IDX_HEADS * 8, PAGE))
            o_a = _dsa_s(qa_bd, qi32, w32, s3(k_a), s3(v_a), s3(k_i),
                         cache_a_k.reshape(-1, n_pool, PAGE, LANES), cache_a_v.reshape(-1, n_pool, PAGE, LANES),
                         cache_a_kidx, e, page_table, bias_a_s, nsel_s, ts)
            o_b = _diff_s(s3(q_b), s3(k_b), s3(v_b), cache_b_k.reshape(-1, n_pool, PAGE, 512),
                          cache_b_v.reshape(-1, n_pool, PAGE, 512), e, page_table, bias_b_s, lamp, subln, lam_init)
            xs = _proj_post(xs, norm_mix_post[l], [o_a.reshape(ms, -1), o_b.reshape(ms, -1)], wo_parts, tm_s)
            even_rows_s.append((k_a.reshape(bs, ts, A_KV_HEADS, HEAD_DIM), v_a.reshape(bs, ts, A_KV_HEADS, HEAD_DIM),
                                k_i.reshape(bs, ts, IDX_DIM), k_b.reshape(bs, ts, B_HEADS, 2 * HEAD_DIM),
                                v_b.reshape(bs, ts, B_HEADS, B_VDIM)))
        else:
            o = l // 2
            ws = _split(w_in_odd[o], ODD_SPLIT)
            ws[3] = _pad_cols(ws[3], LANES)
            ws = [w.astype(BF16) for w in ws]
            bfp = _pad_cols(b_forget[o].reshape(1, C_HEADS), LANES)
            wo = [w_out_odd[o].astype(BF16)]

            q, k, v, lf = _rms_proj(xp, norm_mix_pre[l], ws, tm_p, logsig_bias=bfp)
            r3 = lambda a: a.reshape(bp, tp, a.shape[1])
            cum, cumt = _cumsum(r3(lf), tb_attn)
            o_c = _flash2("fox", r3(q), r3(k), r3(v), (cum, cumt), tb_attn)
            xp = _proj_post(xp, norm_mix_post[l], [o_c.reshape(mp, -1)], wo, tm_p)
            odd_rows_p.append((k.reshape(bp, tp, C_HEADS, HEAD_DIM), v.reshape(bp, tp, C_HEADS, HEAD_DIM),
                               lf[:, :C_HEADS].reshape(bp, tp, C_HEADS)))

            q, k, v, lf = _rms_proj(xs, norm_mix_pre[l], ws, tm_s, logsig_bias=bfp)
            s3 = lambda a: a.reshape(bs, ts, a.shape[1])
            o_c = _fox_s(s3(q), s3(k), s3(v), s3(lf), cache_c_k.reshape(-1, n_pool, PAGE, 1024),
                         cache_c_v.reshape(-1, n_pool, PAGE, 1024), cache_c_logf, o, page_table)
            xs = _proj_post(xs, norm_mix_post[l], [o_c.reshape(ms, -1)], wo, tm_s)
            odd_rows_s.append((k.reshape(bs, ts, C_HEADS, HEAD_DIM), v.reshape(bs, ts, C_HEADS, HEAD_DIM),
                               lf[:, :C_HEADS].reshape(bs, ts, C_HEADS)))

        wgu = w_gate_up[l].astype(BF16)
        wg, wu = wgu[:, :dff], wgu[:, dff:]
        wd = w_down[l].astype(BF16)
        xp, tail = _ffn(xp, norm_ffn_pre[l], norm_ffn_post[l], wg, wu, w_conv[l], b_conv[l], wd,
                        tm_p, tf, tp // tm_p)
        conv_p.append(tail[:, 8 - (CONV_W - 1):, :])
        st = state_conv[l]
        prev1 = jnp.concatenate([st[:, 1:2], jnp.zeros((bs, ts - 1, dff), F32)], axis=1).reshape(ms, dff)
        prev2 = jnp.concatenate([st, jnp.zeros((bs, ts - 2, dff), F32)], axis=1).reshape(ms, dff)
        xs, gfull = _ffn(xs, norm_ffn_pre[l], norm_ffn_post[l], wg, wu, w_conv[l], b_conv[l], wd,
                         tm_s, tf, ts, prev=(prev1, prev2))
        conv_s.append(gfull.reshape(bs, ts, dff)[:, ts - (CONV_W - 1):, :])

    def stack(rows, i):
        return jnp.stack([r[i] for r in rows])

    ev_p = [stack(even_rows_p, i) for i in range(5)]
    ev_s = [stack(even_rows_s, i) for i in range(5)]
    od_p = [stack(odd_rows_p, i) for i in range(3)]
    od_s = [stack(odd_rows_s, i) for i in range(3)]
    return (xp.reshape(bp, tp, d), xs.reshape(bs, ts, d),
            ev_p[0], ev_s[0], ev_p[1], ev_s[1], ev_p[2], ev_s[2], ev_p[3], ev_s[3], ev_p[4], ev_s[4],
            od_p[0], od_s[0], od_p[1], od_s[1], od_p[2], od_s[2],
            jnp.stack(conv_p), jnp.stack(conv_s))
```

```python
import functools
import math

import jax
import jax.numpy as jnp
import numpy as np
from jax import lax
from jax.experimental import pallas as pl
from jax.experimental.pallas import tpu as pltpu

F32 = jnp.float32
BF16 = jnp.bfloat16
HIGHEST = lax.Precision.HIGHEST

D_MODEL = 1024
HEAD_DIM = 64
A_HEADS = 8
A_KV_HEADS = 2
A_GROUP = A_HEADS // A_KV_HEADS
IDX_HEADS = 4
IDX_DIM = 64
TOPK_MAX = 256
B_HEADS = 4
B_VDIM = 128
C_HEADS = 16
N_BUCKETS = 32
MAX_DISTANCE = 128
D_FF = 2816
CONV_W = 3
EPS = 1e-6
PAGE = 128

LANES = 128
SUBLANES = 8
NEG = -1e30
INT_MIN = -2147483648
VMEM_LIMIT = 48 * 1024 * 1024

EVEN_SPLIT = (512, 128, 128, 256, 64, 4, 512, 512, 512)
ODD_SPLIT = (1024, 1024, 1024, 16)

NT_DIMS = (((1,), (1,)), ((), ()))


def _cparams(sem):
    return pltpu.CompilerParams(dimension_semantics=sem, vmem_limit_bytes=VMEM_LIMIT)


def _rms(x, g):
    return x * lax.rsqrt(jnp.mean(x * x, axis=-1, keepdims=True) + EPS) * g


def _log_sigmoid(x):
    return -(jnp.maximum(-x, 0.0) + jnp.log1p(jnp.exp(-jnp.abs(x))))


def _gelu_tanh(x):
    c = math.sqrt(2.0 / math.pi)
    return x * (0.5 * (1.0 + jnp.tanh(c * (x + 0.044715 * (x * x * x)))))


def _dot(a, b):
    return jnp.dot(a, b, preferred_element_type=F32)


def _dot_nt(a, b):
    return lax.dot_general(a, b, NT_DIMS, preferred_element_type=F32)


def _dot_hi(a, b):
    return jnp.dot(a, b, preferred_element_type=F32, precision=HIGHEST)


def _softmax_rows(z, m_old, l_old, acc_old, pv):
    m_new = jnp.maximum(m_old, jnp.max(z, axis=1, keepdims=True))
    alpha = jnp.exp(m_old - m_new)
    p = jnp.exp(z - m_new)
    l_new = alpha * l_old + jnp.sum(p, axis=1, keepdims=True)
    acc_new = alpha * acc_old + pv(p.astype(BF16))
    return m_new, l_new, acc_new


def _softmax_cols(zt, m_old, l_old, acc_old, vt):
    m_new = jnp.maximum(m_old, jnp.max(zt, axis=0, keepdims=True))
    alpha = jnp.exp(m_old - m_new)
    p = jnp.exp(zt - m_new)
    l_new = alpha * l_old + jnp.sum(p, axis=0, keepdims=True)
    acc_new = alpha * acc_old + _dot(vt, p.astype(BF16))
    return m_new, l_new, acc_new


def _sort_key(s):
    s = jnp.where(s == 0.0, 0.0, s)
    bits = pltpu.bitcast(s, jnp.int32)
    return jnp.where(bits < 0, bits ^ jnp.int32(0x7FFFFFFF), bits)


def _kth_largest(count_ge, n_sel, shape):
    def step(it, ans):
        bit = lax.shift_left(jnp.int32(1), jnp.int32(31) - it)
        trial = ans | bit
        cnt = count_ge(trial ^ jnp.int32(INT_MIN))
        return jnp.where(cnt >= n_sel, trial, ans)

    ans = lax.fori_loop(0, 32, step, jnp.zeros(shape, jnp.int32))
    return ans ^ jnp.int32(INT_MIN)


def _rms_proj_kernel(x_ref, g_ref, *refs, n_w, logsig_last):
    w_refs = refs[:n_w]
    if logsig_last:
        bias_ref = refs[n_w]
        o_refs = refs[n_w + 1:]
    else:
        o_refs = refs[n_w:]
    h = _rms(x_ref[...], g_ref[...]).astype(BF16)
    for idx in range(n_w):
        y = _dot(h, w_refs[idx][...])
        if logsig_last and idx == n_w - 1:
            y = _log_sigmoid(y + bias_ref[...])
        o_refs[idx][...] = y


def _rms_proj(x2d, g, w_list, tm, logsig_bias=None):
    m, d = x2d.shape
    n_w = len(w_list)
    in_specs = [pl.BlockSpec((tm, d), lambda i: (i, 0)), pl.BlockSpec((1, d), lambda i: (0, 0))]
    in_specs += [pl.BlockSpec(w.shape, lambda i: (0, 0)) for w in w_list]
    args = [x2d, g.reshape(1, d)] + list(w_list)
    if logsig_bias is not None:
        in_specs.append(pl.BlockSpec(logsig_bias.shape, lambda i: (0, 0)))
        args.append(logsig_bias)
    out_shape = [jax.ShapeDtypeStruct((m, w.shape[1]), F32) for w in w_list]
    out_specs = [pl.BlockSpec((tm, w.shape[1]), lambda i: (i, 0)) for w in w_list]
    return pl.pallas_call(
        functools.partial(_rms_proj_kernel, n_w=n_w, logsig_last=logsig_bias is not None),
        grid=(m // tm,), in_specs=in_specs, out_specs=out_specs, out_shape=out_shape,
        compiler_params=_cparams(("parallel",)), name="rms_proj")(*args)


def _proj_post_kernel(x_ref, g_ref, *refs, n_in):
    o_refs = refs[:n_in]
    w_refs = refs[n_in:2 * n_in]
    out_ref = refs[2 * n_in]
    acc = None
    for o, w in zip(o_refs, w_refs):
        t = _dot(o[...].astype(BF16), w[...])
        acc = t if acc is None else acc + t
    out_ref[...] = x_ref[...] + _rms(acc, g_ref[...])


def _proj_post(x2d, g, o_list, w_list, tm):
    m, d = x2d.shape
    n_in = len(o_list)
    in_specs = [pl.BlockSpec((tm, d), lambda i: (i, 0)), pl.BlockSpec((1, d), lambda i: (0, 0))]
    in_specs += [pl.BlockSpec((tm, o.shape[1]), lambda i: (i, 0)) for o in o_list]
    in_specs += [pl.BlockSpec(w.shape, lambda i: (0, 0)) for w in w_list]
    return pl.pallas_call(
        functools.partial(_proj_post_kernel, n_in=n_in),
        grid=(m // tm,), in_specs=in_specs, out_specs=pl.BlockSpec((tm, d), lambda i: (i, 0)),
        out_shape=jax.ShapeDtypeStruct((m, d), F32),
        compiler_params=_cparams(("parallel",)), name="proj_post")(x2d, g.reshape(1, d), *o_list, *w_list)


def _ffn_kernel(*refs, tm, tiles_per_seq, sample):
    if sample:
        (x_ref, gpre_ref, gpost_ref, wg_ref, wu_ref, wc_ref, bc_ref, wd_ref, p1_ref, p2_ref,
         y_ref, gout_ref, h_ref, acc_ref, gs_ref) = refs
    else:
        (x_ref, gpre_ref, gpost_ref, wg_ref, wu_ref, wc_ref, bc_ref, wd_ref,
         y_ref, gout_ref, h_ref, acc_ref, gs_ref, halo_ref) = refs
    i = pl.program_id(0)
    c = pl.program_id(1)

    @pl.when(c == 0)
    def _():
        h_ref[...] = _rms(x_ref[...], gpre_ref[...]).astype(BF16)
        acc_ref[...] = jnp.zeros_like(acc_ref)

    h = h_ref[...]
    g = _dot(h, wg_ref[...])
    u = _dot(h, wu_ref[...])
    if sample:
        gs_ref[0:8, :] = jnp.zeros((8, g.shape[1]), F32)
    else:
        first = (i % tiles_per_seq) == 0

        @pl.when(first)
        def _():
            gs_ref[0:8, :] = jnp.zeros((8, g.shape[1]), F32)

        @pl.when(jnp.logical_not(first))
        def _():
            gs_ref[0:8, :] = halo_ref[c]

    gs_ref[8:tm + 8, :] = g
    g1 = gs_ref[7:tm + 7, :]
    g2 = gs_ref[6:tm + 6, :]
    if sample:
        t = lax.broadcasted_iota(jnp.int32, (tm, 1), 0) % tiles_per_seq
        g1 = jnp.where(t >= 1, g1, 0.0) + p1_ref[...]
        g2 = jnp.where(t >= 2, g2, 0.0) + p2_ref[...]
        gout_ref[...] = g
    else:
        tail = gs_ref[tm:tm + 8, :]
        halo_ref[c] = tail
        gout_ref[0] = tail
    w = wc_ref[...]
    gc = bc_ref[...] + w[0:1, :] * g2
    gc = gc + w[1:2, :] * g1
    gc = gc + w[2:3, :] * g
    a = _gelu_tanh(gc) * u
    acc_ref[...] += _dot(a.astype(BF16), wd_ref[...])

    @pl.when(c == pl.num_programs(1) - 1)
    def _():
        y_ref[...] = x_ref[...] + _rms(acc_ref[...], gpost_ref[...])


def _ffn(x2d, gpre, gpost, wg, wu, wc, bc, wd, tm, tf, tiles_per_seq, prev=None):
    m, d = x2d.shape
    dff = wg.shape[1]
    nc = dff // tf
    sample = prev is not None
    in_specs = [
        pl.BlockSpec((tm, d), lambda i, c: (i, 0)),
        pl.BlockSpec((1, d), lambda i, c: (0, 0)),
        pl.BlockSpec((1, d), lambda i, c: (0, 0)),
        pl.BlockSpec((d, tf), lambda i, c: (0, c)),
        pl.BlockSpec((d, tf), lambda i, c: (0, c)),
        pl.BlockSpec((CONV_W, tf), lambda i, c: (0, c)),
        pl.BlockSpec((1, tf), lambda i, c: (0, c)),
        pl.BlockSpec((tf, d), lambda i, c: (c, 0)),
    ]
    args = [x2d, gpre.reshape(1, d), gpost.reshape(1, d), wg, wu, wc, bc.reshape(1, dff), wd]
    scratch = [pltpu.VMEM((tm, d), BF16), pltpu.VMEM((tm, d), F32), pltpu.VMEM((tm + 8, tf), F32)]
    if sample:
        in_specs += [pl.BlockSpec((tm, tf), lambda i, c: (i, c))] * 2
        args += list(prev)
        gout_shape = jax.ShapeDtypeStruct((m, dff), F32)
        gout_spec = pl.BlockSpec((tm, tf), lambda i, c: (i, c))
    else:
        gout_shape = jax.ShapeDtypeStruct((m // tm, 8, dff), F32)
        gout_spec = pl.BlockSpec((1, 8, tf), lambda i, c: (i, 0, c))
        scratch.append(pltpu.VMEM((nc, 8, tf), F32))
    return pl.pallas_call(
        functools.partial(_ffn_kernel, tm=tm, tiles_per_seq=tiles_per_seq, sample=sample),
        grid=(m // tm, nc), in_specs=in_specs,
        out_specs=[pl.BlockSpec((tm, d), lambda i, c: (i, 0)), gout_spec],
        out_shape=[jax.ShapeDtypeStruct((m, d), F32), gout_shape],
        scratch_shapes=scratch,
        compiler_params=_cparams(("arbitrary", "arbitrary")), name="conv_ffn")(*args)


def _cumsum_kernel(lf_ref, cum_ref, cumt_ref, *, tb, nchunk):
    row = lax.broadcasted_iota(jnp.int32, (tb, tb), 0)
    col = lax.broadcasted_iota(jnp.int32, (tb, tb), 1)
    lower = jnp.where(col <= row, 1.0, 0.0).astype(F32)
    carry = jnp.zeros((1, LANES), F32)
    for c in range(nchunk):
        x = lf_ref[0, c * tb:(c + 1) * tb, :]
        ct = _dot_hi(lower, x) + carry
        cumt_ref[0, c * tb:(c + 1) * tb, :] = ct
        carry = ct[tb - 1:tb, :]
        cum_ref[0, c] = ct.T[0:C_HEADS, :]


def _cumsum(lf3d, tb):
    b, t, _ = lf3d.shape
    nchunk = t // tb
    return pl.pallas_call(
        functools.partial(_cumsum_kernel, tb=tb, nchunk=nchunk),
        grid=(b,), in_specs=[pl.BlockSpec((1, t, LANES), lambda i: (i, 0, 0))],
        out_specs=[pl.BlockSpec((1, nchunk, C_HEADS, tb), lambda i: (i, 0, 0, 0)),
                   pl.BlockSpec((1, t, LANES), lambda i: (i, 0, 0))],
        out_shape=[jax.ShapeDtypeStruct((b, nchunk, C_HEADS, tb), F32),
                   jax.ShapeDtypeStruct((b, t, LANES), F32)],
        compiler_params=_cparams(("parallel",)), name="logf_cumsum")(lf3d)


def _flash2_kernel(*refs, mode, tb, nk, lam_init):
    if mode == "fox":
        (q_ref, k_ref, v_ref, cum_ref, cumt_ref, o_ref,
         kb_ref, vt_ref, m_ref, l_ref, acc_ref, ck_ref) = refs
    else:
        (q_ref, k_ref, v_ref, bias_ref, lamp_ref, subln_ref, o_ref,
         kb_ref, vt_ref, m_ref, l_ref, acc_ref) = refs
    g = pl.program_id(1)
    qi = pl.program_id(2)
    lane = lax.broadcasted_iota(jnp.int32, (tb, LANES), 1)
    krow = lax.broadcasted_iota(jnp.int32, (tb, tb), 0)
    qcol = lax.broadcasted_iota(jnp.int32, (tb, tb), 1)
    tri = krow <= qcol

    @pl.when(qi == 0)
    def _():
        kb_ref[...] = k_ref[0].astype(BF16)
        for c in range(nk):
            vt_ref[c] = v_ref[0, c * tb:(c + 1) * tb, :].T.astype(BF16)
        if mode == "fox":
            hrow = lax.broadcasted_iota(jnp.int32, (LANES, LANES), 0)
            for a in range(2):
                onehot = jnp.where(hrow == 2 * g + a, 1.0, 0.0).astype(F32)
                ck_ref[a] = _dot_hi(cumt_ref[0], onehot)

    q = q_ref[0] * (HEAD_DIM ** -0.5)
    qa = [jnp.where(lane < HEAD_DIM, q, 0.0).astype(BF16), jnp.where(lane >= HEAD_DIM, q, 0.0).astype(BF16)]
    if mode == "fox":
        cq = [cum_ref[0, qi, pl.ds(2 * g + a, 1), :] for a in range(2)]
    m_ref[...] = jnp.full(m_ref.shape, NEG, F32)
    l_ref[...] = jnp.zeros_like(l_ref)
    acc_ref[...] = jnp.zeros_like(acc_ref)

    def body(c, carry):
        off = pl.multiple_of(c * tb, tb)
        kc = kb_ref[pl.ds(off, tb), :]
        vtc = vt_ref[c]
        valid = jnp.logical_or(c < qi, tri)
        if mode == "diff":
            bt = bias_ref[0, jnp.minimum(qi - c, 2)]
        for a in range(2):
            zt = _dot_nt(kc, qa[a])
            if mode == "fox":
                ck = ck_ref[a, pl.ds(off, tb), :]
                zt = zt + cq[a] - jnp.concatenate([ck] * (tb // LANES), axis=1)
            else:
                zt = zt + bt
            zt = jnp.where(valid, zt, NEG)
            m_new, l_new, acc_new = _softmax_cols(zt, m_ref[a], l_ref[a], acc_ref[a], vtc)
            m_ref[a] = m_new
            l_ref[a] = l_new
            acc_ref[a] = acc_new
        return carry

    lax.fori_loop(0, qi + 1, body, 0)
    o0 = (acc_ref[0] / l_ref[0]).T
    o1 = (acc_ref[1] / l_ref[1]).T
    if mode == "fox":
        o_ref[0] = jnp.where(lane < HEAD_DIM, o0, o1)
    else:
        lp = lamp_ref[...]
        lam = (jnp.exp(jnp.sum(lp[0:1] * lp[1:2], axis=1, keepdims=True))
               - jnp.exp(jnp.sum(lp[2:3] * lp[3:4], axis=1, keepdims=True)) + lam_init)
        o_ref[0] = _rms(o0 - lam * o1, subln_ref[...]) * (1.0 - lam_init)


def _flash2(mode, q, k, v, extra, tb, lam_init=0.0):
    b, t, width = q.shape
    ng = width // LANES
    nq = t // tb
    in_specs = [pl.BlockSpec((1, tb, LANES), lambda i, g, j: (i, j, g)),
                pl.BlockSpec((1, t, LANES), lambda i, g, j: (i, 0, g)),
                pl.BlockSpec((1, t, LANES), lambda i, g, j: (i, 0, g))]
    scratch = [pltpu.VMEM((t, LANES), BF16), pltpu.VMEM((nq, LANES, tb), BF16),
               pltpu.VMEM((2, 1, tb), F32), pltpu.VMEM((2, 1, tb), F32), pltpu.VMEM((2, LANES, tb), F32)]
    if mode == "fox":
        cum, cumt = extra
        in_specs += [pl.BlockSpec((1, nq, C_HEADS, tb), lambda i, g, j: (i, 0, 0, 0)),
                     pl.BlockSpec((1, t, LANES), lambda i, g, j: (i, 0, 0))]
        scratch.append(pltpu.VMEM((2, t, LANES), F32))
    else:
        bias, lamp, subln = extra
        in_specs += [pl.BlockSpec((1, 3, tb, tb), lambda i, g, j: (g, 0, 0, 0)),
                     pl.BlockSpec(lamp.shape, lambda i, g, j: (0, 0)),
                     pl.BlockSpec(subln.shape, lambda i, g, j: (0, 0))]
    return pl.pallas_call(
        functools.partial(_flash2_kernel, mode=mode, tb=tb, nk=nq, lam_init=lam_init),
        grid=(b, ng, nq), in_specs=in_specs,
        out_specs=pl.BlockSpec((1, tb, LANES), lambda i, g, j: (i, j, g)),
        out_shape=jax.ShapeDtypeStruct((b, t, width), F32),
        scratch_shapes=scratch,
        compiler_params=_cparams(("arbitrary", "arbitrary", "arbitrary")),
        name="attn_" + mode)(q, k, v, *extra)


def _dsa_p_kernel(qa_ref, ka_ref, va_ref, qi_ref, ki_ref, wi_ref, bias_ref, o_ref,
                  kab_ref, vat_ref, kib_ref, key_ref, am_ref, qm_ref, m_ref, l_ref, acc_ref, *, n_sel, nk):
    tb = LANES
    qi = pl.program_id(1)
    krow = lax.broadcasted_iota(jnp.int32, (tb, tb), 0)
    qcol = lax.broadcasted_iota(jnp.int32, (tb, tb), 1)
    tri = krow <= qcol
    nchunks = qi + 1

    @pl.when(qi == 0)
    def _():
        kab_ref[...] = ka_ref[0].astype(BF16)
        kib_ref[...] = ki_ref[0].astype(BF16)
        for c in range(nk):
            vat_ref[c] = va_ref[0, c * tb:(c + 1) * tb, :].T.astype(BF16)

    qidx = qi_ref[0]
    qh = [qidx[:, h * IDX_DIM:(h + 1) * IDX_DIM].astype(BF16) for h in range(IDX_HEADS)]
    wt = (wi_ref[0] * (IDX_DIM ** -0.5 * IDX_HEADS ** -0.5)).T
    wrow = [wt[h:h + 1, :] for h in range(IDX_HEADS)]

    def p1(c, carry):
        off = pl.multiple_of(c * tb, tb)
        kc = kib_ref[pl.ds(off, tb), :]
        s = jnp.zeros((tb, tb), F32)
        for h in range(IDX_HEADS):
            s = s + jnp.maximum(_dot_nt(kc, qh[h]), 0.0) * wrow[h]
        s = jnp.where(jnp.logical_or(c < qi, tri), s, -jnp.inf)
        key_ref[c] = _sort_key(s)
        return carry

    lax.fori_loop(0, nchunks, p1, 0)

    def count(pred):
        def cb(c, acc):
            return acc + jnp.where(pred(key_ref[c]), 1.0, 0.0)
        acc = lax.fori_loop(0, nchunks, cb, jnp.zeros((tb, tb), F32))
        return jnp.sum(acc, axis=0, keepdims=True)

    thr = _kth_largest(lambda trial: count(lambda kk: kk >= trial), n_sel, (1, tb))
    need = n_sel - count(lambda kk: kk > thr)
    lstrict = jnp.where(qcol < krow, 1.0, 0.0).astype(BF16)

    def p3(c, carry):
        kk = key_ref[c]
        eq = kk == thr
        eqf = jnp.where(eq, 1.0, 0.0)
        prefix = _dot(lstrict, eqf.astype(BF16)) + carry
        sel = jnp.logical_or(kk > thr, jnp.logical_and(eq, prefix < need))
        sel = jnp.logical_and(sel, jnp.logical_or(c < qi, tri))
        am_ref[c] = jnp.where(sel, 0.0, NEG)
        return carry + jnp.sum(eqf, axis=0, keepdims=True)

    lax.fori_loop(0, nchunks, p3, jnp.zeros((1, tb), F32))

    lane = qcol
    qa = qa_ref[0] * (HEAD_DIM ** -0.5)
    for j in range(A_HEADS):
        blk = qa[:, (j // 2) * LANES:(j // 2 + 1) * LANES]
        grp = j // A_GROUP
        if j % 2 != grp:
            blk = pltpu.roll(blk, HEAD_DIM, axis=1)
        keep = (lane < HEAD_DIM) if grp == 0 else (lane >= HEAD_DIM)
        qm_ref[j] = jnp.where(keep, blk, 0.0).astype(BF16)
    m_ref[...] = jnp.full(m_ref.shape, NEG, F32)
    l_ref[...] = jnp.zeros_like(l_ref)
    acc_ref[...] = jnp.zeros_like(acc_ref)

    def p4(c, carry):
        off = pl.multiple_of(c * tb, tb)
        kc = kab_ref[pl.ds(off, tb), :]
        vtc = vat_ref[c]
        am = am_ref[c]
        bidx = jnp.minimum(qi - c, 2)
        for j in range(A_HEADS):
            zt = _dot_nt(kc, qm_ref[j]) + bias_ref[j, bidx] + am
            m_new, l_new, acc_new = _softmax_cols(zt, m_ref[j], l_ref[j], acc_ref[j], vtc)
            m_ref[j] = m_new
            l_ref[j] = l_new
            acc_ref[j] = acc_new
        return carry

    lax.fori_loop(0, nchunks, p4, 0)
    for c2 in range(A_HEADS // 2):
        grp = (2 * c2) // A_GROUP
        lo = (acc_ref[2 * c2] / l_ref[2 * c2]).T
        hi = (acc_ref[2 * c2 + 1] / l_ref[2 * c2 + 1]).T
        if grp == 0:
            hi = pltpu.roll(hi, HEAD_DIM, axis=1)
        else:
            lo = pltpu.roll(lo, HEAD_DIM, axis=1)
        o_ref[0, :, c2 * LANES:(c2 + 1) * LANES] = jnp.where(lane < HEAD_DIM, lo, hi)


def _dsa_p(q_a, k_a, v_a, q_i, k_i, w_i, bias, n_sel):
    b, t, _ = q_a.shape
    tb = LANES
    nq = t // tb
    in_specs = [pl.BlockSpec((1, tb, q_a.shape[2]), lambda i, j: (i, j, 0)),
                pl.BlockSpec((1, t, LANES), lambda i, j: (i, 0, 0)),
                pl.BlockSpec((1, t, LANES), lambda i, j: (i, 0, 0)),
                pl.BlockSpec((1, tb, q_i.shape[2]), lambda i, j: (i, j, 0)),
                pl.BlockSpec((1, t, IDX_DIM), lambda i, j: (i, 0, 0)),
                pl.BlockSpec((1, tb, LANES), lambda i, j: (i, j, 0)),
                pl.BlockSpec(bias.shape, lambda i, j: (0, 0, 0, 0))]
    return pl.pallas_call(
        functools.partial(_dsa_p_kernel, n_sel=n_sel, nk=nq),
        grid=(b, nq), in_specs=in_specs,
        out_specs=pl.BlockSpec((1, tb, q_a.shape[2]), lambda i, j: (i, j, 0)),
        out_shape=jax.ShapeDtypeStruct(q_a.shape, F32),
        scratch_shapes=[pltpu.VMEM((t, LANES), BF16), pltpu.VMEM((nq, LANES, tb), BF16),
                        pltpu.VMEM((t, IDX_DIM), BF16),
                        pltpu.VMEM((nq, tb, tb), jnp.int32), pltpu.VMEM((nq, tb, tb), F32),
                        pltpu.VMEM((A_HEADS, tb, LANES), BF16),
                        pltpu.VMEM((A_HEADS, 1, tb), F32), pltpu.VMEM((A_HEADS, 1, tb), F32),
                        pltpu.VMEM((A_HEADS, LANES, tb), F32)],
        compiler_params=_cparams(("arbitrary", "arbitrary")), name="attn_dsa")(q_a, k_a, v_a, q_i, k_i, w_i, bias)


def _fox_s_kernel(pt_ref, q_ref, kn_ref, vn_ref, lfn_ref, *refs, ts, pp):
    kt_refs = refs[:pp]
    vt_refs = refs[pp:2 * pp]
    lft_refs = refs[2 * pp:3 * pp]
    o_ref = refs[3 * pp]
    qbd_ref, kpad_ref, vpad_ref, lfpad_ref, m_ref, l_ref, acc_ref, carry_ref, cq_ref = refs[3 * pp + 1:]
    b = pl.program_id(0)
    s = pl.program_id(1)
    nrow = ts * C_HEADS
    width = C_HEADS * HEAD_DIM
    row = lax.broadcasted_iota(jnp.int32, (PAGE, PAGE), 0)
    col = lax.broadcasted_iota(jnp.int32, (PAGE, PAGE), 1)

    @pl.when(jnp.logical_and(b == 0, s == 0))
    def _():
        kpad_ref[...] = jnp.zeros_like(kpad_ref)
        vpad_ref[...] = jnp.zeros_like(vpad_ref)

    def update(z, pv):
        m_new, l_new, acc_new = _softmax_rows(z, m_ref[...], l_ref[...], acc_ref[...], pv)
        m_ref[...] = m_new
        l_ref[...] = l_new
        acc_ref[...] = acc_new

    @pl.when(s == 0)
    def _():
        hmask = (lax.broadcasted_iota(jnp.int32, (C_HEADS, width), 1) // HEAD_DIM
                 == lax.broadcasted_iota(jnp.int32, (C_HEADS, width), 0))
        q = q_ref[0] * (HEAD_DIM ** -0.5)
        for i in range(ts):
            qbd_ref[i * C_HEADS:(i + 1) * C_HEADS, :] = jnp.where(
                hmask, jnp.broadcast_to(q[i:i + 1, :], (C_HEADS, width)), 0.0).astype(BF16)
        kpad_ref[0:ts, :] = kn_ref[0]
        vpad_ref[0:ts, :] = vn_ref[0]
        lfpad_ref[...] = jnp.zeros_like(lfpad_ref)
        lfpad_ref[0:ts, :] = lfn_ref[0]
        m_ref[...] = jnp.full(m_ref.shape, NEG, F32)
        l_ref[...] = jnp.zeros_like(l_ref)
        acc_ref[...] = jnp.zeros_like(acc_ref)
        carry_ref[...] = jnp.zeros_like(carry_ref)
        lft = lfpad_ref[...].T[0:C_HEADS, :]
        incl = jnp.where(row <= col, 1.0, 0.0).astype(F32)
        cnew = _dot_hi(lft, incl)
        for i in range(ts):
            cq_ref[i * C_HEADS:(i + 1) * C_HEADS, :] = jnp.broadcast_to(cnew[:, i:i + 1], (C_HEADS, PAGE))
        z = _dot_nt(qbd_ref[...], kpad_ref[...].astype(BF16))
        z = z + cq_ref[...] - jnp.concatenate([cnew] * ts, axis=0)
        rr = lax.broadcasted_iota(jnp.int32, (nrow, PAGE), 0)
        cc = lax.broadcasted_iota(jnp.int32, (nrow, PAGE), 1)
        z = jnp.where(cc * C_HEADS <= rr, z, NEG)
        vnew = vpad_ref[...].astype(BF16)
        update(z, lambda p: _dot(p, vnew))

    later = jnp.where(row > col, 1.0, 0.0).astype(F32)
    for k in range(pp):
        lft = lft_refs[k][...]
        suf = _dot_hi(lft, later) + carry_ref[...]
        carry_ref[...] += jnp.sum(lft, axis=1, keepdims=True)
        z = _dot(qbd_ref[...], kt_refs[k][...].astype(BF16))
        z = z + cq_ref[...] + jnp.concatenate([suf] * ts, axis=0)
        vt = vt_refs[k][...].astype(BF16)
        update(z, lambda p: _dot_nt(p, vt))

    @pl.when(s == pl.num_programs(1) - 1)
    def _():
        hmask = (lax.broadcasted_iota(jnp.int32, (C_HEADS, width), 1) // HEAD_DIM
                 == lax.broadcasted_iota(jnp.int32, (C_HEADS, width), 0))
        o = acc_ref[...] / l_ref[...]
        for i in range(ts):
            blk = jnp.where(hmask, o[i * C_HEADS:(i + 1) * C_HEADS, :], 0.0)
            o_ref[0, i:i + 1, :] = jnp.sum(blk, axis=0, keepdims=True)


def _fox_s(q, kn, vn, lfn, cache_kt, cache_vt, cache_lft, layer, page_table, pp):
    b, ts, width = q.shape
    npg = page_table.shape[1]
    nrow = ts * C_HEADS

    def page_idx(k):
        return lambda i, s, pt: (layer, pt[i, npg - 1 - (s * pp + k)], 0, 0)

    def seq_idx(i, s, pt):
        return (i, 0, 0)

    in_specs = [pl.BlockSpec((1, ts, width), seq_idx), pl.BlockSpec((1, ts, width), seq_idx),
                pl.BlockSpec((1, ts, width), seq_idx), pl.BlockSpec((1, ts, LANES), seq_idx)]
    in_specs += [pl.BlockSpec((None, None, width, PAGE), page_idx(k)) for k in range(pp)]
    in_specs += [pl.BlockSpec((None, None, width, PAGE), page_idx(k)) for k in range(pp)]
    in_specs += [pl.BlockSpec((None, None, C_HEADS, PAGE), page_idx(k)) for k in range(pp)]
    grid_spec = pltpu.PrefetchScalarGridSpec(
        num_scalar_prefetch=1, grid=(b, npg // pp), in_specs=in_specs,
        out_specs=pl.BlockSpec((1, ts, width), seq_idx),
        scratch_shapes=[pltpu.VMEM((nrow, width), BF16), pltpu.VMEM((PAGE, width), F32),
                        pltpu.VMEM((PAGE, width), F32), pltpu.VMEM((PAGE, LANES), F32),
                        pltpu.VMEM((nrow, 1), F32), pltpu.VMEM((nrow, 1), F32),
                        pltpu.VMEM((nrow, width), F32), pltpu.VMEM((C_HEADS, PAGE), F32),
                        pltpu.VMEM((nrow, PAGE), F32)])
    return pl.pallas_call(
        functools.partial(_fox_s_kernel, ts=ts, pp=pp), grid_spec=grid_spec,
        out_shape=jax.ShapeDtypeStruct((b, ts, width), F32),
        compiler_params=_cparams(("arbitrary", "arbitrary")), name="attn_fox_sample")(
            page_table, q, kn, vn, lfn, *([cache_kt] * pp), *([cache_vt] * pp), *([cache_lft] * pp))


def _diff_s_kernel(pt_ref, q_ref, kn_ref, vn_ref, *refs, ts, pp, npg, lam_init):
    k_refs = refs[:pp]
    v_refs = refs[pp:2 * pp]
    bias_ref, lamp_ref, subln_ref, o_ref, qx_ref, kpad_ref, vpad_ref, m_ref, l_ref, acc_ref = refs[2 * pp:]
    b = pl.program_id(0)
    s = pl.program_id(1)
    hrows = B_HEADS * ts
    lane = lax.broadcasted_iota(jnp.int32, (ts, LANES), 1)

    @pl.when(jnp.logical_and(b == 0, s == 0))
    def _():
        kpad_ref[...] = jnp.zeros_like(kpad_ref)
        vpad_ref[...] = jnp.zeros_like(vpad_ref)

    @pl.when(s == 0)
    def _():
        q = q_ref[0] * (HEAD_DIM ** -0.5)
        for n in range(B_HEADS):
            blk = q[:, n * LANES:(n + 1) * LANES]
            qx_ref[n * ts:(n + 1) * ts, :] = jnp.where(lane < HEAD_DIM, blk, 0.0)
            qx_ref[hrows + n * ts:hrows + (n + 1) * ts, :] = jnp.where(lane >= HEAD_DIM, blk, 0.0)
        m_ref[...] = jnp.full(m_ref.shape, NEG, F32)
        l_ref[...] = jnp.zeros_like(l_ref)
        acc_ref[...] = jnp.zeros_like(acc_ref)

    def step(kall, vall, chunk):
        z = _dot_nt(qx_ref[...].astype(BF16), kall.astype(BF16)) + bias_ref[chunk]
        vb = vall.astype(BF16)
        m_new, l_new, acc_new = _softmax_rows(z, m_ref[...], l_ref[...], acc_ref[...], lambda p: _dot(p, vb))
        m_ref[...] = m_new
        l_ref[...] = l_new
        acc_ref[...] = acc_new

    for k in range(pp):
        step(k_refs[k][...], v_refs[k][...], s * pp + k)

    @pl.when(s == pl.num_programs(1) - 1)
    def _():
        kpad_ref[0:ts * B_HEADS, :] = kn_ref[0]
        vpad_ref[0:ts * B_HEADS, :] = vn_ref[0]
        step(kpad_ref[...], vpad_ref[...], npg)
        lp = lamp_ref[...]
        lam = (jnp.exp(jnp.sum(lp[0:1] * lp[1:2], axis=1, keepdims=True))
               - jnp.exp(jnp.sum(lp[2:3] * lp[3:4], axis=1, keepdims=True)) + lam_init)
        o = acc_ref[...] / l_ref[...]
        od = o[0:hrows] - lam * o[hrows:2 * hrows]
        for n in range(B_HEADS):
            o_ref[0, :, n * B_VDIM:(n + 1) * B_VDIM] = (
                _rms(od[n * ts:(n + 1) * ts], subln_ref[...]) * (1.0 - lam_init))


def _diff_s(q, kn16, vn16, cache_k, cache_v, layer, page_table, bias, lamp, subln, lam_init, pp):
    b, ts, width = q.shape
    npg = page_table.shape[1]
    nrow = 2 * B_HEADS * ts
    krows = PAGE * B_HEADS

    def page_idx(k):
        return lambda i, s, pt: (layer, pt[i, s * pp + k], 0, 0)

    def seq_idx(i, s, pt):
        return (i, 0, 0)

    in_specs = [pl.BlockSpec((1, ts, width), seq_idx), pl.BlockSpec((1,) + kn16.shape[1:], seq_idx),
                pl.BlockSpec((1,) + vn16.shape[1:], seq_idx)]
    in_specs += [pl.BlockSpec((None, None, krows, LANES), page_idx(k)) for k in range(pp)] * 2
    in_specs += [pl.BlockSpec(bias.shape, lambda i, s, pt: (0, 0, 0)),
                 pl.BlockSpec(lamp.shape, lambda i, s, pt: (0, 0)),
                 pl.BlockSpec(subln.shape, lambda i, s, pt: (0, 0))]
    grid_spec = pltpu.PrefetchScalarGridSpec(
        num_scalar_prefetch=1, grid=(b, npg // pp), in_specs=in_specs,
        out_specs=pl.BlockSpec((1, ts, width), seq_idx),
        scratch_shapes=[pltpu.VMEM((nrow, LANES), F32), pltpu.VMEM((krows, LANES), F32),
                        pltpu.VMEM((krows, LANES), F32), pltpu.VMEM((nrow, 1), F32),
                        pltpu.VMEM((nrow, 1), F32), pltpu.VMEM((nrow, LANES), F32)])
    return pl.pallas_call(
        functools.partial(_diff_s_kernel, ts=ts, pp=pp, npg=npg, lam_init=lam_init), grid_spec=grid_spec,
        out_shape=jax.ShapeDtypeStruct((b, ts, width), F32),
        compiler_params=_cparams(("arbitrary", "arbitrary")), name="attn_diff_sample")(
            page_table, q, kn16, vn16, *([cache_k] * pp), *([cache_v] * pp), bias, lamp, subln)


def _dsa_s_kernel(pt_ref, qa_ref, qi_ref, wi_ref, kn_ref, vn_ref, kin_ref, *refs, ts, pp, n_sel, nchunk):
    kt_refs = refs[:pp]
    vt_refs = refs[pp:2 * pp]
    kit_refs = refs[2 * pp:3 * pp]
    bias_ref, o_ref, kst_ref, vst_ref, pad_ref, key_ref, am_ref = refs[3 * pp:]
    b = pl.program_id(0)
    s = pl.program_id(1)
    last = nchunk - 1
    rpad = SUBLANES

    @pl.when(jnp.logical_and(b == 0, s == 0))
    def _():
        pad_ref[...] = jnp.zeros_like(pad_ref)

    def index_scores(kit):
        s32 = jnp.maximum(_dot(qi_ref[0].astype(BF16), kit.astype(BF16)), 0.0) * wi_ref[0]
        sc = s32[0:rpad]
        for h in range(1, IDX_HEADS):
            sc = sc + s32[h * rpad:(h + 1) * rpad]
        return sc

    for k in range(pp):
        chunk = s * pp + k
        kst_ref[chunk] = kt_refs[k][...]
        vst_ref[chunk] = vt_refs[k][...]
        key_ref[chunk] = _sort_key(index_scores(kit_refs[k][...]))

    @pl.when(s == pl.num_programs(1) - 1)
    def _():
        pad_ref[0, 0:ts, :] = kn_ref[0]
        pad_ref[1, 0:ts, :] = vn_ref[0]
        pad_ref[2, 0:ts, 0:IDX_DIM] = kin_ref[0]
        kst_ref[last] = pad_ref[0].T
        vst_ref[last] = pad_ref[1].T
        rr = lax.broadcasted_iota(jnp.int32, (rpad, PAGE), 0)
        cc = lax.broadcasted_iota(jnp.int32, (rpad, PAGE), 1)
        causal = jnp.logical_and(cc <= rr, cc < ts)
        key_ref[last] = _sort_key(jnp.where(causal, index_scores(pad_ref[2].T[0:IDX_DIM, :]), -jnp.inf))

        def count(pred):
            acc = jnp.zeros((rpad, PAGE), F32)
            for c in range(nchunk):
                acc = acc + jnp.where(pred(key_ref[c]), 1.0, 0.0)
            return jnp.sum(acc, axis=1, keepdims=True)

        thr = _kth_largest(lambda trial: count(lambda kk: kk >= trial), n_sel, (rpad, 1))
        need = n_sel - count(lambda kk: kk > thr)
        r2 = lax.broadcasted_iota(jnp.int32, (PAGE, PAGE), 0)
        c2 = lax.broadcasted_iota(jnp.int32, (PAGE, PAGE), 1)
        ustrict = jnp.where(r2 < c2, 1.0, 0.0).astype(BF16)
        carry = jnp.zeros((rpad, 1), F32)
        for c in range(nchunk):
            kk = key_ref[c]
            eq = kk == thr
            eqf = jnp.where(eq, 1.0, 0.0)
            prefix = _dot(eqf.astype(BF16), ustrict) + carry
            sel = jnp.logical_or(kk > thr, jnp.logical_and(eq, prefix < need))
            if c == last:
                sel = jnp.logical_and(sel, causal)
            am_ref[c] = jnp.where(sel, 0.0, NEG)
            carry = carry + jnp.sum(eqf, axis=1, keepdims=True)

        nrow = A_HEADS * rpad
        qbd = qa_ref[0].astype(BF16)

        def attend(c, st):
            m_old, l_old, acc_old = st
            z = (_dot(qbd, kst_ref[c].astype(BF16)) + bias_ref[c]
                 + jnp.concatenate([am_ref[c]] * A_HEADS, axis=0))
            vt = vst_ref[c].astype(BF16)
            return _softmax_rows(z, m_old, l_old, acc_old, lambda p: _dot_nt(p, vt))

        st = (jnp.full((nrow, 1), NEG, F32), jnp.zeros((nrow, 1), F32), jnp.zeros((nrow, LANES), F32))
        _, l_fin, acc_fin = lax.fori_loop(0, nchunk, attend, st)
        o = acc_fin / l_fin
        lane = lax.broadcasted_iota(jnp.int32, (rpad, LANES), 1)
        for blk in range(A_HEADS // 2):
            grp = (2 * blk) // A_GROUP
            lo = o[(2 * blk) * rpad:(2 * blk + 1) * rpad]
            hi = o[(2 * blk + 1) * rpad:(2 * blk + 2) * rpad]
            if grp == 0:
                hi = pltpu.roll(hi, HEAD_DIM, axis=1)
            else:
                lo = pltpu.roll(lo, HEAD_DIM, axis=1)
            res = jnp.where(lane < HEAD_DIM, lo, hi)
            o_ref[0, :, blk * LANES:(blk + 1) * LANES] = res[0:ts]


def _dsa_s(qa_bd, qi32, w32, kn, vn, kin, cache_kt, cache_vt, cache_kit, layer, page_table, bias, n_sel, ts, pp):
    b = qa_bd.shape[0]
    npg = page_table.shape[1]
    nchunk = npg + 1
    width = A_HEADS * HEAD_DIM

    def page_idx(k):
        return lambda i, s, pt: (layer, pt[i, s * pp + k], 0, 0)

    def seq_idx(i, s, pt):
        return (i, 0, 0)

    in_specs = [pl.BlockSpec((1,) + qa_bd.shape[1:], seq_idx), pl.BlockSpec((1,) + qi32.shape[1:], seq_idx),
                pl.BlockSpec((1,) + w32.shape[1:], seq_idx),
                pl.BlockSpec((1, ts, LANES), seq_idx), pl.BlockSpec((1, ts, LANES), seq_idx),
                pl.BlockSpec((1, ts, IDX_DIM), seq_idx)]
    in_specs += [pl.BlockSpec((None, None, LANES, PAGE), page_idx(k)) for k in range(pp)] * 2
    in_specs += [pl.BlockSpec((None, None, IDX_DIM, PAGE), page_idx(k)) for k in range(pp)]
    in_specs += [pl.BlockSpec(bias.shape, lambda i, s, pt: (0, 0, 0))]
    grid_spec = pltpu.PrefetchScalarGridSpec(
        num_scalar_prefetch=1, grid=(b, npg // pp), in_specs=in_specs,
        out_specs=pl.BlockSpec((1, ts, width), seq_idx),
        scratch_shapes=[pltpu.VMEM((nchunk, LANES, PAGE), F32), pltpu.VMEM((nchunk, LANES, PAGE), F32),
                        pltpu.VMEM((3, PAGE, LANES), F32),
                        pltpu.VMEM((nchunk, SUBLANES, PAGE), jnp.int32), pltpu.VMEM((nchunk, SUBLANES, PAGE), F32)])
    return pl.pallas_call(
        functools.partial(_dsa_s_kernel, ts=ts, pp=pp, n_sel=n_sel, nchunk=nchunk), grid_spec=grid_spec,
        out_shape=jax.ShapeDtypeStruct((b, ts, width), F32),
        compiler_params=_cparams(("arbitrary", "arbitrary")), name="attn_dsa_sample")(
            page_table, qa_bd, qi32, w32, kn, vn, kin,
            *([cache_kt] * pp), *([cache_vt] * pp), *([cache_kit] * pp), bias)


def _t5_bucket(rel):
    n = jnp.maximum(rel, 0)
    exact = N_BUCKETS // 2
    nf = jnp.maximum(n, 1).astype(F32)
    log_b = exact + (jnp.log(nf / exact) / math.log(MAX_DISTANCE / exact) * (N_BUCKETS - exact)).astype(jnp.int32)
    return jnp.where(n < exact, n, jnp.minimum(log_b, N_BUCKETS - 1))


def _bias_of(tab, rel):
    bucket = _t5_bucket(rel)[None]
    tab = tab.astype(F32)
    out = jnp.broadcast_to(tab[0].reshape((-1,) + (1,) * rel.ndim), (tab.shape[1],) + rel.shape)
    for bkt in range(1, N_BUCKETS):
        out = jnp.where(bucket == bkt, tab[bkt].reshape((-1,) + (1,) * rel.ndim), out)
    return out


def _bias_tiles_t(tab, tb):
    j = jnp.arange(tb, dtype=jnp.int32)[:, None]
    i = jnp.arange(tb, dtype=jnp.int32)[None, :]
    return _bias_of(tab, jnp.stack([d * tb + i - j for d in range(3)]))


def _split(w, sizes):
    offs = np.cumsum((0,) + tuple(sizes))
    return [w[:, int(offs[i]):int(offs[i + 1])] for i in range(len(sizes))]


def _pad_cols(w, n):
    return jnp.pad(w, ((0, 0), (0, n - w.shape[1])))


def _row_tile(m, want):
    t = min(m, want)
    while m % t:
        t //= 2
    return t


def _pages_per_step(npg, want):
    pp = min(npg, want)
    while npg % pp:
        pp -= 1
    return pp


def kernel(x_prompt, x_sample, cache_a_k, cache_a_v, cache_a_kidx, cache_b_k, cache_b_v, cache_c_k, cache_c_v,
           cache_c_logf, state_conv, page_table, rel_bias_table, w_in_even, w_out_even, lambda_q1, lambda_k1,
           lambda_q2, lambda_k2, diff_subln, w_in_odd, b_forget, w_out_odd, norm_mix_pre, norm_mix_post,
           norm_ffn_pre, norm_ffn_post, w_gate_up, w_conv, b_conv, w_down):
    bp, tp, d = x_prompt.shape
    bs, ts, _ = x_sample.shape
    depth = w_gate_up.shape[0]
    n_pool = cache_a_k.shape[1]
    npg = page_table.shape[1]
    past = npg * PAGE
    mp, ms = bp * tp, bs * ts
    tb_dsa = LANES
    tb_attn = min(256, tp)
    nsel_p = min(TOPK_MAX, tp // 4)
    nsel_s = min(TOPK_MAX, (past + ts) // 4)
    tm_p = _row_tile(mp, 512)
    tm_s = _row_tile(ms, 512)
    tf = D_FF // 2
    dff = D_FF
    pp = _pages_per_step(npg, 4)

    xp = x_prompt.reshape(mp, d)
    xs = x_sample.reshape(ms, d)
    tab_a = rel_bias_table[:, :A_HEADS]
    tab_b = rel_bias_table[:, A_HEADS:]
    bias_a_p = _bias_tiles_t(tab_a, tb_dsa)
    bias_b_p = _bias_tiles_t(tab_b, tb_attn)
    rel_s = (past + jnp.arange(ts, dtype=jnp.int32))[:, None] - jnp.arange(past + PAGE, dtype=jnp.int32)[None, :]
    rows_a = _bias_of(tab_a, rel_s).reshape(A_HEADS, ts, npg + 1, PAGE)
    rows_b = _bias_of(tab_b, rel_s).reshape(B_HEADS, ts, npg + 1, PAGE)
    bias_a_s = jnp.pad(jnp.moveaxis(rows_a, 2, 0), ((0, 0), (0, 0), (0, SUBLANES - ts), (0, 0)))
    bias_a_s = bias_a_s.reshape(npg + 1, A_HEADS * SUBLANES, PAGE)
    kq_ok = rel_s.reshape(ts, npg + 1, PAGE) >= 0
    same = jnp.eye(B_HEADS, dtype=bool)
    bias_b_s = jnp.where(same[:, None, None, None, :] & kq_ok[None, :, :, :, None],
                         rows_b[..., None], NEG)
    bias_b_s = jnp.moveaxis(bias_b_s, 2, 0).reshape(npg + 1, B_HEADS * ts, PAGE * B_HEADS)
    bias_b_s = jnp.concatenate([bias_b_s, bias_b_s], axis=1)

    ca_kt = jnp.transpose(cache_a_k, (0, 1, 3, 4, 2)).reshape(-1, n_pool, LANES, PAGE)
    ca_vt = jnp.transpose(cache_a_v, (0, 1, 3, 4, 2)).reshape(-1, n_pool, LANES, PAGE)
    ca_kit = jnp.transpose(cache_a_kidx, (0, 1, 3, 2))
    cb_k = cache_b_k.reshape(-1, n_pool, PAGE * B_HEADS, 2 * HEAD_DIM)
    cb_v = cache_b_v.reshape(-1, n_pool, PAGE * B_HEADS, B_VDIM)
    cc_kt = jnp.transpose(cache_c_k, (0, 1, 3, 4, 2)).reshape(-1, n_pool, C_HEADS * HEAD_DIM, PAGE)
    cc_vt = jnp.transpose(cache_c_v, (0, 1, 3, 4, 2)).reshape(-1, n_pool, C_HEADS * HEAD_DIM, PAGE)
    cc_lft = jnp.transpose(cache_c_logf, (0, 1, 3, 2))

    even_rows_p, even_rows_s, odd_rows_p, odd_rows_s, conv_p, conv_s = [], [], [], [], [], []
    for l in range(depth):
        if l % 2 == 0:
            e = l // 2
            lam_init = 0.8 - 0.6 * math.exp(-0.3 * l)
            ws = _split(w_in_even[e], EVEN_SPLIT)
            ws[5] = _pad_cols(ws[5], LANES)
            ws = [w.astype(BF16) for w in ws]
            wo = w_out_even[e].astype(BF16)
            wo_parts = [wo[:A_HEADS * HEAD_DIM], wo[A_HEADS * HEAD_DIM:]]
            lamp = jnp.stack([lambda_q1[e], lambda_k1[e], lambda_q2[e], lambda_k2[e]]).astype(F32)
            subln = diff_subln[e].reshape(1, B_VDIM)

            q_a, k_a, v_a, q_i, k_i, w_i, q_b, k_b, v_b = _rms_proj(xp, norm_mix_pre[l], ws, tm_p)
            r3 = lambda a: a.reshape(bp, tp, a.shape[1])
            o_a = _dsa_p(r3(q_a), r3(k_a), r3(v_a), r3(q_i), r3(k_i), r3(w_i), bias_a_p, nsel_p)
            o_b = _flash2("diff", r3(q_b), r3(k_b), r3(v_b), (bias_b_p, lamp, subln), tb_attn, lam_init)
            xp = _proj_post(xp, norm_mix_post[l], [o_a.reshape(mp, -1), o_b.reshape(mp, -1)], wo_parts, tm_p)
            even_rows_p.append((k_a.reshape(bp, tp, A_KV_HEADS, HEAD_DIM), v_a.reshape(bp, tp, A_KV_HEADS, HEAD_DIM),
                                k_i.reshape(bp, tp, IDX_DIM), k_b.reshape(bp, tp, B_HEADS, 2 * HEAD_DIM),
                                v_b.reshape(bp, tp, B_HEADS, B_VDIM)))

            q_a, k_a, v_a, q_i, k_i, w_i, q_b, k_b, v_b = _rms_proj(xs, norm_mix_pre[l], ws, tm_s)
            s3 = lambda a: a.reshape(bs, ts, a.shape[1])
            qa4 = (q_a * (HEAD_DIM ** -0.5)).reshape(bs, ts, A_HEADS, HEAD_DIM)
            qa4 = jnp.pad(jnp.moveaxis(qa4, 1, 2), ((0, 0), (0, 0), (0, SUBLANES - ts), (0, 0)))
            zeros = jnp.zeros_like(qa4)
            qa_bd = jnp.concatenate([jnp.concatenate([qa4[:, :A_GROUP], zeros[:, :A_GROUP]], axis=-1),
                                     jnp.concatenate([zeros[:, A_GROUP:], qa4[:, A_GROUP:]], axis=-1)], axis=1)
            qa_bd = qa_bd.reshape(bs, A_HEADS * SUBLANES, LANES)
            qi4 = jnp.moveaxis(q_i.reshape(bs, ts, IDX_HEADS, IDX_DIM), 1, 2)
            qi32 = jnp.pad(qi4, ((0, 0), (0, 0), (0, SUBLANES - ts), (0, 0))).reshape(bs, IDX_HEADS * SUBLANES, IDX_DIM)
            w4 = jnp.moveaxis(w_i[:, :IDX_HEADS].reshape(bs, ts, IDX_HEADS), 1, 2) * (IDX_DIM ** -0.5 * IDX_HEADS ** -0.5)
            w32 = jnp.pad(w4, ((0, 0), (0, 0), (0, SUBLANES - ts))).reshape(bs, IDX_HEADS * SUBLANES, 1)
            w32 = jnp.broadcast_to(w32, (bs, IDX_HEADS * SUBLANES, PAGE))
            o_a = _dsa_s(qa_bd, qi32, w32, s3(k_a), s3(v_a), s3(k_i), ca_kt, ca_vt, ca_kit,
                         e, page_table, bias_a_s, nsel_s, ts, pp)
            o_b = _diff_s(s3(q_b), k_b.reshape(bs, ts * B_HEADS, 2 * HEAD_DIM), v_b.reshape(bs, ts * B_HEADS, B_VDIM),
                          cb_k, cb_v, e, page_table, bias_b_s, lamp, subln, lam_init, pp)
            xs = _proj_post(xs, norm_mix_post[l], [o_a.reshape(ms, -1), o_b.reshape(ms, -1)], wo_parts, tm_s)
            even_rows_s.append((k_a.reshape(bs, ts, A_KV_HEADS, HEAD_DIM), v_a.reshape(bs, ts, A_KV_HEADS, HEAD_DIM),
                                k_i.reshape(bs, ts, IDX_DIM), k_b.reshape(bs, ts, B_HEADS, 2 * HEAD_DIM),
                                v_b.reshape(bs, ts, B_HEADS, B_VDIM)))
        else:
            o = l // 2
            ws = _split(w_in_odd[o], ODD_SPLIT)
            ws[3] = _pad_cols(ws[3], LANES)
            ws = [w.astype(BF16) for w in ws]
            bfp = _pad_cols(b_forget[o].reshape(1, C_HEADS), LANES)
            wo = [w_out_odd[o].astype(BF16)]

            q, k, v, lf = _rms_proj(xp, norm_mix_pre[l], ws, tm_p, logsig_bias=bfp)
            r3 = lambda a: a.reshape(bp, tp, a.shape[1])
            cum, cumt = _cumsum(r3(lf), tb_attn)
            o_c = _flash2("fox", r3(q), r3(k), r3(v), (cum, cumt), tb_attn)
            xp = _proj_post(xp, norm_mix_post[l], [o_c.reshape(mp, -1)], wo, tm_p)
            odd_rows_p.append((k.reshape(bp, tp, C_HEADS, HEAD_DIM), v.reshape(bp, tp, C_HEADS, HEAD_DIM),
                               lf[:, :C_HEADS].reshape(bp, tp, C_HEADS)))

            q, k, v, lf = _rms_proj(xs, norm_mix_pre[l], ws, tm_s, logsig_bias=bfp)
            s3 = lambda a: a.reshape(bs, ts, a.shape[1])
            o_c = _fox_s(s3(q), s3(k), s3(v), s3(lf), cc_kt, cc_vt, cc_lft, o, page_table, pp)
            xs = _proj_post(xs, norm_mix_post[l], [o_c.reshape(ms, -1)], wo, tm_s)
            odd_rows_s.append((k.reshape(bs, ts, C_HEADS, HEAD_DIM), v.reshape(bs, ts, C_HEADS, HEAD_DIM),
                               lf[:, :C_HEADS].reshape(bs, ts, C_HEADS)))

        wgu = w_gate_up[l].astype(BF16)
        wg, wu = wgu[:, :dff], wgu[:, dff:]
        wd = w_down[l].astype(BF16)
        tiles = tp // tm_p
        xp, tail = _ffn(xp, norm_ffn_pre[l], norm_ffn_post[l], wg, wu, w_conv[l], b_conv[l], wd,
                        tm_p, tf, tiles)
        conv_p.append(tail.reshape(bp, tiles, 8, dff)[:, tiles - 1, 8 - (CONV_W - 1):, :])
        st = state_conv[l]
        prev1 = jnp.concatenate([st[:, 1:2], jnp.zeros((bs, ts - 1, dff), F32)], axis=1).reshape(ms, dff)
        prev2 = jnp.concatenate([st, jnp.zeros((bs, ts - 2, dff), F32)], axis=1).reshape(ms, dff)
        xs, gfull = _ffn(xs, norm_ffn_pre[l], norm_ffn_post[l], wg, wu, w_conv[l], b_conv[l], wd,
                         tm_s, tf, ts, prev=(prev1, prev2))
        conv_s.append(gfull.reshape(bs, ts, dff)[:, ts - (CONV_W - 1):, :])

    def stack(rows, i):
        return jnp.stack([r[i] for r in rows])

    ev_p = [stack(even_rows_p, i) for i in range(5)]
    ev_s = [stack(even_rows_s, i) for i in range(5)]
    od_p = [stack(odd_rows_p, i) for i in range(3)]
    od_s = [stack(odd_rows_s, i) for i in range(3)]
    return (xp.reshape(bp, tp, d), xs.reshape(bs, ts, d),
            ev_p[0], ev_s[0], ev_p[1], ev_s[1], ev_p[2], ev_s[2], ev_p[3], ev_s[3], ev_p[4], ev_s[4],
            od_p[0], od_s[0], od_p[1], od_s[1], od_p[2], od_s[2],
            jnp.stack(conv_p), jnp.stack(conv_s))
```

```python
import functools
import math

import jax
import jax.numpy as jnp
import numpy as np
from jax import lax
from jax.experimental import pallas as pl
from jax.experimental.pallas import tpu as pltpu

F32 = jnp.float32
BF16 = jnp.bfloat16
HIGHEST = lax.Precision.HIGHEST

D_MODEL = 1024
HEAD_DIM = 64
A_HEADS = 8
A_KV_HEADS = 2
A_GROUP = A_HEADS // A_KV_HEADS
IDX_HEADS = 4
IDX_DIM = 64
TOPK_MAX = 256
B_HEADS = 4
B_VDIM = 128
C_HEADS = 16
N_BUCKETS = 32
MAX_DISTANCE = 128
D_FF = 2816
CONV_W = 3
EPS = 1e-6
PAGE = 128

LANES = 128
SUBLANES = 8
NEG = -1e30
INT_MIN = -2147483648
VMEM_LIMIT = 48 * 1024 * 1024

EVEN_SPLIT = (512, 128, 128, 256, 64, 4, 512, 512, 512)
ODD_SPLIT = (1024, 1024, 1024, 16)

NT_DIMS = (((1,), (1,)), ((), ()))


def _cparams(sem):
    return pltpu.CompilerParams(dimension_semantics=sem, vmem_limit_bytes=VMEM_LIMIT)


def _rms(x, g):
    return x * lax.rsqrt(jnp.mean(x * x, axis=-1, keepdims=True) + EPS) * g


def _log_sigmoid(x):
    return -(jnp.maximum(-x, 0.0) + jnp.log1p(jnp.exp(-jnp.abs(x))))


def _gelu_tanh(x):
    c = math.sqrt(2.0 / math.pi)
    return x * (0.5 * (1.0 + jnp.tanh(c * (x + 0.044715 * (x * x * x)))))


def _dot(a, b):
    return jnp.dot(a, b, preferred_element_type=F32)


def _dot_nt(a, b):
    return lax.dot_general(a, b, NT_DIMS, preferred_element_type=F32)


def _dot_hi(a, b):
    return jnp.dot(a, b, preferred_element_type=F32, precision=HIGHEST)


def _softmax_rows(z, m_old, l_old, acc_old, pv):
    m_new = jnp.maximum(m_old, jnp.max(z, axis=1, keepdims=True))
    alpha = jnp.exp(m_old - m_new)
    p = jnp.exp(z - m_new)
    l_new = alpha * l_old + jnp.sum(p, axis=1, keepdims=True)
    acc_new = alpha * acc_old + pv(p.astype(BF16))
    return m_new, l_new, acc_new


def _softmax_cols(zt, m_old, l_old, acc_old, vt):
    m_new = jnp.maximum(m_old, jnp.max(zt, axis=0, keepdims=True))
    alpha = jnp.exp(m_old - m_new)
    p = jnp.exp(zt - m_new)
    l_new = alpha * l_old + jnp.sum(p, axis=0, keepdims=True)
    acc_new = alpha * acc_old + _dot(vt, p.astype(BF16))
    return m_new, l_new, acc_new


def _sort_key(s):
    s = jnp.where(s == 0.0, 0.0, s)
    bits = pltpu.bitcast(s, jnp.int32)
    return jnp.where(bits < 0, bits ^ jnp.int32(0x7FFFFFFF), bits)


def _kth_largest(count_ge, n_sel, shape):
    def step(it, ans):
        bit = lax.shift_left(jnp.int32(1), jnp.int32(31) - it)
        trial = ans | bit
        cnt = count_ge(trial ^ jnp.int32(INT_MIN))
        return jnp.where(cnt >= n_sel, trial, ans)

    ans = lax.fori_loop(0, 32, step, jnp.zeros(shape, jnp.int32))
    return ans ^ jnp.int32(INT_MIN)


def _rms_proj_kernel(x_ref, g_ref, *refs, n_w, logsig_last):
    w_refs = refs[:n_w]
    if logsig_last:
        bias_ref = refs[n_w]
        o_refs = refs[n_w + 1:]
    else:
        o_refs = refs[n_w:]
    h = _rms(x_ref[...], g_ref[...]).astype(BF16)
    for idx in range(n_w):
        y = _dot(h, w_refs[idx][...])
        if logsig_last and idx == n_w - 1:
            y = _log_sigmoid(y + bias_ref[...])
        o_refs[idx][...] = y


def _rms_proj(x2d, g, w_list, tm, logsig_bias=None):
    m, d = x2d.shape
    n_w = len(w_list)
    in_specs = [pl.BlockSpec((tm, d), lambda i: (i, 0)), pl.BlockSpec((1, d), lambda i: (0, 0))]
    in_specs += [pl.BlockSpec(w.shape, lambda i: (0, 0)) for w in w_list]
    args = [x2d, g.reshape(1, d)] + list(w_list)
    if logsig_bias is not None:
        in_specs.append(pl.BlockSpec(logsig_bias.shape, lambda i: (0, 0)))
        args.append(logsig_bias)
    out_shape = [jax.ShapeDtypeStruct((m, w.shape[1]), F32) for w in w_list]
    out_specs = [pl.BlockSpec((tm, w.shape[1]), lambda i: (i, 0)) for w in w_list]
    return pl.pallas_call(
        functools.partial(_rms_proj_kernel, n_w=n_w, logsig_last=logsig_bias is not None),
        grid=(m // tm,), in_specs=in_specs, out_specs=out_specs, out_shape=out_shape,
        compiler_params=_cparams(("parallel",)), name="rms_proj")(*args)


def _proj_post_kernel(x_ref, g_ref, *refs, n_in):
    o_refs = refs[:n_in]
    w_refs = refs[n_in:2 * n_in]
    out_ref = refs[2 * n_in]
    acc = None
    for o, w in zip(o_refs, w_refs):
        t = _dot(o[...].astype(BF16), w[...])
        acc = t if acc is None else acc + t
    out_ref[...] = x_ref[...] + _rms(acc, g_ref[...])


def _proj_post(x2d, g, o_list, w_list, tm):
    m, d = x2d.shape
    n_in = len(o_list)
    in_specs = [pl.BlockSpec((tm, d), lambda i: (i, 0)), pl.BlockSpec((1, d), lambda i: (0, 0))]
    in_specs += [pl.BlockSpec((tm, o.shape[1]), lambda i: (i, 0)) for o in o_list]
    in_specs += [pl.BlockSpec(w.shape, lambda i: (0, 0)) for w in w_list]
    return pl.pallas_call(
        functools.partial(_proj_post_kernel, n_in=n_in),
        grid=(m // tm,), in_specs=in_specs, out_specs=pl.BlockSpec((tm, d), lambda i: (i, 0)),
        out_shape=jax.ShapeDtypeStruct((m, d), F32),
        compiler_params=_cparams(("parallel",)), name="proj_post")(x2d, g.reshape(1, d), *o_list, *w_list)


def _ffn_kernel(*refs, tm, tiles_per_seq, sample):
    if sample:
        (x_ref, gpre_ref, gpost_ref, wg_ref, wu_ref, wc_ref, bc_ref, wd_ref, p1_ref, p2_ref,
         y_ref, gout_ref, h_ref, acc_ref, gs_ref) = refs
    else:
        (x_ref, gpre_ref, gpost_ref, wg_ref, wu_ref, wc_ref, bc_ref, wd_ref,
         y_ref, gout_ref, h_ref, acc_ref, gs_ref, halo_ref) = refs
    i = pl.program_id(0)
    c = pl.program_id(1)

    @pl.when(c == 0)
    def _():
        h_ref[...] = _rms(x_ref[...], gpre_ref[...]).astype(BF16)
        acc_ref[...] = jnp.zeros_like(acc_ref)

    h = h_ref[...]
    g = _dot(h, wg_ref[...])
    u = _dot(h, wu_ref[...])
    if sample:
        gs_ref[0:8, :] = jnp.zeros((8, g.shape[1]), F32)
    else:
        first = (i % tiles_per_seq) == 0

        @pl.when(first)
        def _():
            gs_ref[0:8, :] = jnp.zeros((8, g.shape[1]), F32)

        @pl.when(jnp.logical_not(first))
        def _():
            gs_ref[0:8, :] = halo_ref[c]

    gs_ref[8:tm + 8, :] = g
    g1 = gs_ref[7:tm + 7, :]
    g2 = gs_ref[6:tm + 6, :]
    if sample:
        t = lax.broadcasted_iota(jnp.int32, (tm, 1), 0) % tiles_per_seq
        g1 = jnp.where(t >= 1, g1, 0.0) + p1_ref[...]
        g2 = jnp.where(t >= 2, g2, 0.0) + p2_ref[...]
        gout_ref[...] = g
    else:
        tail = gs_ref[tm:tm + 8, :]
        halo_ref[c] = tail
        gout_ref[0] = tail
    w = wc_ref[...]
    gc = bc_ref[...] + w[0:1, :] * g2
    gc = gc + w[1:2, :] * g1
    gc = gc + w[2:3, :] * g
    a = _gelu_tanh(gc) * u
    acc_ref[...] += _dot(a.astype(BF16), wd_ref[...])

    @pl.when(c == pl.num_programs(1) - 1)
    def _():
        y_ref[...] = x_ref[...] + _rms(acc_ref[...], gpost_ref[...])


def _ffn(x2d, gpre, gpost, wg, wu, wc, bc, wd, tm, tf, tiles_per_seq, prev=None):
    m, d = x2d.shape
    dff = wg.shape[1]
    nc = dff // tf
    sample = prev is not None
    in_specs = [
        pl.BlockSpec((tm, d), lambda i, c: (i, 0)),
        pl.BlockSpec((1, d), lambda i, c: (0, 0)),
        pl.BlockSpec((1, d), lambda i, c: (0, 0)),
        pl.BlockSpec((d, tf), lambda i, c: (0, c)),
        pl.BlockSpec((d, tf), lambda i, c: (0, c)),
        pl.BlockSpec((CONV_W, tf), lambda i, c: (0, c)),
        pl.BlockSpec((1, tf), lambda i, c: (0, c)),
        pl.BlockSpec((tf, d), lambda i, c: (c, 0)),
    ]
    args = [x2d, gpre.reshape(1, d), gpost.reshape(1, d), wg, wu, wc, bc.reshape(1, dff), wd]
    scratch = [pltpu.VMEM((tm, d), BF16), pltpu.VMEM((tm, d), F32), pltpu.VMEM((tm + 8, tf), F32)]
    if sample:
        in_specs += [pl.BlockSpec((tm, tf), lambda i, c: (i, c))] * 2
        args += list(prev)
        gout_shape = jax.ShapeDtypeStruct((m, dff), F32)
        gout_spec = pl.BlockSpec((tm, tf), lambda i, c: (i, c))
    else:
        gout_shape = jax.ShapeDtypeStruct((m // tm, 8, dff), F32)
        gout_spec = pl.BlockSpec((1, 8, tf), lambda i, c: (i, 0, c))
        scratch.append(pltpu.VMEM((nc, 8, tf), F32))
    return pl.pallas_call(
        functools.partial(_ffn_kernel, tm=tm, tiles_per_seq=tiles_per_seq, sample=sample),
        grid=(m // tm, nc), in_specs=in_specs,
        out_specs=[pl.BlockSpec((tm, d), lambda i, c: (i, 0)), gout_spec],
        out_shape=[jax.ShapeDtypeStruct((m, d), F32), gout_shape],
        scratch_shapes=scratch,
        compiler_params=_cparams(("arbitrary", "arbitrary")), name="conv_ffn")(*args)


def _cumsum_kernel(lf_ref, cum_ref, cumt_ref, *, tb, nchunk):
    row = lax.broadcasted_iota(jnp.int32, (tb, tb), 0)
    col = lax.broadcasted_iota(jnp.int32, (tb, tb), 1)
    lower = jnp.where(col <= row, 1.0, 0.0).astype(F32)
    carry = jnp.zeros((1, LANES), F32)
    for c in range(nchunk):
        x = lf_ref[0, c * tb:(c + 1) * tb, :]
        ct = _dot_hi(lower, x) + carry
        cumt_ref[0, c * tb:(c + 1) * tb, :] = ct
        carry = ct[tb - 1:tb, :]
        cum_ref[0, c] = ct.T[0:C_HEADS, :]


def _cumsum(lf3d, tb):
    b, t, _ = lf3d.shape
    nchunk = t // tb
    return pl.pallas_call(
        functools.partial(_cumsum_kernel, tb=tb, nchunk=nchunk),
        grid=(b,), in_specs=[pl.BlockSpec((1, t, LANES), lambda i: (i, 0, 0))],
        out_specs=[pl.BlockSpec((1, nchunk, C_HEADS, tb), lambda i: (i, 0, 0, 0)),
                   pl.BlockSpec((1, t, LANES), lambda i: (i, 0, 0))],
        out_shape=[jax.ShapeDtypeStruct((b, nchunk, C_HEADS, tb), F32),
                   jax.ShapeDtypeStruct((b, t, LANES), F32)],
        compiler_params=_cparams(("parallel",)), name="logf_cumsum")(lf3d)


def _flash2_kernel(*refs, mode, tb, nk, lam_init):
    if mode == "fox":
        (q_ref, k_ref, v_ref, cum_ref, cumt_ref, o_ref,
         kb_ref, vt_ref, m_ref, l_ref, acc_ref, ck_ref) = refs
    else:
        (q_ref, k_ref, v_ref, bias_ref, lamp_ref, subln_ref, o_ref,
         kb_ref, vt_ref, m_ref, l_ref, acc_ref) = refs
    g = pl.program_id(1)
    qi = pl.program_id(2)
    lane = lax.broadcasted_iota(jnp.int32, (tb, LANES), 1)
    krow = lax.broadcasted_iota(jnp.int32, (tb, tb), 0)
    qcol = lax.broadcasted_iota(jnp.int32, (tb, tb), 1)
    tri = krow <= qcol

    @pl.when(qi == 0)
    def _():
        kb_ref[...] = k_ref[0].astype(BF16)
        for c in range(nk):
            vt_ref[c] = v_ref[0, c * tb:(c + 1) * tb, :].T.astype(BF16)
        if mode == "fox":
            hrow = lax.broadcasted_iota(jnp.int32, (LANES, LANES), 0)
            for a in range(2):
                onehot = jnp.where(hrow == 2 * g + a, 1.0, 0.0).astype(F32)
                ck_ref[a] = _dot_hi(cumt_ref[0], onehot)

    q = q_ref[0] * (HEAD_DIM ** -0.5)
    qall = jnp.concatenate([jnp.where(lane < HEAD_DIM, q, 0.0), jnp.where(lane >= HEAD_DIM, q, 0.0)],
                           axis=0).astype(BF16)
    if mode == "fox":
        cq = jnp.concatenate([cum_ref[0, qi, pl.ds(2 * g + a, 1), :] for a in range(2)], axis=1)
    m_ref[...] = jnp.full(m_ref.shape, NEG, F32)
    l_ref[...] = jnp.zeros_like(l_ref)
    acc_ref[...] = jnp.zeros_like(acc_ref)
    rep = tb // LANES

    def chunk(c, diagonal):
        off = pl.multiple_of(c * tb, tb)
        zt = _dot_nt(kb_ref[pl.ds(off, tb), :], qall)
        if mode == "fox":
            ck0 = ck_ref[0, pl.ds(off, tb), :]
            ck1 = ck_ref[1, pl.ds(off, tb), :]
            zt = zt + cq - jnp.concatenate([ck0] * rep + [ck1] * rep, axis=1)
        else:
            bt = bias_ref[0, jnp.minimum(qi - c, 2)]
            zt = zt + jnp.concatenate([bt, bt], axis=1)
        if diagonal:
            zt = jnp.where(jnp.concatenate([tri, tri], axis=1), zt, NEG)
        m_new, l_new, acc_new = _softmax_cols(zt, m_ref[...], l_ref[...], acc_ref[...], vt_ref[c])
        m_ref[...] = m_new
        l_ref[...] = l_new
        acc_ref[...] = acc_new

    def body(c, carry):
        chunk(c, False)
        return carry

    lax.fori_loop(0, qi, body, 0)
    chunk(qi, True)
    o = acc_ref[...] / l_ref[...]
    o0 = o[:, 0:tb].T
    o1 = o[:, tb:2 * tb].T
    if mode == "fox":
        o_ref[0] = jnp.where(lane < HEAD_DIM, o0, o1)
    else:
        lp = lamp_ref[...]
        lam = (jnp.exp(jnp.sum(lp[0:1] * lp[1:2], axis=1, keepdims=True))
               - jnp.exp(jnp.sum(lp[2:3] * lp[3:4], axis=1, keepdims=True)) + lam_init)
        o_ref[0] = _rms(o0 - lam * o1, subln_ref[...]) * (1.0 - lam_init)


def _flash2(mode, q, k, v, extra, tb, lam_init=0.0):
    b, t, width = q.shape
    ng = width // LANES
    nq = t // tb
    in_specs = [pl.BlockSpec((1, tb, LANES), lambda i, g, j: (i, j, g)),
                pl.BlockSpec((1, t, LANES), lambda i, g, j: (i, 0, g)),
                pl.BlockSpec((1, t, LANES), lambda i, g, j: (i, 0, g))]
    scratch = [pltpu.VMEM((t, LANES), BF16), pltpu.VMEM((nq, LANES, tb), BF16),
               pltpu.VMEM((1, 2 * tb), F32), pltpu.VMEM((1, 2 * tb), F32), pltpu.VMEM((LANES, 2 * tb), F32)]
    if mode == "fox":
        cum, cumt = extra
        in_specs += [pl.BlockSpec((1, nq, C_HEADS, tb), lambda i, g, j: (i, 0, 0, 0)),
                     pl.BlockSpec((1, t, LANES), lambda i, g, j: (i, 0, 0))]
        scratch.append(pltpu.VMEM((2, t, LANES), F32))
    else:
        bias, lamp, subln = extra
        in_specs += [pl.BlockSpec((1, 3, tb, tb), lambda i, g, j: (g, 0, 0, 0)),
                     pl.BlockSpec(lamp.shape, lambda i, g, j: (0, 0)),
                     pl.BlockSpec(subln.shape, lambda i, g, j: (0, 0))]
    return pl.pallas_call(
        functools.partial(_flash2_kernel, mode=mode, tb=tb, nk=nq, lam_init=lam_init),
        grid=(b, ng, nq), in_specs=in_specs,
        out_specs=pl.BlockSpec((1, tb, LANES), lambda i, g, j: (i, j, g)),
        out_shape=jax.ShapeDtypeStruct((b, t, width), F32),
        scratch_shapes=scratch,
        compiler_params=_cparams(("arbitrary", "arbitrary", "arbitrary")),
        name="attn_" + mode)(q, k, v, *extra)


def _dsa_p_kernel(qa_ref, ka_ref, va_ref, qi_ref, ki_ref, wi_ref, bias_ref, o_ref,
                  kab_ref, vat_ref, kib_ref, key_ref, am_ref, qm_ref, m_ref, l_ref, acc_ref, *, n_sel, nk):
    tb = LANES
    qi = pl.program_id(1)
    krow = lax.broadcasted_iota(jnp.int32, (tb, tb), 0)
    qcol = lax.broadcasted_iota(jnp.int32, (tb, tb), 1)
    tri = krow <= qcol
    nchunks = qi + 1

    @pl.when(qi == 0)
    def _():
        kab_ref[...] = ka_ref[0].astype(BF16)
        kib_ref[...] = ki_ref[0].astype(BF16)
        for c in range(nk):
            vat_ref[c] = va_ref[0, c * tb:(c + 1) * tb, :].T.astype(BF16)

    qidx = qi_ref[0]
    qh = jnp.concatenate([qidx[:, h * IDX_DIM:(h + 1) * IDX_DIM] for h in range(IDX_HEADS)],
                         axis=0).astype(BF16)
    wt = (wi_ref[0] * (IDX_DIM ** -0.5 * IDX_HEADS ** -0.5)).T
    wall = jnp.concatenate([wt[h:h + 1, :] for h in range(IDX_HEADS)], axis=1)

    def p1(c, carry):
        off = pl.multiple_of(c * tb, tb)
        sh = jnp.maximum(_dot_nt(kib_ref[pl.ds(off, tb), :], qh), 0.0) * wall
        s = sh[:, 0:tb]
        for h in range(1, IDX_HEADS):
            s = s + sh[:, h * tb:(h + 1) * tb]
        s = jnp.where(jnp.logical_or(c < qi, tri), s, -jnp.inf)
        key_ref[c] = _sort_key(s)
        return carry

    lax.fori_loop(0, nchunks, p1, 0)

    def count(pred):
        def cb(c, acc):
            return acc + jnp.where(pred(key_ref[c]), 1.0, 0.0)
        acc = lax.fori_loop(0, nchunks, cb, jnp.zeros((tb, tb), F32))
        return jnp.sum(acc, axis=0, keepdims=True)

    thr = _kth_largest(lambda trial: count(lambda kk: kk >= trial), n_sel, (1, tb))
    need = n_sel - count(lambda kk: kk > thr)
    lstrict = jnp.where(qcol < krow, 1.0, 0.0).astype(BF16)

    def p3(c, carry):
        kk = key_ref[c]
        eq = kk == thr
        eqf = jnp.where(eq, 1.0, 0.0)
        prefix = _dot(lstrict, eqf.astype(BF16)) + carry
        sel = jnp.logical_or(kk > thr, jnp.logical_and(eq, prefix < need))
        sel = jnp.logical_and(sel, jnp.logical_or(c < qi, tri))
        am_ref[c] = jnp.where(sel, 0.0, NEG)
        return carry + jnp.sum(eqf, axis=0, keepdims=True)

    lax.fori_loop(0, nchunks, p3, jnp.zeros((1, tb), F32))

    lane = qcol
    qa = qa_ref[0] * (HEAD_DIM ** -0.5)
    for j in range(A_HEADS):
        blk = qa[:, (j // 2) * LANES:(j // 2 + 1) * LANES]
        grp = j // A_GROUP
        if j % 2 != grp:
            blk = pltpu.roll(blk, HEAD_DIM, axis=1)
        keep = (lane < HEAD_DIM) if grp == 0 else (lane >= HEAD_DIM)
        qm_ref[j] = jnp.where(keep, blk, 0.0).astype(BF16)
    m_ref[...] = jnp.full(m_ref.shape, NEG, F32)
    l_ref[...] = jnp.zeros_like(l_ref)
    acc_ref[...] = jnp.zeros_like(acc_ref)

    qall = qm_ref[...].reshape(A_HEADS * tb, LANES)

    def p4(c, carry):
        off = pl.multiple_of(c * tb, tb)
        zt = (_dot_nt(kab_ref[pl.ds(off, tb), :], qall) + bias_ref[jnp.minimum(qi - c, 2)]
              + jnp.concatenate([am_ref[c]] * A_HEADS, axis=1))
        m_new, l_new, acc_new = _softmax_cols(zt, m_ref[...], l_ref[...], acc_ref[...], vat_ref[c])
        m_ref[...] = m_new
        l_ref[...] = l_new
        acc_ref[...] = acc_new
        return carry

    lax.fori_loop(0, nchunks, p4, 0)
    oall = acc_ref[...] / l_ref[...]
    for c2 in range(A_HEADS // 2):
        grp = (2 * c2) // A_GROUP
        lo = oall[:, (2 * c2) * tb:(2 * c2 + 1) * tb].T
        hi = oall[:, (2 * c2 + 1) * tb:(2 * c2 + 2) * tb].T
        if grp == 0:
            hi = pltpu.roll(hi, HEAD_DIM, axis=1)
        else:
            lo = pltpu.roll(lo, HEAD_DIM, axis=1)
        o_ref[0, :, c2 * LANES:(c2 + 1) * LANES] = jnp.where(lane < HEAD_DIM, lo, hi)


def _dsa_p(q_a, k_a, v_a, q_i, k_i, w_i, bias, n_sel):
    b, t, _ = q_a.shape
    tb = LANES
    nq = t // tb
    in_specs = [pl.BlockSpec((1, tb, q_a.shape[2]), lambda i, j: (i, j, 0)),
                pl.BlockSpec((1, t, LANES), lambda i, j: (i, 0, 0)),
                pl.BlockSpec((1, t, LANES), lambda i, j: (i, 0, 0)),
                pl.BlockSpec((1, tb, q_i.shape[2]), lambda i, j: (i, j, 0)),
                pl.BlockSpec((1, t, IDX_DIM), lambda i, j: (i, 0, 0)),
                pl.BlockSpec((1, tb, LANES), lambda i, j: (i, j, 0)),
                pl.BlockSpec(bias.shape, lambda i, j: (0, 0, 0))]
    return pl.pallas_call(
        functools.partial(_dsa_p_kernel, n_sel=n_sel, nk=nq),
        grid=(b, nq), in_specs=in_specs,
        out_specs=pl.BlockSpec((1, tb, q_a.shape[2]), lambda i, j: (i, j, 0)),
        out_shape=jax.ShapeDtypeStruct(q_a.shape, F32),
        scratch_shapes=[pltpu.VMEM((t, LANES), BF16), pltpu.VMEM((nq, LANES, tb), BF16),
                        pltpu.VMEM((t, IDX_DIM), BF16),
                        pltpu.VMEM((nq, tb, tb), jnp.int32), pltpu.VMEM((nq, tb, tb), F32),
                        pltpu.VMEM((A_HEADS, tb, LANES), BF16),
                        pltpu.VMEM((1, A_HEADS * tb), F32), pltpu.VMEM((1, A_HEADS * tb), F32),
                        pltpu.VMEM((LANES, A_HEADS * tb), F32)],
        compiler_params=_cparams(("arbitrary", "arbitrary")), name="attn_dsa")(q_a, k_a, v_a, q_i, k_i, w_i, bias)


def _fox_s_kernel(pt_ref, q_ref, kn_ref, vn_ref, lfn_ref, *refs, ts, pp):
    kt_refs = refs[:pp]
    vt_refs = refs[pp:2 * pp]
    lft_refs = refs[2 * pp:3 * pp]
    o_ref = refs[3 * pp]
    qbd_ref, kpad_ref, vpad_ref, lfpad_ref, m_ref, l_ref, acc_ref, carry_ref, cq_ref = refs[3 * pp + 1:]
    b = pl.program_id(0)
    s = pl.program_id(1)
    nrow = ts * C_HEADS
    width = C_HEADS * HEAD_DIM
    row = lax.broadcasted_iota(jnp.int32, (PAGE, PAGE), 0)
    col = lax.broadcasted_iota(jnp.int32, (PAGE, PAGE), 1)

    @pl.when(jnp.logical_and(b == 0, s == 0))
    def _():
        kpad_ref[...] = jnp.zeros_like(kpad_ref)
        vpad_ref[...] = jnp.zeros_like(vpad_ref)

    def update(z, pv):
        m_new, l_new, acc_new = _softmax_rows(z, m_ref[...], l_ref[...], acc_ref[...], pv)
        m_ref[...] = m_new
        l_ref[...] = l_new
        acc_ref[...] = acc_new

    @pl.when(s == 0)
    def _():
        hmask = (lax.broadcasted_iota(jnp.int32, (C_HEADS, width), 1) // HEAD_DIM
                 == lax.broadcasted_iota(jnp.int32, (C_HEADS, width), 0))
        q = q_ref[0] * (HEAD_DIM ** -0.5)
        for i in range(ts):
            qbd_ref[i * C_HEADS:(i + 1) * C_HEADS, :] = jnp.where(
                hmask, jnp.broadcast_to(q[i:i + 1, :], (C_HEADS, width)), 0.0).astype(BF16)
        kpad_ref[0:ts, :] = kn_ref[0]
        vpad_ref[0:ts, :] = vn_ref[0]
        lfpad_ref[...] = jnp.zeros_like(lfpad_ref)
        lfpad_ref[0:ts, :] = lfn_ref[0]
        m_ref[...] = jnp.full(m_ref.shape, NEG, F32)
        l_ref[...] = jnp.zeros_like(l_ref)
        acc_ref[...] = jnp.zeros_like(acc_ref)
        carry_ref[...] = jnp.zeros_like(carry_ref)
        lft = lfpad_ref[...].T[0:C_HEADS, :]
        incl = jnp.where(row <= col, 1.0, 0.0).astype(F32)
        cnew = _dot_hi(lft, incl)
        for i in range(ts):
            cq_ref[i * C_HEADS:(i + 1) * C_HEADS, :] = jnp.broadcast_to(cnew[:, i:i + 1], (C_HEADS, PAGE))
        z = _dot_nt(qbd_ref[...], kpad_ref[...].astype(BF16))
        z = z + cq_ref[...] - jnp.concatenate([cnew] * ts, axis=0)
        rr = lax.broadcasted_iota(jnp.int32, (nrow, PAGE), 0)
        cc = lax.broadcasted_iota(jnp.int32, (nrow, PAGE), 1)
        z = jnp.where(cc * C_HEADS <= rr, z, NEG)
        vnew = vpad_ref[...].astype(BF16)
        update(z, lambda p: _dot(p, vnew))

    later = jnp.where(row > col, 1.0, 0.0).astype(F32)
    carry = carry_ref[...]
    sufs = []
    for k in range(pp):
        lft = lft_refs[k][...]
        sufs.append(_dot_hi(lft, later) + carry)
        carry = carry + jnp.sum(lft, axis=1, keepdims=True)
    carry_ref[...] = carry
    suf = jnp.concatenate(sufs, axis=1)
    kt = jnp.concatenate([kt_refs[k][...].astype(BF16) for k in range(pp)], axis=1)
    vt = jnp.concatenate([vt_refs[k][...].astype(BF16) for k in range(pp)], axis=1)
    z = _dot(qbd_ref[...], kt)
    z = z + jnp.concatenate([cq_ref[...]] * pp, axis=1) + jnp.concatenate([suf] * ts, axis=0)
    update(z, lambda p: _dot_nt(p, vt))

    @pl.when(s == pl.num_programs(1) - 1)
    def _():
        hmask = (lax.broadcasted_iota(jnp.int32, (C_HEADS, width), 1) // HEAD_DIM
                 == lax.broadcasted_iota(jnp.int32, (C_HEADS, width), 0))
        o = acc_ref[...] / l_ref[...]
        for i in range(ts):
            blk = jnp.where(hmask, o[i * C_HEADS:(i + 1) * C_HEADS, :], 0.0)
            o_ref[0, i:i + 1, :] = jnp.sum(blk, axis=0, keepdims=True)


def _fox_s(q, kn, vn, lfn, cache_kt, cache_vt, cache_lft, layer, page_table, pp):
    b, ts, width = q.shape
    npg = page_table.shape[1]
    nrow = ts * C_HEADS

    def page_idx(k):
        return lambda i, s, pt: (layer, pt[i, npg - 1 - (s * pp + k)], 0, 0)

    def seq_idx(i, s, pt):
        return (i, 0, 0)

    in_specs = [pl.BlockSpec((1, ts, width), seq_idx), pl.BlockSpec((1, ts, width), seq_idx),
                pl.BlockSpec((1, ts, width), seq_idx), pl.BlockSpec((1, ts, LANES), seq_idx)]
    in_specs += [pl.BlockSpec((None, None, width, PAGE), page_idx(k)) for k in range(pp)]
    in_specs += [pl.BlockSpec((None, None, width, PAGE), page_idx(k)) for k in range(pp)]
    in_specs += [pl.BlockSpec((None, None, C_HEADS, PAGE), page_idx(k)) for k in range(pp)]
    grid_spec = pltpu.PrefetchScalarGridSpec(
        num_scalar_prefetch=1, grid=(b, npg // pp), in_specs=in_specs,
        out_specs=pl.BlockSpec((1, ts, width), seq_idx),
        scratch_shapes=[pltpu.VMEM((nrow, width), BF16), pltpu.VMEM((PAGE, width), F32),
                        pltpu.VMEM((PAGE, width), F32), pltpu.VMEM((PAGE, LANES), F32),
                        pltpu.VMEM((nrow, 1), F32), pltpu.VMEM((nrow, 1), F32),
                        pltpu.VMEM((nrow, width), F32), pltpu.VMEM((C_HEADS, PAGE), F32),
                        pltpu.VMEM((nrow, PAGE), F32)])
    return pl.pallas_call(
        functools.partial(_fox_s_kernel, ts=ts, pp=pp), grid_spec=grid_spec,
        out_shape=jax.ShapeDtypeStruct((b, ts, width), F32),
        compiler_params=_cparams(("arbitrary", "arbitrary")), name="attn_fox_sample")(
            page_table, q, kn, vn, lfn, *([cache_kt] * pp), *([cache_vt] * pp), *([cache_lft] * pp))


def _diff_s_kernel(pt_ref, q_ref, kn_ref, vn_ref, *refs, ts, pp, npg, lam_init):
    k_refs = refs[:pp]
    v_refs = refs[pp:2 * pp]
    (bias_ref, biasn_ref, lamp_ref, subln_ref, o_ref,
     qx_ref, kpad_ref, vpad_ref, m_ref, l_ref, acc_ref) = refs[2 * pp:]
    b = pl.program_id(0)
    s = pl.program_id(1)
    hrows = B_HEADS * ts
    lane = lax.broadcasted_iota(jnp.int32, (ts, LANES), 1)

    @pl.when(jnp.logical_and(b == 0, s == 0))
    def _():
        kpad_ref[...] = jnp.zeros_like(kpad_ref)
        vpad_ref[...] = jnp.zeros_like(vpad_ref)

    @pl.when(s == 0)
    def _():
        q = q_ref[0] * (HEAD_DIM ** -0.5)
        for n in range(B_HEADS):
            blk = q[:, n * LANES:(n + 1) * LANES]
            qx_ref[n * ts:(n + 1) * ts, :] = jnp.where(lane < HEAD_DIM, blk, 0.0)
            qx_ref[hrows + n * ts:hrows + (n + 1) * ts, :] = jnp.where(lane >= HEAD_DIM, blk, 0.0)
        m_ref[...] = jnp.full(m_ref.shape, NEG, F32)
        l_ref[...] = jnp.zeros_like(l_ref)
        acc_ref[...] = jnp.zeros_like(acc_ref)

    def step(kall, vall, bias):
        z = _dot_nt(qx_ref[...].astype(BF16), kall) + bias
        m_new, l_new, acc_new = _softmax_rows(z, m_ref[...], l_ref[...], acc_ref[...], lambda p: _dot(p, vall))
        m_ref[...] = m_new
        l_ref[...] = l_new
        acc_ref[...] = acc_new

    step(jnp.concatenate([k_refs[k][...].astype(BF16) for k in range(pp)], axis=0),
         jnp.concatenate([v_refs[k][...].astype(BF16) for k in range(pp)], axis=0), bias_ref[s])

    @pl.when(s == pl.num_programs(1) - 1)
    def _():
        kpad_ref[0:ts * B_HEADS, :] = kn_ref[0]
        vpad_ref[0:ts * B_HEADS, :] = vn_ref[0]
        step(kpad_ref[...].astype(BF16), vpad_ref[...].astype(BF16), biasn_ref[...])
        lp = lamp_ref[...]
        lam = (jnp.exp(jnp.sum(lp[0:1] * lp[1:2], axis=1, keepdims=True))
               - jnp.exp(jnp.sum(lp[2:3] * lp[3:4], axis=1, keepdims=True)) + lam_init)
        o = acc_ref[...] / l_ref[...]
        od = o[0:hrows] - lam * o[hrows:2 * hrows]
        for n in range(B_HEADS):
            o_ref[0, :, n * B_VDIM:(n + 1) * B_VDIM] = (
                _rms(od[n * ts:(n + 1) * ts], subln_ref[...]) * (1.0 - lam_init))


def _diff_s(q, kn16, vn16, cache_k, cache_v, layer, page_table, bias, bias_new, lamp, subln, lam_init, pp):
    b, ts, width = q.shape
    npg = page_table.shape[1]
    nrow = 2 * B_HEADS * ts
    krows = PAGE * B_HEADS

    def page_idx(k):
        return lambda i, s, pt: (layer, pt[i, s * pp + k], 0, 0)

    def seq_idx(i, s, pt):
        return (i, 0, 0)

    in_specs = [pl.BlockSpec((1, ts, width), seq_idx), pl.BlockSpec((1,) + kn16.shape[1:], seq_idx),
                pl.BlockSpec((1,) + vn16.shape[1:], seq_idx)]
    in_specs += [pl.BlockSpec((None, None, krows, LANES), page_idx(k)) for k in range(pp)] * 2
    in_specs += [pl.BlockSpec(bias.shape, lambda i, s, pt: (0, 0, 0)),
                 pl.BlockSpec(bias_new.shape, lambda i, s, pt: (0, 0)),
                 pl.BlockSpec(lamp.shape, lambda i, s, pt: (0, 0)),
                 pl.BlockSpec(subln.shape, lambda i, s, pt: (0, 0))]
    grid_spec = pltpu.PrefetchScalarGridSpec(
        num_scalar_prefetch=1, grid=(b, npg // pp), in_specs=in_specs,
        out_specs=pl.BlockSpec((1, ts, width), seq_idx),
        scratch_shapes=[pltpu.VMEM((nrow, LANES), F32), pltpu.VMEM((krows, LANES), F32),
                        pltpu.VMEM((krows, LANES), F32), pltpu.VMEM((nrow, 1), F32),
                        pltpu.VMEM((nrow, 1), F32), pltpu.VMEM((nrow, LANES), F32)])
    return pl.pallas_call(
        functools.partial(_diff_s_kernel, ts=ts, pp=pp, npg=npg, lam_init=lam_init), grid_spec=grid_spec,
        out_shape=jax.ShapeDtypeStruct((b, ts, width), F32),
        compiler_params=_cparams(("arbitrary", "arbitrary")), name="attn_diff_sample")(
            page_table, q, kn16, vn16, *([cache_k] * pp), *([cache_v] * pp), bias, bias_new, lamp, subln)


def _dsa_s_kernel(pt_ref, qa_ref, qi_ref, wi_ref, kn_ref, vn_ref, kin_ref, *refs, ts, pp, n_sel, nchunk):
    kt_refs = refs[:pp]
    vt_refs = refs[pp:2 * pp]
    kit_refs = refs[2 * pp:3 * pp]
    bias_ref, o_ref, kst_ref, vst_ref, pad_ref, key_ref, am_ref = refs[3 * pp:]
    b = pl.program_id(0)
    s = pl.program_id(1)
    last = nchunk - 1
    rpad = SUBLANES

    @pl.when(jnp.logical_and(b == 0, s == 0))
    def _():
        pad_ref[...] = jnp.zeros_like(pad_ref)

    def index_scores(kit):
        s32 = jnp.maximum(_dot(qi_ref[0].astype(BF16), kit.astype(BF16)), 0.0) * wi_ref[0]
        sc = s32[0:rpad]
        for h in range(1, IDX_HEADS):
            sc = sc + s32[h * rpad:(h + 1) * rpad]
        return sc

    for k in range(pp):
        chunk = s * pp + k
        kst_ref[chunk] = kt_refs[k][...]
        vst_ref[chunk] = vt_refs[k][...]
        key_ref[chunk] = _sort_key(index_scores(kit_refs[k][...]))

    @pl.when(s == pl.num_programs(1) - 1)
    def _():
        pad_ref[0, 0:ts, :] = kn_ref[0]
        pad_ref[1, 0:ts, :] = vn_ref[0]
        pad_ref[2, 0:ts, 0:IDX_DIM] = kin_ref[0]
        kst_ref[last] = pad_ref[0].T
        vst_ref[last] = pad_ref[1].T
        rr = lax.broadcasted_iota(jnp.int32, (rpad, PAGE), 0)
        cc = lax.broadcasted_iota(jnp.int32, (rpad, PAGE), 1)
        causal = jnp.logical_and(cc <= rr, cc < ts)
        key_ref[last] = _sort_key(jnp.where(causal, index_scores(pad_ref[2].T[0:IDX_DIM, :]), -jnp.inf))

        def count(pred):
            acc = jnp.zeros((rpad, PAGE), F32)
            for c in range(nchunk):
                acc = acc + jnp.where(pred(key_ref[c]), 1.0, 0.0)
            return jnp.sum(acc, axis=1, keepdims=True)

        thr = _kth_largest(lambda trial: count(lambda kk: kk >= trial), n_sel, (rpad, 1))
        need = n_sel - count(lambda kk: kk > thr)
        r2 = lax.broadcasted_iota(jnp.int32, (PAGE, PAGE), 0)
        c2 = lax.broadcasted_iota(jnp.int32, (PAGE, PAGE), 1)
        ustrict = jnp.where(r2 < c2, 1.0, 0.0).astype(BF16)
        carry = jnp.zeros((rpad, 1), F32)
        for c in range(nchunk):
            kk = key_ref[c]
            eq = kk == thr
            eqf = jnp.where(eq, 1.0, 0.0)
            prefix = _dot(eqf.astype(BF16), ustrict) + carry
            sel = jnp.logical_or(kk > thr, jnp.logical_and(eq, prefix < need))
            if c == last:
                sel = jnp.logical_and(sel, causal)
            am_ref[c] = jnp.where(sel, 0.0, NEG)
            carry = carry + jnp.sum(eqf, axis=1, keepdims=True)

        nrow = A_HEADS * rpad
        qbd = qa_ref[0].astype(BF16)

        zs = [_dot(qbd, kst_ref[c].astype(BF16)) + bias_ref[c] + jnp.concatenate([am_ref[c]] * A_HEADS, axis=0)
              for c in range(nchunk)]
        zmax = zs[0]
        for c in range(1, nchunk):
            zmax = jnp.maximum(zmax, zs[c])
        m_row = jnp.max(zmax, axis=1, keepdims=True)
        psum = jnp.zeros((nrow, LANES), F32)
        acc = jnp.zeros((nrow, LANES), F32)
        for c in range(nchunk):
            p = jnp.exp(zs[c] - m_row)
            psum = psum + p
            acc = acc + _dot_nt(p.astype(BF16), vst_ref[c].astype(BF16))
        o = acc / jnp.sum(psum, axis=1, keepdims=True)
        lane = lax.broadcasted_iota(jnp.int32, (rpad, LANES), 1)
        for blk in range(A_HEADS // 2):
            grp = (2 * blk) // A_GROUP
            lo = o[(2 * blk) * rpad:(2 * blk + 1) * rpad]
            hi = o[(2 * blk + 1) * rpad:(2 * blk + 2) * rpad]
            if grp == 0:
                hi = pltpu.roll(hi, HEAD_DIM, axis=1)
            else:
                lo = pltpu.roll(lo, HEAD_DIM, axis=1)
            res = jnp.where(lane < HEAD_DIM, lo, hi)
            o_ref[0, :, blk * LANES:(blk + 1) * LANES] = res[0:ts]


def _dsa_s(qa_bd, qi32, w32, kn, vn, kin, cache_kt, cache_vt, cache_kit, layer, page_table, bias, n_sel, ts, pp):
    b = qa_bd.shape[0]
    npg = page_table.shape[1]
    nchunk = npg + 1
    width = A_HEADS * HEAD_DIM

    def page_idx(k):
        return lambda i, s, pt: (layer, pt[i, s * pp + k], 0, 0)

    def seq_idx(i, s, pt):
        return (i, 0, 0)

    in_specs = [pl.BlockSpec((1,) + qa_bd.shape[1:], seq_idx), pl.BlockSpec((1,) + qi32.shape[1:], seq_idx),
                pl.BlockSpec((1,) + w32.shape[1:], seq_idx),
                pl.BlockSpec((1, ts, LANES), seq_idx), pl.BlockSpec((1, ts, LANES), seq_idx),
                pl.BlockSpec((1, ts, IDX_DIM), seq_idx)]
    in_specs += [pl.BlockSpec((None, None, LANES, PAGE), page_idx(k)) for k in range(pp)] * 2
    in_specs += [pl.BlockSpec((None, None, IDX_DIM, PAGE), page_idx(k)) for k in range(pp)]
    in_specs += [pl.BlockSpec(bias.shape, lambda i, s, pt: (0, 0, 0))]
    grid_spec = pltpu.PrefetchScalarGridSpec(
        num_scalar_prefetch=1, grid=(b, npg // pp), in_specs=in_specs,
        out_specs=pl.BlockSpec((1, ts, width), seq_idx),
        scratch_shapes=[pltpu.VMEM((nchunk, LANES, PAGE), F32), pltpu.VMEM((nchunk, LANES, PAGE), F32),
                        pltpu.VMEM((3, PAGE, LANES), F32),
                        pltpu.VMEM((nchunk, SUBLANES, PAGE), jnp.int32), pltpu.VMEM((nchunk, SUBLANES, PAGE), F32)])
    return pl.pallas_call(
        functools.partial(_dsa_s_kernel, ts=ts, pp=pp, n_sel=n_sel, nchunk=nchunk), grid_spec=grid_spec,
        out_shape=jax.ShapeDtypeStruct((b, ts, width), F32),
        compiler_params=_cparams(("arbitrary", "arbitrary")), name="attn_dsa_sample")(
            page_table, qa_bd, qi32, w32, kn, vn, kin,
            *([cache_kt] * pp), *([cache_vt] * pp), *([cache_kit] * pp), bias)


def _t5_bucket(rel):
    n = jnp.maximum(rel, 0)
    exact = N_BUCKETS // 2
    nf = jnp.maximum(n, 1).astype(F32)
    log_b = exact + (jnp.log(nf / exact) / math.log(MAX_DISTANCE / exact) * (N_BUCKETS - exact)).astype(jnp.int32)
    return jnp.where(n < exact, n, jnp.minimum(log_b, N_BUCKETS - 1))


def _bias_of(tab, rel):
    bucket = _t5_bucket(rel)[None]
    tab = tab.astype(F32)
    out = jnp.broadcast_to(tab[0].reshape((-1,) + (1,) * rel.ndim), (tab.shape[1],) + rel.shape)
    for bkt in range(1, N_BUCKETS):
        out = jnp.where(bucket == bkt, tab[bkt].reshape((-1,) + (1,) * rel.ndim), out)
    return out


def _bias_tiles_t(tab, tb):
    j = jnp.arange(tb, dtype=jnp.int32)[:, None]
    i = jnp.arange(tb, dtype=jnp.int32)[None, :]
    return _bias_of(tab, jnp.stack([d * tb + i - j for d in range(3)]))


def _split(w, sizes):
    offs = np.cumsum((0,) + tuple(sizes))
    return [w[:, int(offs[i]):int(offs[i + 1])] for i in range(len(sizes))]


def _pad_cols(w, n):
    return jnp.pad(w, ((0, 0), (0, n - w.shape[1])))


def _row_tile(m, want):
    t = min(m, want)
    while m % t:
        t //= 2
    return t


def _pages_per_step(npg, want):
    pp = min(npg, want)
    while npg % pp:
        pp -= 1
    return pp


def kernel(x_prompt, x_sample, cache_a_k, cache_a_v, cache_a_kidx, cache_b_k, cache_b_v, cache_c_k, cache_c_v,
           cache_c_logf, state_conv, page_table, rel_bias_table, w_in_even, w_out_even, lambda_q1, lambda_k1,
           lambda_q2, lambda_k2, diff_subln, w_in_odd, b_forget, w_out_odd, norm_mix_pre, norm_mix_post,
           norm_ffn_pre, norm_ffn_post, w_gate_up, w_conv, b_conv, w_down):
    bp, tp, d = x_prompt.shape
    bs, ts, _ = x_sample.shape
    depth = w_gate_up.shape[0]
    n_pool = cache_a_k.shape[1]
    npg = page_table.shape[1]
    past = npg * PAGE
    mp, ms = bp * tp, bs * ts
    tb_dsa = LANES
    tb_attn = min(512, tp)
    nsel_p = min(TOPK_MAX, tp // 4)
    nsel_s = min(TOPK_MAX, (past + ts) // 4)
    tm_p = _row_tile(mp, 512)
    tm_s = _row_tile(ms, 512)
    tf = D_FF // 2
    dff = D_FF
    pp = _pages_per_step(npg, 4)

    xp = x_prompt.reshape(mp, d)
    xs = x_sample.reshape(ms, d)
    tab_a = rel_bias_table[:, :A_HEADS]
    tab_b = rel_bias_table[:, A_HEADS:]
    bias_a_p = _bias_tiles_t(tab_a, tb_dsa)
    bias_a_p = jnp.transpose(bias_a_p, (1, 2, 0, 3)).reshape(3, tb_dsa, A_HEADS * tb_dsa)
    bias_b_p = _bias_tiles_t(tab_b, tb_attn)
    rel_s = (past + jnp.arange(ts, dtype=jnp.int32))[:, None] - jnp.arange(past + PAGE, dtype=jnp.int32)[None, :]
    rows_a = _bias_of(tab_a, rel_s).reshape(A_HEADS, ts, npg + 1, PAGE)
    rows_b = _bias_of(tab_b, rel_s).reshape(B_HEADS, ts, npg + 1, PAGE)
    bias_a_s = jnp.pad(jnp.moveaxis(rows_a, 2, 0), ((0, 0), (0, 0), (0, SUBLANES - ts), (0, 0)))
    bias_a_s = bias_a_s.reshape(npg + 1, A_HEADS * SUBLANES, PAGE)
    kq_ok = rel_s.reshape(ts, npg + 1, PAGE) >= 0
    same = jnp.eye(B_HEADS, dtype=bool)
    bias_b_s = jnp.where(same[:, None, None, None, :] & kq_ok[None, :, :, :, None],
                         rows_b[..., None], NEG)
    bias_b_s = jnp.moveaxis(bias_b_s, 2, 0).reshape(npg + 1, B_HEADS * ts, PAGE * B_HEADS)
    bias_b_s = jnp.concatenate([bias_b_s, bias_b_s], axis=1)
    bias_b_new = bias_b_s[npg]
    bias_b_s = bias_b_s[:npg].reshape(npg // pp, pp, 2 * B_HEADS * ts, PAGE * B_HEADS)
    bias_b_s = jnp.swapaxes(bias_b_s, 1, 2).reshape(npg // pp, 2 * B_HEADS * ts, pp * PAGE * B_HEADS)

    ca_kt = jnp.transpose(cache_a_k, (0, 1, 3, 4, 2)).reshape(-1, n_pool, LANES, PAGE)
    ca_vt = jnp.transpose(cache_a_v, (0, 1, 3, 4, 2)).reshape(-1, n_pool, LANES, PAGE)
    ca_kit = jnp.transpose(cache_a_kidx, (0, 1, 3, 2))
    cb_k = cache_b_k.reshape(-1, n_pool, PAGE * B_HEADS, 2 * HEAD_DIM)
    cb_v = cache_b_v.reshape(-1, n_pool, PAGE * B_HEADS, B_VDIM)
    cc_kt = jnp.transpose(cache_c_k, (0, 1, 3, 4, 2)).reshape(-1, n_pool, C_HEADS * HEAD_DIM, PAGE)
    cc_vt = jnp.transpose(cache_c_v, (0, 1, 3, 4, 2)).reshape(-1, n_pool, C_HEADS * HEAD_DIM, PAGE)
    cc_lft = jnp.transpose(cache_c_logf, (0, 1, 3, 2))

    even_rows_p, even_rows_s, odd_rows_p, odd_rows_s, conv_p, conv_s = [], [], [], [], [], []
    for l in range(depth):
        if l % 2 == 0:
            e = l // 2
            lam_init = 0.8 - 0.6 * math.exp(-0.3 * l)
            ws = _split(w_in_even[e], EVEN_SPLIT)
            ws[5] = _pad_cols(ws[5], LANES)
            ws = [w.astype(BF16) for w in ws]
            wo = w_out_even[e].astype(BF16)
            wo_parts = [wo[:A_HEADS * HEAD_DIM], wo[A_HEADS * HEAD_DIM:]]
            lamp = jnp.stack([lambda_q1[e], lambda_k1[e], lambda_q2[e], lambda_k2[e]]).astype(F32)
            subln = diff_subln[e].reshape(1, B_VDIM)

            q_a, k_a, v_a, q_i, k_i, w_i, q_b, k_b, v_b = _rms_proj(xp, norm_mix_pre[l], ws, tm_p)
            r3 = lambda a: a.reshape(bp, tp, a.shape[1])
            o_a = _dsa_p(r3(q_a), r3(k_a), r3(v_a), r3(q_i), r3(k_i), r3(w_i), bias_a_p, nsel_p)
            o_b = _flash2("diff", r3(q_b), r3(k_b), r3(v_b), (bias_b_p, lamp, subln), tb_attn, lam_init)
            xp = _proj_post(xp, norm_mix_post[l], [o_a.reshape(mp, -1), o_b.reshape(mp, -1)], wo_parts, tm_p)
            even_rows_p.append((k_a.reshape(bp, tp, A_KV_HEADS, HEAD_DIM), v_a.reshape(bp, tp, A_KV_HEADS, HEAD_DIM),
                                k_i.reshape(bp, tp, IDX_DIM), k_b.reshape(bp, tp, B_HEADS, 2 * HEAD_DIM),
                                v_b.reshape(bp, tp, B_HEADS, B_VDIM)))

            q_a, k_a, v_a, q_i, k_i, w_i, q_b, k_b, v_b = _rms_proj(xs, norm_mix_pre[l], ws, tm_s)
            s3 = lambda a: a.reshape(bs, ts, a.shape[1])
            qa4 = (q_a * (HEAD_DIM ** -0.5)).reshape(bs, ts, A_HEADS, HEAD_DIM)
            qa4 = jnp.pad(jnp.moveaxis(qa4, 1, 2), ((0, 0), (0, 0), (0, SUBLANES - ts), (0, 0)))
            zeros = jnp.zeros_like(qa4)
            qa_bd = jnp.concatenate([jnp.concatenate([qa4[:, :A_GROUP], zeros[:, :A_GROUP]], axis=-1),
                                     jnp.concatenate([zeros[:, A_GROUP:], qa4[:, A_GROUP:]], axis=-1)], axis=1)
            qa_bd = qa_bd.reshape(bs, A_HEADS * SUBLANES, LANES)
            qi4 = jnp.moveaxis(q_i.reshape(bs, ts, IDX_HEADS, IDX_DIM), 1, 2)
            qi32 = jnp.pad(qi4, ((0, 0), (0, 0), (0, SUBLANES - ts), (0, 0))).reshape(bs, IDX_HEADS * SUBLANES, IDX_DIM)
            w4 = jnp.moveaxis(w_i[:, :IDX_HEADS].reshape(bs, ts, IDX_HEADS), 1, 2) * (IDX_DIM ** -0.5 * IDX_HEADS ** -0.5)
            w32 = jnp.pad(w4, ((0, 0), (0, 0), (0, SUBLANES - ts))).reshape(bs, IDX_HEADS * SUBLANES, 1)
            w32 = jnp.broadcast_to(w32, (bs, IDX_HEADS * SUBLANES, PAGE))
            o_a = _dsa_s(qa_bd, qi32, w32, s3(k_a), s3(v_a), s3(k_i), ca_kt, ca_vt, ca_kit,
                         e, page_table, bias_a_s, nsel_s, ts, pp)
            o_b = _diff_s(s3(q_b), k_b.reshape(bs, ts * B_HEADS, 2 * HEAD_DIM), v_b.reshape(bs, ts * B_HEADS, B_VDIM),
                          cb_k, cb_v, e, page_table, bias_b_s, bias_b_new, lamp, subln, lam_init, pp)
            xs = _proj_post(xs, norm_mix_post[l], [o_a.reshape(ms, -1), o_b.reshape(ms, -1)], wo_parts, tm_s)
            even_rows_s.append((k_a.reshape(bs, ts, A_KV_HEADS, HEAD_DIM), v_a.reshape(bs, ts, A_KV_HEADS, HEAD_DIM),
                                k_i.reshape(bs, ts, IDX_DIM), k_b.reshape(bs, ts, B_HEADS, 2 * HEAD_DIM),
                                v_b.reshape(bs, ts, B_HEADS, B_VDIM)))
        else:
            o = l // 2
            ws = _split(w_in_odd[o], ODD_SPLIT)
            ws[3] = _pad_cols(ws[3], LANES)
            ws = [w.astype(BF16) for w in ws]
            bfp = _pad_cols(b_forget[o].reshape(1, C_HEADS), LANES)
            wo = [w_out_odd[o].astype(BF16)]

            q, k, v, lf = _rms_proj(xp, norm_mix_pre[l], ws, tm_p, logsig_bias=bfp)
            r3 = lambda a: a.reshape(bp, tp, a.shape[1])
            cum, cumt = _cumsum(r3(lf), tb_attn)
            o_c = _flash2("fox", r3(q), r3(k), r3(v), (cum, cumt), tb_attn)
            xp = _proj_post(xp, norm_mix_post[l], [o_c.reshape(mp, -1)], wo, tm_p)
            odd_rows_p.append((k.reshape(bp, tp, C_HEADS, HEAD_DIM), v.reshape(bp, tp, C_HEADS, HEAD_DIM),
                               lf[:, :C_HEADS].reshape(bp, tp, C_HEADS)))

            q, k, v, lf = _rms_proj(xs, norm_mix_pre[l], ws, tm_s, logsig_bias=bfp)
            s3 = lambda a: a.reshape(bs, ts, a.shape[1])
            o_c = _fox_s(s3(q), s3(k), s3(v), s3(lf), cc_kt, cc_vt, cc_lft, o, page_table, pp)
            xs = _proj_post(xs, norm_mix_post[l], [o_c.reshape(ms, -1)], wo, tm_s)
            odd_rows_s.append((k.reshape(bs, ts, C_HEADS, HEAD_DIM), v.reshape(bs, ts, C_HEADS, HEAD_DIM),
                               lf[:, :C_HEADS].reshape(bs, ts, C_HEADS)))

        wgu = w_gate_up[l].astype(BF16)
        wg, wu = wgu[:, :dff], wgu[:, dff:]
        wd = w_down[l].astype(BF16)
        tiles = tp // tm_p
        xp, tail = _ffn(xp, norm_ffn_pre[l], norm_ffn_post[l], wg, wu, w_conv[l], b_conv[l], wd,
                        tm_p, tf, tiles)
        conv_p.append(tail.reshape(bp, tiles, 8, dff)[:, tiles - 1, 8 - (CONV_W - 1):, :])
        st = state_conv[l]
        prev1 = jnp.concatenate([st[:, 1:2], jnp.zeros((bs, ts - 1, dff), F32)], axis=1).reshape(ms, dff)
        prev2 = jnp.concatenate([st, jnp.zeros((bs, ts - 2, dff), F32)], axis=1).reshape(ms, dff)
        xs, gfull = _ffn(xs, norm_ffn_pre[l], norm_ffn_post[l], wg, wu, w_conv[l], b_conv[l], wd,
                         tm_s, tf, ts, prev=(prev1, prev2))
        conv_s.append(gfull.reshape(bs, ts, dff)[:, ts - (CONV_W - 1):, :])

    def stack(rows, i):
        return jnp.stack([r[i] for r in rows])

    ev_p = [stack(even_rows_p, i) for i in range(5)]
    ev_s = [stack(even_rows_s, i) for i in range(5)]
    od_p = [stack(odd_rows_p, i) for i in range(3)]
    od_s = [stack(odd_rows_s, i) for i in range(3)]
    return (xp.reshape(bp, tp, d), xs.reshape(bs, ts, d),
            ev_p[0], ev_s[0], ev_p[1], ev_s[1], ev_p[2], ev_s[2], ev_p[3], ev_s[3], ev_p[4], ev_s[4],
            od_p[0], od_s[0], od_p[1], od_s[1], od_p[2], od_s[2],
            jnp.stack(conv_p), jnp.stack(conv_s))
```

```python
import functools
import math

import jax
import jax.numpy as jnp
import numpy as np
from jax import lax
from jax.experimental import pallas as pl
from jax.experimental.pallas import tpu as pltpu

F32 = jnp.float32
BF16 = jnp.bfloat16
HIGHEST = lax.Precision.HIGHEST

D_MODEL = 1024
HEAD_DIM = 64
A_HEADS = 8
A_KV_HEADS = 2
A_GROUP = A_HEADS // A_KV_HEADS
IDX_HEADS = 4
IDX_DIM = 64
TOPK_MAX = 256
B_HEADS = 4
B_VDIM = 128
C_HEADS = 16
N_BUCKETS = 32
MAX_DISTANCE = 128
D_FF = 2816
CONV_W = 3
EPS = 1e-6
PAGE = 128

LANES = 128
SUBLANES = 8
NEG = -1e30
INT_MIN = -2147483648
VMEM_LIMIT = 48 * 1024 * 1024

EVEN_SPLIT = (512, 128, 128, 256, 64, 4, 512, 512, 512)
ODD_SPLIT = (1024, 1024, 1024, 16)

NT_DIMS = (((1,), (1,)), ((), ()))


def _cparams(sem):
    return pltpu.CompilerParams(dimension_semantics=sem, vmem_limit_bytes=VMEM_LIMIT)


def _rms(x, g):
    return x * lax.rsqrt(jnp.mean(x * x, axis=-1, keepdims=True) + EPS) * g


def _log_sigmoid(x):
    return -(jnp.maximum(-x, 0.0) + jnp.log1p(jnp.exp(-jnp.abs(x))))


def _gelu_tanh(x):
    c = math.sqrt(2.0 / math.pi)
    return x * (0.5 * (1.0 + jnp.tanh(c * (x + 0.044715 * (x * x * x)))))


def _dot(a, b):
    return jnp.dot(a, b, preferred_element_type=F32)


def _dot_nt(a, b):
    return lax.dot_general(a, b, NT_DIMS, preferred_element_type=F32)


def _dot_hi(a, b):
    return jnp.dot(a, b, preferred_element_type=F32, precision=HIGHEST)


def _softmax_rows(z, m_old, l_old, acc_old, pv):
    m_new = jnp.maximum(m_old, jnp.max(z, axis=1, keepdims=True))
    alpha = jnp.exp(m_old - m_new)
    p = jnp.exp(z - m_new)
    l_new = alpha * l_old + jnp.sum(p, axis=1, keepdims=True)
    acc_new = alpha * acc_old + pv(p.astype(BF16))
    return m_new, l_new, acc_new


def _softmax_cols(zt, m_old, l_old, acc_old, vt, shift=None):
    m_tile = jnp.max(zt, axis=0, keepdims=True)
    if shift is None:
        m_new = jnp.maximum(m_old, m_tile)
        p = jnp.exp(zt - m_new)
    else:
        m_new = jnp.maximum(m_old, m_tile + shift)
        p = jnp.exp(zt - (m_new - shift))
    alpha = jnp.exp(m_old - m_new)
    l_new = alpha * l_old + jnp.sum(p, axis=0, keepdims=True)
    acc_new = alpha * acc_old + _dot(vt, p.astype(BF16))
    return m_new, l_new, acc_new


def _sort_key(s):
    s = jnp.where(s == 0.0, 0.0, s)
    bits = pltpu.bitcast(s, jnp.int32)
    return jnp.where(bits < 0, bits ^ jnp.int32(0x7FFFFFFF), bits)


def _kth_largest(count_ge, n_sel, shape):
    def step(it, ans):
        bit = lax.shift_left(jnp.int32(1), jnp.int32(31) - it)
        trial = ans | bit
        cnt = count_ge(trial ^ jnp.int32(INT_MIN))
        return jnp.where(cnt >= n_sel, trial, ans)

    ans = lax.fori_loop(0, 32, step, jnp.zeros(shape, jnp.int32))
    return ans ^ jnp.int32(INT_MIN)


def _rms_proj_kernel(x_ref, g_ref, *refs, kinds, logsig_last):
    n_w = len(kinds)
    w_refs = refs[:n_w]
    if logsig_last:
        bias_ref = refs[n_w]
        o_refs = refs[n_w + 1:]
    else:
        o_refs = refs[n_w:]
    h = _rms(x_ref[...], g_ref[...]).astype(BF16)
    for idx, kind in enumerate(kinds):
        if kind == "col":
            o_refs[idx][0] = _dot_nt(w_refs[idx][...], h)
            continue
        y = _dot(h, w_refs[idx][...])
        if logsig_last and idx == n_w - 1:
            y = _log_sigmoid(y + bias_ref[...])
        o_refs[idx][...] = y.astype(o_refs[idx].dtype)


def _rms_proj(x2d, g, items, tm, logsig_bias=None, seq=None):
    m, d = x2d.shape
    kinds = tuple(kind for _, kind in items)
    in_specs = [pl.BlockSpec((tm, d), lambda i: (i, 0)), pl.BlockSpec((1, d), lambda i: (0, 0))]
    in_specs += [pl.BlockSpec(w.shape, lambda i: (0, 0)) for w, _ in items]
    args = [x2d, g.reshape(1, d)] + [w for w, _ in items]
    if logsig_bias is not None:
        in_specs.append(pl.BlockSpec(logsig_bias.shape, lambda i: (0, 0)))
        args.append(logsig_bias)
    out_shape, out_specs = [], []
    for w, kind in items:
        if kind == "col":
            nb, t = seq
            tiles = t // tm
            out_shape.append(jax.ShapeDtypeStruct((nb, w.shape[0], t), F32))
            out_specs.append(pl.BlockSpec((1, w.shape[0], tm), lambda i, tiles=tiles: (i // tiles, 0, i % tiles)))
        else:
            out_shape.append(jax.ShapeDtypeStruct((m, w.shape[1]), F32 if kind == "f32" else BF16))
            out_specs.append(pl.BlockSpec((tm, w.shape[1]), lambda i: (i, 0)))
    return pl.pallas_call(
        functools.partial(_rms_proj_kernel, kinds=kinds, logsig_last=logsig_bias is not None),
        grid=(m // tm,), in_specs=in_specs, out_specs=out_specs, out_shape=out_shape,
        compiler_params=_cparams(("parallel",)), name="rms_proj")(*args)


def _proj_post_kernel(x_ref, g_ref, *refs, n_in):
    o_refs = refs[:n_in]
    w_refs = refs[n_in:2 * n_in]
    out_ref = refs[2 * n_in]
    acc = None
    for o, w in zip(o_refs, w_refs):
        t = _dot(o[...].astype(BF16), w[...])
        acc = t if acc is None else acc + t
    out_ref[...] = x_ref[...] + _rms(acc, g_ref[...])


def _proj_post(x2d, g, o_list, w_list, tm):
    m, d = x2d.shape
    n_in = len(o_list)
    in_specs = [pl.BlockSpec((tm, d), lambda i: (i, 0)), pl.BlockSpec((1, d), lambda i: (0, 0))]
    in_specs += [pl.BlockSpec((tm, o.shape[1]), lambda i: (i, 0)) for o in o_list]
    in_specs += [pl.BlockSpec(w.shape, lambda i: (0, 0)) for w in w_list]
    return pl.pallas_call(
        functools.partial(_proj_post_kernel, n_in=n_in),
        grid=(m // tm,), in_specs=in_specs, out_specs=pl.BlockSpec((tm, d), lambda i: (i, 0)),
        out_shape=jax.ShapeDtypeStruct((m, d), F32),
        compiler_params=_cparams(("parallel",)), name="proj_post")(x2d, g.reshape(1, d), *o_list, *w_list)


def _ffn_kernel(*refs, tm, tiles_per_seq, sample):
    if sample:
        (x_ref, gpre_ref, gpost_ref, wg_ref, wu_ref, wc_ref, bc_ref, wd_ref, p1_ref, p2_ref,
         y_ref, gout_ref, h_ref, acc_ref, gs_ref) = refs
    else:
        (x_ref, gpre_ref, gpost_ref, wg_ref, wu_ref, wc_ref, bc_ref, wd_ref,
         y_ref, gout_ref, h_ref, acc_ref, gs_ref, halo_ref) = refs
    i = pl.program_id(0)
    c = pl.program_id(1)

    @pl.when(c == 0)
    def _():
        h_ref[...] = _rms(x_ref[...], gpre_ref[...]).astype(BF16)
        acc_ref[...] = jnp.zeros_like(acc_ref)

    h = h_ref[...]
    g = _dot(h, wg_ref[...])
    u = _dot(h, wu_ref[...])
    if sample:
        gs_ref[0:8, :] = jnp.zeros((8, g.shape[1]), F32)
    else:
        first = (i % tiles_per_seq) == 0

        @pl.when(first)
        def _():
            gs_ref[0:8, :] = jnp.zeros((8, g.shape[1]), F32)

        @pl.when(jnp.logical_not(first))
        def _():
            gs_ref[0:8, :] = halo_ref[c]

    gs_ref[8:tm + 8, :] = g
    g1 = gs_ref[7:tm + 7, :]
    g2 = gs_ref[6:tm + 6, :]
    if sample:
        t = lax.broadcasted_iota(jnp.int32, (tm, 1), 0) % tiles_per_seq
        g1 = jnp.where(t >= 1, g1, 0.0) + p1_ref[...]
        g2 = jnp.where(t >= 2, g2, 0.0) + p2_ref[...]
        gout_ref[...] = g
    else:
        tail = gs_ref[tm:tm + 8, :]
        halo_ref[c] = tail
        gout_ref[0] = tail
    w = wc_ref[...]
    gc = bc_ref[...] + w[0:1, :] * g2
    gc = gc + w[1:2, :] * g1
    gc = gc + w[2:3, :] * g
    a = _gelu_tanh(gc) * u
    acc_ref[...] += _dot(a.astype(BF16), wd_ref[...])

    @pl.when(c == pl.num_programs(1) - 1)
    def _():
        y_ref[...] = x_ref[...] + _rms(acc_ref[...], gpost_ref[...])


def _ffn(x2d, gpre, gpost, wg, wu, wc, bc, wd, tm, tf, tiles_per_seq, prev=None):
    m, d = x2d.shape
    dff = wg.shape[1]
    nc = dff // tf
    sample = prev is not None
    in_specs = [
        pl.BlockSpec((tm, d), lambda i, c: (i, 0)),
        pl.BlockSpec((1, d), lambda i, c: (0, 0)),
        pl.BlockSpec((1, d), lambda i, c: (0, 0)),
        pl.BlockSpec((d, tf), lambda i, c: (0, c)),
        pl.BlockSpec((d, tf), lambda i, c: (0, c)),
        pl.BlockSpec((CONV_W, tf), lambda i, c: (0, c)),
        pl.BlockSpec((1, tf), lambda i, c: (0, c)),
        pl.BlockSpec((tf, d), lambda i, c: (c, 0)),
    ]
    args = [x2d, gpre.reshape(1, d), gpost.reshape(1, d), wg, wu, wc, bc.reshape(1, dff), wd]
    scratch = [pltpu.VMEM((tm, d), BF16), pltpu.VMEM((tm, d), F32), pltpu.VMEM((tm + 8, tf), F32)]
    if sample:
        in_specs += [pl.BlockSpec((tm, tf), lambda i, c: (i, c))] * 2
        args += list(prev)
        gout_shape = jax.ShapeDtypeStruct((m, dff), F32)
        gout_spec = pl.BlockSpec((tm, tf), lambda i, c: (i, c))
    else:
        gout_shape = jax.ShapeDtypeStruct((m // tm, 8, dff), F32)
        gout_spec = pl.BlockSpec((1, 8, tf), lambda i, c: (i, 0, c))
        scratch.append(pltpu.VMEM((nc, 8, tf), F32))
    return pl.pallas_call(
        functools.partial(_ffn_kernel, tm=tm, tiles_per_seq=tiles_per_seq, sample=sample),
        grid=(m // tm, nc), in_specs=in_specs,
        out_specs=[pl.BlockSpec((tm, d), lambda i, c: (i, 0)), gout_spec],
        out_shape=[jax.ShapeDtypeStruct((m, d), F32), gout_shape],
        scratch_shapes=scratch,
        compiler_params=_cparams(("arbitrary", "arbitrary")), name="conv_ffn")(*args)


def _cumsum_kernel(lf_ref, cum_ref, cumt_ref, *, tb, nchunk):
    row = lax.broadcasted_iota(jnp.int32, (tb, tb), 0)
    col = lax.broadcasted_iota(jnp.int32, (tb, tb), 1)
    lower = jnp.where(col <= row, 1.0, 0.0).astype(F32)
    carry = jnp.zeros((1, LANES), F32)
    for c in range(nchunk):
        x = lf_ref[0, c * tb:(c + 1) * tb, :]
        ct = _dot_hi(lower, x) + carry
        cumt_ref[0, c * tb:(c + 1) * tb, :] = ct
        carry = ct[tb - 1:tb, :]
        cum_ref[0, c] = ct.T[0:C_HEADS, :]


def _cumsum(lf3d, tb):
    b, t, _ = lf3d.shape
    nchunk = t // tb
    return pl.pallas_call(
        functools.partial(_cumsum_kernel, tb=tb, nchunk=nchunk),
        grid=(b,), in_specs=[pl.BlockSpec((1, t, LANES), lambda i: (i, 0, 0))],
        out_specs=[pl.BlockSpec((1, nchunk, C_HEADS, tb), lambda i: (i, 0, 0, 0)),
                   pl.BlockSpec((1, t, LANES), lambda i: (i, 0, 0))],
        out_shape=[jax.ShapeDtypeStruct((b, nchunk, C_HEADS, tb), F32),
                   jax.ShapeDtypeStruct((b, t, LANES), F32)],
        compiler_params=_cparams(("parallel",)), name="logf_cumsum")(lf3d)


def _flash2_kernel(*refs, mode, tb, nk, lam_init):
    if mode == "fox":
        (q_ref, k_ref, v_ref, cum_ref, cumt_ref, o_ref,
         kb_ref, vt_ref, m_ref, l_ref, acc_ref, ck_ref) = refs
    else:
        (q_ref, k_ref, v_ref, bias_ref, lamp_ref, subln_ref, o_ref,
         kb_ref, vt_ref, m_ref, l_ref, acc_ref) = refs
    g = pl.program_id(1)
    qi = pl.program_id(2)
    lane = lax.broadcasted_iota(jnp.int32, (tb, LANES), 1)
    krow = lax.broadcasted_iota(jnp.int32, (tb, tb), 0)
    qcol = lax.broadcasted_iota(jnp.int32, (tb, tb), 1)
    tri = krow <= qcol

    @pl.when(qi == 0)
    def _():
        if mode == "fox":
            for c in range(nk):
                vt_ref[c] = v_ref[0, :, c * tb:(c + 1) * tb].astype(BF16)
        else:
            kb_ref[...] = k_ref[0].astype(BF16)
            for c in range(nk):
                vt_ref[c] = v_ref[0, c * tb:(c + 1) * tb, :].T.astype(BF16)
        if mode == "fox":
            hrow = lax.broadcasted_iota(jnp.int32, (LANES, LANES), 0)
            for a in range(2):
                onehot = jnp.where(hrow == 2 * g + a, 1.0, 0.0).astype(F32)
                ck_ref[a] = _dot_hi(cumt_ref[0], onehot)

    q = q_ref[0] * (HEAD_DIM ** -0.5)
    qall = jnp.concatenate([jnp.where(lane < HEAD_DIM, q, 0.0), jnp.where(lane >= HEAD_DIM, q, 0.0)],
                           axis=0).astype(BF16)
    if mode == "fox":
        cq = jnp.concatenate([cum_ref[0, qi, pl.ds(2 * g + a, 1), :] for a in range(2)], axis=1)
    m_ref[...] = jnp.full(m_ref.shape, NEG, F32)
    l_ref[...] = jnp.zeros_like(l_ref)
    acc_ref[...] = jnp.zeros_like(acc_ref)
    rep = tb // LANES

    def chunk(c, diagonal):
        off = pl.multiple_of(c * tb, tb)
        if mode == "fox":
            zt = _dot_nt(k_ref[0, pl.ds(off, tb), :], qall)
            ck0 = ck_ref[0, pl.ds(off, tb), :]
            ck1 = ck_ref[1, pl.ds(off, tb), :]
            zt = zt - jnp.concatenate([ck0] * rep + [ck1] * rep, axis=1)
            shift = cq
        else:
            zt = _dot_nt(kb_ref[pl.ds(off, tb), :], qall)
            bt = bias_ref[0, jnp.minimum(qi - c, 2)]
            zt = zt + jnp.concatenate([bt, bt], axis=1)
            shift = None
        if diagonal:
            zt = jnp.where(jnp.concatenate([tri, tri], axis=1), zt, NEG)
        m_new, l_new, acc_new = _softmax_cols(zt, m_ref[...], l_ref[...], acc_ref[...], vt_ref[c], shift)
        m_ref[...] = m_new
        l_ref[...] = l_new
        acc_ref[...] = acc_new

    def body(c, carry):
        chunk(c, False)
        return carry

    lax.fori_loop(0, qi, body, 0)
    chunk(qi, True)
    o = acc_ref[...] / l_ref[...]
    o0 = o[:, 0:tb].T
    o1 = o[:, tb:2 * tb].T
    if mode == "fox":
        o_ref[0] = jnp.where(lane < HEAD_DIM, o0, o1)
    else:
        lp = lamp_ref[...]
        lam = (jnp.exp(jnp.sum(lp[0:1] * lp[1:2], axis=1, keepdims=True))
               - jnp.exp(jnp.sum(lp[2:3] * lp[3:4], axis=1, keepdims=True)) + lam_init)
        o_ref[0] = _rms(o0 - lam * o1, subln_ref[...]) * (1.0 - lam_init)


def _flash2(mode, q, k, v, extra, tb, lam_init=0.0):
    b, t, width = q.shape
    ng = width // LANES
    nq = t // tb
    in_specs = [pl.BlockSpec((1, tb, LANES), lambda i, g, j: (i, j, g)),
                pl.BlockSpec((1, t, LANES), lambda i, g, j: (i, 0, g)),
                pl.BlockSpec((1, t, LANES), lambda i, g, j: (i, 0, g))]
    scratch = [pltpu.VMEM((t, LANES), BF16), pltpu.VMEM((nq, LANES, tb), BF16),
               pltpu.VMEM((1, 2 * tb), F32), pltpu.VMEM((1, 2 * tb), F32), pltpu.VMEM((LANES, 2 * tb), F32)]
    if mode == "fox":
        cum, cumt = extra
        in_specs[2] = pl.BlockSpec((1, LANES, t), lambda i, g, j: (i, g, 0))
        in_specs += [pl.BlockSpec((1, nq, C_HEADS, tb), lambda i, g, j: (i, 0, 0, 0)),
                     pl.BlockSpec((1, t, LANES), lambda i, g, j: (i, 0, 0))]
        scratch.append(pltpu.VMEM((2, t, LANES), F32))
    else:
        bias, lamp, subln = extra
        in_specs += [pl.BlockSpec((1, 3, tb, tb), lambda i, g, j: (g, 0, 0, 0)),
                     pl.BlockSpec(lamp.shape, lambda i, g, j: (0, 0)),
                     pl.BlockSpec(subln.shape, lambda i, g, j: (0, 0))]
    return pl.pallas_call(
        functools.partial(_flash2_kernel, mode=mode, tb=tb, nk=nq, lam_init=lam_init),
        grid=(b, ng, nq), in_specs=in_specs,
        out_specs=pl.BlockSpec((1, tb, LANES), lambda i, g, j: (i, j, g)),
        out_shape=jax.ShapeDtypeStruct((b, t, width), F32),
        scratch_shapes=scratch,
        compiler_params=_cparams(("arbitrary", "arbitrary", "arbitrary")),
        name="attn_" + mode)(q, k, v, *extra)


def _dsa_p_kernel(qa_ref, ka_ref, vat_in_ref, qi_ref, ki_ref, wi_ref, bias_ref, o_ref,
                  vat_ref, key_ref, qm_ref, m_ref, l_ref, acc_ref, *, n_sel, nk):
    tb = LANES
    qi = pl.program_id(1)
    krow = lax.broadcasted_iota(jnp.int32, (tb, tb), 0)
    qcol = lax.broadcasted_iota(jnp.int32, (tb, tb), 1)
    tri = krow <= qcol
    nchunks = qi + 1

    @pl.when(qi == 0)
    def _():
        for c in range(nk):
            vat_ref[c] = vat_in_ref[0, :, c * tb:(c + 1) * tb].astype(BF16)

    qidx = qi_ref[0]
    qh = jnp.concatenate([qidx[:, h * IDX_DIM:(h + 1) * IDX_DIM] for h in range(IDX_HEADS)],
                         axis=0).astype(BF16)
    wt = (wi_ref[0] * (IDX_DIM ** -0.5 * IDX_HEADS ** -0.5)).T
    wall = jnp.concatenate([wt[h:h + 1, :] for h in range(IDX_HEADS)], axis=1)

    def p1(c, carry):
        off = pl.multiple_of(c * tb, tb)
        sh = jnp.maximum(_dot_nt(ki_ref[0, pl.ds(off, tb), :], qh), 0.0) * wall
        s = sh[:, 0:tb]
        for h in range(1, IDX_HEADS):
            s = s + sh[:, h * tb:(h + 1) * tb]
        s = jnp.where(jnp.logical_or(c < qi, tri), s, -jnp.inf)
        key_ref[c] = _sort_key(s)
        return carry

    lax.fori_loop(0, nchunks, p1, 0)

    def count(pred):
        def cb(c, acc):
            return acc + jnp.where(pred(key_ref[c]), 1.0, 0.0)
        acc = lax.fori_loop(0, nchunks, cb, jnp.zeros((tb, tb), F32))
        return jnp.sum(acc, axis=0, keepdims=True)

    thr = _kth_largest(lambda trial: count(lambda kk: kk >= trial), n_sel, (1, tb))
    need = n_sel - count(lambda kk: kk > thr)
    lstrict = jnp.where(qcol < krow, 1.0, 0.0).astype(BF16)

    lane = qcol
    qa = qa_ref[0] * (HEAD_DIM ** -0.5)
    for j in range(A_HEADS):
        blk = qa[:, (j // 2) * LANES:(j // 2 + 1) * LANES]
        grp = j // A_GROUP
        if j % 2 != grp:
            blk = pltpu.roll(blk, HEAD_DIM, axis=1)
        keep = (lane < HEAD_DIM) if grp == 0 else (lane >= HEAD_DIM)
        qm_ref[j] = jnp.where(keep, blk, 0.0).astype(BF16)
    m_ref[...] = jnp.full(m_ref.shape, NEG, F32)
    l_ref[...] = jnp.zeros_like(l_ref)
    acc_ref[...] = jnp.zeros_like(acc_ref)

    qall = qm_ref[...].reshape(A_HEADS * tb, LANES)

    def p4(c, carry):
        kk = key_ref[c]
        eq = kk == thr
        eqf = jnp.where(eq, 1.0, 0.0)
        prefix = _dot(lstrict, eqf.astype(BF16)) + carry
        sel = jnp.logical_or(kk > thr, jnp.logical_and(eq, prefix < need))
        sel = jnp.logical_and(sel, jnp.logical_or(c < qi, tri))
        am = jnp.where(sel, 0.0, NEG)
        off = pl.multiple_of(c * tb, tb)
        zt = (_dot_nt(ka_ref[0, pl.ds(off, tb), :], qall) + bias_ref[jnp.minimum(qi - c, 2)]
              + jnp.concatenate([am] * A_HEADS, axis=1))
        m_new, l_new, acc_new = _softmax_cols(zt, m_ref[...], l_ref[...], acc_ref[...], vat_ref[c])
        m_ref[...] = m_new
        l_ref[...] = l_new
        acc_ref[...] = acc_new
        return carry + jnp.sum(eqf, axis=0, keepdims=True)

    lax.fori_loop(0, nchunks, p4, jnp.zeros((1, tb), F32))
    oall = acc_ref[...] / l_ref[...]
    for c2 in range(A_HEADS // 2):
        grp = (2 * c2) // A_GROUP
        lo = oall[:, (2 * c2) * tb:(2 * c2 + 1) * tb].T
        hi = oall[:, (2 * c2 + 1) * tb:(2 * c2 + 2) * tb].T
        if grp == 0:
            hi = pltpu.roll(hi, HEAD_DIM, axis=1)
        else:
            lo = pltpu.roll(lo, HEAD_DIM, axis=1)
        o_ref[0, :, c2 * LANES:(c2 + 1) * LANES] = jnp.where(lane < HEAD_DIM, lo, hi)


def _dsa_p(q_a, k_a, v_at, q_i, k_i, w_i, bias, n_sel):
    b, t, _ = q_a.shape
    tb = LANES
    nq = t // tb
    in_specs = [pl.BlockSpec((1, tb, q_a.shape[2]), lambda i, j: (i, j, 0)),
                pl.BlockSpec((1, t, LANES), lambda i, j: (i, 0, 0)),
                pl.BlockSpec((1, LANES, t), lambda i, j: (i, 0, 0)),
                pl.BlockSpec((1, tb, q_i.shape[2]), lambda i, j: (i, j, 0)),
                pl.BlockSpec((1, t, IDX_DIM), lambda i, j: (i, 0, 0)),
                pl.BlockSpec((1, tb, LANES), lambda i, j: (i, j, 0)),
                pl.BlockSpec(bias.shape, lambda i, j: (0, 0, 0))]
    return pl.pallas_call(
        functools.partial(_dsa_p_kernel, n_sel=n_sel, nk=nq),
        grid=(b, nq), in_specs=in_specs,
        out_specs=pl.BlockSpec((1, tb, q_a.shape[2]), lambda i, j: (i, j, 0)),
        out_shape=jax.ShapeDtypeStruct(q_a.shape, F32),
        scratch_shapes=[pltpu.VMEM((nq, LANES, tb), BF16),
                        pltpu.VMEM((nq, tb, tb), jnp.int32),
                        pltpu.VMEM((A_HEADS, tb, LANES), BF16),
                        pltpu.VMEM((1, A_HEADS * tb), F32), pltpu.VMEM((1, A_HEADS * tb), F32),
                        pltpu.VMEM((LANES, A_HEADS * tb), F32)],
        compiler_params=_cparams(("arbitrary", "arbitrary")), name="attn_dsa")(q_a, k_a, v_at, q_i, k_i, w_i, bias)


def _fox_s_kernel(pt_ref, q_ref, kn_ref, vn_ref, lfn_ref, *refs, ts, pp):
    kt_refs = refs[:pp]
    vt_refs = refs[pp:2 * pp]
    lft_refs = refs[2 * pp:3 * pp]
    o_ref = refs[3 * pp]
    qbd_ref, kpad_ref, vpad_ref, lfpad_ref, m_ref, l_ref, acc_ref, carry_ref, cq_ref = refs[3 * pp + 1:]
    b = pl.program_id(0)
    s = pl.program_id(1)
    nrow = ts * C_HEADS
    width = C_HEADS * HEAD_DIM
    row = lax.broadcasted_iota(jnp.int32, (PAGE, PAGE), 0)
    col = lax.broadcasted_iota(jnp.int32, (PAGE, PAGE), 1)

    @pl.when(jnp.logical_and(b == 0, s == 0))
    def _():
        kpad_ref[...] = jnp.zeros_like(kpad_ref)
        vpad_ref[...] = jnp.zeros_like(vpad_ref)

    def update(z, pv):
        m_new, l_new, acc_new = _softmax_rows(z, m_ref[...], l_ref[...], acc_ref[...], pv)
        m_ref[...] = m_new
        l_ref[...] = l_new
        acc_ref[...] = acc_new

    @pl.when(s == 0)
    def _():
        hmask = (lax.broadcasted_iota(jnp.int32, (C_HEADS, width), 1) // HEAD_DIM
                 == lax.broadcasted_iota(jnp.int32, (C_HEADS, width), 0))
        q = q_ref[0] * (HEAD_DIM ** -0.5)
        for i in range(ts):
            qbd_ref[i * C_HEADS:(i + 1) * C_HEADS, :] = jnp.where(
                hmask, jnp.broadcast_to(q[i:i + 1, :], (C_HEADS, width)), 0.0).astype(BF16)
        kpad_ref[0:ts, :] = kn_ref[0]
        vpad_ref[0:ts, :] = vn_ref[0]
        lfpad_ref[...] = jnp.zeros_like(lfpad_ref)
        lfpad_ref[0:ts, :] = lfn_ref[0]
        m_ref[...] = jnp.full(m_ref.shape, NEG, F32)
        l_ref[...] = jnp.zeros_like(l_ref)
        acc_ref[...] = jnp.zeros_like(acc_ref)
        carry_ref[...] = jnp.zeros_like(carry_ref)
        lft = lfpad_ref[...].T[0:C_HEADS, :]
        incl = jnp.where(row <= col, 1.0, 0.0).astype(F32)
        cnew = _dot_hi(lft, incl)
        for i in range(ts):
            cq_ref[i * C_HEADS:(i + 1) * C_HEADS, :] = jnp.broadcast_to(cnew[:, i:i + 1], (C_HEADS, PAGE))
        z = _dot_nt(qbd_ref[...], kpad_ref[...].astype(BF16))
        z = z + cq_ref[...] - jnp.concatenate([cnew] * ts, axis=0)
        rr = lax.broadcasted_iota(jnp.int32, (nrow, PAGE), 0)
        cc = lax.broadcasted_iota(jnp.int32, (nrow, PAGE), 1)
        z = jnp.where(cc * C_HEADS <= rr, z, NEG)
        vnew = vpad_ref[...].astype(BF16)
        update(z, lambda p: _dot(p, vnew))

    later = jnp.where(row > col, 1.0, 0.0).astype(F32)
    carry = carry_ref[...]
    sufs = []
    for k in range(pp):
        lft = lft_refs[k][...]
        sufs.append(_dot_hi(lft, later) + carry)
        carry = carry + jnp.sum(lft, axis=1, keepdims=True)
    carry_ref[...] = carry
    suf = jnp.concatenate(sufs, axis=1)
    kt = jnp.concatenate([kt_refs[k][...].astype(BF16) for k in range(pp)], axis=1)
    vt = jnp.concatenate([vt_refs[k][...].astype(BF16) for k in range(pp)], axis=1)
    z = _dot(qbd_ref[...], kt)
    z = z + jnp.concatenate([cq_ref[...]] * pp, axis=1) + jnp.concatenate([suf] * ts, axis=0)
    update(z, lambda p: _dot_nt(p, vt))

    @pl.when(s == pl.num_programs(1) - 1)
    def _():
        hmask = (lax.broadcasted_iota(jnp.int32, (C_HEADS, width), 1) // HEAD_DIM
                 == lax.broadcasted_iota(jnp.int32, (C_HEADS, width), 0))
        o = acc_ref[...] / l_ref[...]
        for i in range(ts):
            blk = jnp.where(hmask, o[i * C_HEADS:(i + 1) * C_HEADS, :], 0.0)
            o_ref[0, i:i + 1, :] = jnp.sum(blk, axis=0, keepdims=True)


def _fox_s(q, kn, vn, lfn, cache_kt, cache_vt, cache_lft, layer, page_table, pp):
    b, ts, width = q.shape
    npg = page_table.shape[1]
    nrow = ts * C_HEADS

    def page_idx(k):
        return lambda i, s, pt: (layer, pt[i, npg - 1 - (s * pp + k)], 0, 0)

    def seq_idx(i, s, pt):
        return (i, 0, 0)

    in_specs = [pl.BlockSpec((1, ts, width), seq_idx), pl.BlockSpec((1, ts, width), seq_idx),
                pl.BlockSpec((1, ts, width), seq_idx), pl.BlockSpec((1, ts, LANES), seq_idx)]
    in_specs += [pl.BlockSpec((None, None, width, PAGE), page_idx(k)) for k in range(pp)]
    in_specs += [pl.BlockSpec((None, None, width, PAGE), page_idx(k)) for k in range(pp)]
    in_specs += [pl.BlockSpec((None, None, C_HEADS, PAGE), page_idx(k)) for k in range(pp)]
    grid_spec = pltpu.PrefetchScalarGridSpec(
        num_scalar_prefetch=1, grid=(b, npg // pp), in_specs=in_specs,
        out_specs=pl.BlockSpec((1, ts, width), seq_idx),
        scratch_shapes=[pltpu.VMEM((nrow, width), BF16), pltpu.VMEM((PAGE, width), F32),
                        pltpu.VMEM((PAGE, width), F32), pltpu.VMEM((PAGE, LANES), F32),
                        pltpu.VMEM((nrow, 1), F32), pltpu.VMEM((nrow, 1), F32),
                        pltpu.VMEM((nrow, width), F32), pltpu.VMEM((C_HEADS, PAGE), F32),
                        pltpu.VMEM((nrow, PAGE), F32)])
    return pl.pallas_call(
        functools.partial(_fox_s_kernel, ts=ts, pp=pp), grid_spec=grid_spec,
        out_shape=jax.ShapeDtypeStruct((b, ts, width), F32),
        compiler_params=_cparams(("arbitrary", "arbitrary")), name="attn_fox_sample")(
            page_table, q, kn, vn, lfn, *([cache_kt] * pp), *([cache_vt] * pp), *([cache_lft] * pp))


def _diff_s_kernel(pt_ref, q_ref, kn_ref, vn_ref, *refs, ts, pp, npg, lam_init):
    k_refs = refs[:pp]
    v_refs = refs[pp:2 * pp]
    (bias_ref, biasn_ref, lamp_ref, subln_ref, o_ref,
     qx_ref, kpad_ref, vpad_ref, m_ref, l_ref, acc_ref) = refs[2 * pp:]
    b = pl.program_id(0)
    s = pl.program_id(1)
    hrows = B_HEADS * ts
    lane = lax.broadcasted_iota(jnp.int32, (ts, LANES), 1)

    @pl.when(jnp.logical_and(b == 0, s == 0))
    def _():
        kpad_ref[...] = jnp.zeros_like(kpad_ref)
        vpad_ref[...] = jnp.zeros_like(vpad_ref)

    @pl.when(s == 0)
    def _():
        q = q_ref[0] * (HEAD_DIM ** -0.5)
        for n in range(B_HEADS):
            blk = q[:, n * LANES:(n + 1) * LANES]
            qx_ref[n * ts:(n + 1) * ts, :] = jnp.where(lane < HEAD_DIM, blk, 0.0)
            qx_ref[hrows + n * ts:hrows + (n + 1) * ts, :] = jnp.where(lane >= HEAD_DIM, blk, 0.0)
        m_ref[...] = jnp.full(m_ref.shape, NEG, F32)
        l_ref[...] = jnp.zeros_like(l_ref)
        acc_ref[...] = jnp.zeros_like(acc_ref)

    def step(kall, vall, bias):
        z = _dot_nt(qx_ref[...].astype(BF16), kall) + bias
        m_new, l_new, acc_new = _softmax_rows(z, m_ref[...], l_ref[...], acc_ref[...], lambda p: _dot(p, vall))
        m_ref[...] = m_new
        l_ref[...] = l_new
        acc_ref[...] = acc_new

    step(jnp.concatenate([k_refs[k][...].astype(BF16) for k in range(pp)], axis=0),
         jnp.concatenate([v_refs[k][...].astype(BF16) for k in range(pp)], axis=0), bias_ref[s])

    @pl.when(s == pl.num_programs(1) - 1)
    def _():
        kpad_ref[0:ts * B_HEADS, :] = kn_ref[0]
        vpad_ref[0:ts * B_HEADS, :] = vn_ref[0]
        step(kpad_ref[...].astype(BF16), vpad_ref[...].astype(BF16), biasn_ref[...])
        lp = lamp_ref[...]
        lam = (jnp.exp(jnp.sum(lp[0:1] * lp[1:2], axis=1, keepdims=True))
               - jnp.exp(jnp.sum(lp[2:3] * lp[3:4], axis=1, keepdims=True)) + lam_init)
        o = acc_ref[...] / l_ref[...]
        od = o[0:hrows] - lam * o[hrows:2 * hrows]
        for n in range(B_HEADS):
            o_ref[0, :, n * B_VDIM:(n + 1) * B_VDIM] = (
                _rms(od[n * ts:(n + 1) * ts], subln_ref[...]) * (1.0 - lam_init))


def _diff_s(q, kn16, vn16, cache_k, cache_v, layer, page_table, bias, bias_new, lamp, subln, lam_init, pp):
    b, ts, width = q.shape
    npg = page_table.shape[1]
    nrow = 2 * B_HEADS * ts
    krows = PAGE * B_HEADS

    def page_idx(k):
        return lambda i, s, pt: (layer, pt[i, s * pp + k], 0, 0)

    def seq_idx(i, s, pt):
        return (i, 0, 0)

    in_specs = [pl.BlockSpec((1, ts, width), seq_idx), pl.BlockSpec((1,) + kn16.shape[1:], seq_idx),
                pl.BlockSpec((1,) + vn16.shape[1:], seq_idx)]
    in_specs += [pl.BlockSpec((None, None, krows, LANES), page_idx(k)) for k in range(pp)] * 2
    in_specs += [pl.BlockSpec(bias.shape, lambda i, s, pt: (0, 0, 0)),
                 pl.BlockSpec(bias_new.shape, lambda i, s, pt: (0, 0)),
                 pl.BlockSpec(lamp.shape, lambda i, s, pt: (0, 0)),
                 pl.BlockSpec(subln.shape, lambda i, s, pt: (0, 0))]
    grid_spec = pltpu.PrefetchScalarGridSpec(
        num_scalar_prefetch=1, grid=(b, npg // pp), in_specs=in_specs,
        out_specs=pl.BlockSpec((1, ts, width), seq_idx),
        scratch_shapes=[pltpu.VMEM((nrow, LANES), F32), pltpu.VMEM((krows, LANES), F32),
                        pltpu.VMEM((krows, LANES), F32), pltpu.VMEM((nrow, 1), F32),
                        pltpu.VMEM((nrow, 1), F32), pltpu.VMEM((nrow, LANES), F32)])
    return pl.pallas_call(
        functools.partial(_diff_s_kernel, ts=ts, pp=pp, npg=npg, lam_init=lam_init), grid_spec=grid_spec,
        out_shape=jax.ShapeDtypeStruct((b, ts, width), F32),
        compiler_params=_cparams(("arbitrary", "arbitrary")), name="attn_diff_sample")(
            page_table, q, kn16, vn16, *([cache_k] * pp), *([cache_v] * pp), bias, bias_new, lamp, subln)


def _dsa_s_kernel(pt_ref, qa_ref, qi_ref, wi_ref, kn_ref, vn_ref, kin_ref, *refs, ts, pp, n_sel, nchunk):
    kt_refs = refs[:pp]
    vt_refs = refs[pp:2 * pp]
    kit_refs = refs[2 * pp:3 * pp]
    bias_ref, o_ref, kst_ref, vst_ref, pad_ref, key_ref, am_ref = refs[3 * pp:]
    b = pl.program_id(0)
    s = pl.program_id(1)
    last = nchunk - 1
    rpad = SUBLANES

    @pl.when(jnp.logical_and(b == 0, s == 0))
    def _():
        pad_ref[...] = jnp.zeros_like(pad_ref)

    def index_scores(kit):
        s32 = jnp.maximum(_dot(qi_ref[0].astype(BF16), kit.astype(BF16)), 0.0) * wi_ref[0]
        sc = s32[0:rpad]
        for h in range(1, IDX_HEADS):
            sc = sc + s32[h * rpad:(h + 1) * rpad]
        return sc

    for k in range(pp):
        chunk = s * pp + k
        kst_ref[chunk] = kt_refs[k][...]
        vst_ref[chunk] = vt_refs[k][...]
        key_ref[chunk] = _sort_key(index_scores(kit_refs[k][...]))

    @pl.when(s == pl.num_programs(1) - 1)
    def _():
        pad_ref[0, 0:ts, :] = kn_ref[0]
        pad_ref[1, 0:ts, :] = vn_ref[0]
        pad_ref[2, 0:ts, 0:IDX_DIM] = kin_ref[0]
        kst_ref[last] = pad_ref[0].T
        vst_ref[last] = pad_ref[1].T
        rr = lax.broadcasted_iota(jnp.int32, (rpad, PAGE), 0)
        cc = lax.broadcasted_iota(jnp.int32, (rpad, PAGE), 1)
        causal = jnp.logical_and(cc <= rr, cc < ts)
        key_ref[last] = _sort_key(jnp.where(causal, index_scores(pad_ref[2].T[0:IDX_DIM, :]), -jnp.inf))

        def count(pred):
            acc = jnp.zeros((rpad, PAGE), F32)
            for c in range(nchunk):
                acc = acc + jnp.where(pred(key_ref[c]), 1.0, 0.0)
            return jnp.sum(acc, axis=1, keepdims=True)

        thr = _kth_largest(lambda trial: count(lambda kk: kk >= trial), n_sel, (rpad, 1))
        need = n_sel - count(lambda kk: kk > thr)
        r2 = lax.broadcasted_iota(jnp.int32, (PAGE, PAGE), 0)
        c2 = lax.broadcasted_iota(jnp.int32, (PAGE, PAGE), 1)
        ustrict = jnp.where(r2 < c2, 1.0, 0.0).astype(BF16)
        carry = jnp.zeros((rpad, 1), F32)
        for c in range(nchunk):
            kk = key_ref[c]
            eq = kk == thr
            eqf = jnp.where(eq, 1.0, 0.0)
            prefix = _dot(eqf.astype(BF16), ustrict) + carry
            sel = jnp.logical_or(kk > thr, jnp.logical_and(eq, prefix < need))
            if c == last:
                sel = jnp.logical_and(sel, causal)
            am_ref[c] = jnp.where(sel, 0.0, NEG)
            carry = carry + jnp.sum(eqf, axis=1, keepdims=True)

        nrow = A_HEADS * rpad
        qbd = qa_ref[0].astype(BF16)

        zs = [_dot(qbd, kst_ref[c].astype(BF16)) + bias_ref[c] + jnp.concatenate([am_ref[c]] * A_HEADS, axis=0)
              for c in range(nchunk)]
        zmax = zs[0]
        for c in range(1, nchunk):
            zmax = jnp.maximum(zmax, zs[c])
        m_row = jnp.max(zmax, axis=1, keepdims=True)
        psum = jnp.zeros((nrow, LANES), F32)
        acc = jnp.zeros((nrow, LANES), F32)
        for c in range(nchunk):
            p = jnp.exp(zs[c] - m_row)
            psum = psum + p
            acc = acc + _dot_nt(p.astype(BF16), vst_ref[c].astype(BF16))
        o = acc / jnp.sum(psum, axis=1, keepdims=True)
        lane = lax.broadcasted_iota(jnp.int32, (rpad, LANES), 1)
        for blk in range(A_HEADS // 2):
            grp = (2 * blk) // A_GROUP
            lo = o[(2 * blk) * rpad:(2 * blk + 1) * rpad]
            hi = o[(2 * blk + 1) * rpad:(2 * blk + 2) * rpad]
            if grp == 0:
                hi = pltpu.roll(hi, HEAD_DIM, axis=1)
            else:
                lo = pltpu.roll(lo, HEAD_DIM, axis=1)
            res = jnp.where(lane < HEAD_DIM, lo, hi)
            o_ref[0, :, blk * LANES:(blk + 1) * LANES] = res[0:ts]


def _dsa_s(qa_bd, qi32, w32, kn, vn, kin, cache_kt, cache_vt, cache_kit, layer, page_table, bias, n_sel, ts, pp):
    b = qa_bd.shape[0]
    npg = page_table.shape[1]
    nchunk = npg + 1
    width = A_HEADS * HEAD_DIM

    def page_idx(k):
        return lambda i, s, pt: (layer, pt[i, s * pp + k], 0, 0)

    def seq_idx(i, s, pt):
        return (i, 0, 0)

    in_specs = [pl.BlockSpec((1,) + qa_bd.shape[1:], seq_idx), pl.BlockSpec((1,) + qi32.shape[1:], seq_idx),
                pl.BlockSpec((1,) + w32.shape[1:], seq_idx),
                pl.BlockSpec((1, ts, LANES), seq_idx), pl.BlockSpec((1, ts, LANES), seq_idx),
                pl.BlockSpec((1, ts, IDX_DIM), seq_idx)]
    in_specs += [pl.BlockSpec((None, None, LANES, PAGE), page_idx(k)) for k in range(pp)] * 2
    in_specs += [pl.BlockSpec((None, None, IDX_DIM, PAGE), page_idx(k)) for k in range(pp)]
    in_specs += [pl.BlockSpec(bias.shape, lambda i, s, pt: (0, 0, 0))]
    grid_spec = pltpu.PrefetchScalarGridSpec(
        num_scalar_prefetch=1, grid=(b, npg // pp), in_specs=in_specs,
        out_specs=pl.BlockSpec((1, ts, width), seq_idx),
        scratch_shapes=[pltpu.VMEM((nchunk, LANES, PAGE), F32), pltpu.VMEM((nchunk, LANES, PAGE), F32),
                        pltpu.VMEM((3, PAGE, LANES), F32),
                        pltpu.VMEM((nchunk, SUBLANES, PAGE), jnp.int32), pltpu.VMEM((nchunk, SUBLANES, PAGE), F32)])
    return pl.pallas_call(
        functools.partial(_dsa_s_kernel, ts=ts, pp=pp, n_sel=n_sel, nchunk=nchunk), grid_spec=grid_spec,
        out_shape=jax.ShapeDtypeStruct((b, ts, width), F32),
        compiler_params=_cparams(("arbitrary", "arbitrary")), name="attn_dsa_sample")(
            page_table, qa_bd, qi32, w32, kn, vn, kin,
            *([cache_kt] * pp), *([cache_vt] * pp), *([cache_kit] * pp), bias)


def _t5_bucket(rel):
    n = jnp.maximum(rel, 0)
    exact = N_BUCKETS // 2
    nf = jnp.maximum(n, 1).astype(F32)
    log_b = exact + (jnp.log(nf / exact) / math.log(MAX_DISTANCE / exact) * (N_BUCKETS - exact)).astype(jnp.int32)
    return jnp.where(n < exact, n, jnp.minimum(log_b, N_BUCKETS - 1))


def _bias_of(tab, rel):
    bucket = _t5_bucket(rel)[None]
    tab = tab.astype(F32)
    out = jnp.broadcast_to(tab[0].reshape((-1,) + (1,) * rel.ndim), (tab.shape[1],) + rel.shape)
    for bkt in range(1, N_BUCKETS):
        out = jnp.where(bucket == bkt, tab[bkt].reshape((-1,) + (1,) * rel.ndim), out)
    return out


def _bias_tiles_t(tab, tb):
    j = jnp.arange(tb, dtype=jnp.int32)[:, None]
    i = jnp.arange(tb, dtype=jnp.int32)[None, :]
    return _bias_of(tab, jnp.stack([d * tb + i - j for d in range(3)]))


def _split(w, sizes):
    offs = np.cumsum((0,) + tuple(sizes))
    return [w[:, int(offs[i]):int(offs[i + 1])] for i in range(len(sizes))]


def _pad_cols(w, n):
    return jnp.pad(w, ((0, 0), (0, n - w.shape[1])))


def _row_tile(m, want):
    t = min(m, want)
    while m % t:
        t //= 2
    return t


def _pages_per_step(npg, want):
    pp = min(npg, want)
    while npg % pp:
        pp -= 1
    return pp


def kernel(x_prompt, x_sample, cache_a_k, cache_a_v, cache_a_kidx, cache_b_k, cache_b_v, cache_c_k, cache_c_v,
           cache_c_logf, state_conv, page_table, rel_bias_table, w_in_even, w_out_even, lambda_q1, lambda_k1,
           lambda_q2, lambda_k2, diff_subln, w_in_odd, b_forget, w_out_odd, norm_mix_pre, norm_mix_post,
           norm_ffn_pre, norm_ffn_post, w_gate_up, w_conv, b_conv, w_down):
    bp, tp, d = x_prompt.shape
    bs, ts, _ = x_sample.shape
    depth = w_gate_up.shape[0]
    n_pool = cache_a_k.shape[1]
    npg = page_table.shape[1]
    past = npg * PAGE
    mp, ms = bp * tp, bs * ts
    tb_dsa = LANES
    tb_attn = min(512, tp)
    nsel_p = min(TOPK_MAX, tp // 4)
    nsel_s = min(TOPK_MAX, (past + ts) // 4)
    tm_p = _row_tile(mp, 512)
    tm_s = _row_tile(ms, 512)
    tf = D_FF // 2
    dff = D_FF
    pp = _pages_per_step(npg, 8)

    xp = x_prompt.reshape(mp, d)
    xs = x_sample.reshape(ms, d)
    tab_a = rel_bias_table[:, :A_HEADS]
    tab_b = rel_bias_table[:, A_HEADS:]
    bias_a_p = _bias_tiles_t(tab_a, tb_dsa)
    bias_a_p = jnp.transpose(bias_a_p, (1, 2, 0, 3)).reshape(3, tb_dsa, A_HEADS * tb_dsa)
    bias_b_p = _bias_tiles_t(tab_b, tb_attn)
    rel_s = (past + jnp.arange(ts, dtype=jnp.int32))[:, None] - jnp.arange(past + PAGE, dtype=jnp.int32)[None, :]
    rows_a = _bias_of(tab_a, rel_s).reshape(A_HEADS, ts, npg + 1, PAGE)
    rows_b = _bias_of(tab_b, rel_s).reshape(B_HEADS, ts, npg + 1, PAGE)
    bias_a_s = jnp.pad(jnp.moveaxis(rows_a, 2, 0), ((0, 0), (0, 0), (0, SUBLANES - ts), (0, 0)))
    bias_a_s = bias_a_s.reshape(npg + 1, A_HEADS * SUBLANES, PAGE)
    kq_ok = rel_s.reshape(ts, npg + 1, PAGE) >= 0
    same = jnp.eye(B_HEADS, dtype=bool)
    bias_b_s = jnp.where(same[:, None, None, None, :] & kq_ok[None, :, :, :, None],
                         rows_b[..., None], NEG)
    bias_b_s = jnp.moveaxis(bias_b_s, 2, 0).reshape(npg + 1, B_HEADS * ts, PAGE * B_HEADS)
    bias_b_s = jnp.concatenate([bias_b_s, bias_b_s], axis=1)
    bias_b_new = bias_b_s[npg]
    bias_b_s = bias_b_s[:npg].reshape(npg // pp, pp, 2 * B_HEADS * ts, PAGE * B_HEADS)
    bias_b_s = jnp.swapaxes(bias_b_s, 1, 2).reshape(npg // pp, 2 * B_HEADS * ts, pp * PAGE * B_HEADS)

    ca_kt = jnp.transpose(cache_a_k, (0, 1, 3, 4, 2)).reshape(-1, n_pool, LANES, PAGE)
    ca_vt = jnp.transpose(cache_a_v, (0, 1, 3, 4, 2)).reshape(-1, n_pool, LANES, PAGE)
    ca_kit = jnp.transpose(cache_a_kidx, (0, 1, 3, 2))
    cb_k = cache_b_k.reshape(-1, n_pool, PAGE * B_HEADS, 2 * HEAD_DIM)
    cb_v = cache_b_v.reshape(-1, n_pool, PAGE * B_HEADS, B_VDIM)
    cc_kt = jnp.transpose(cache_c_k, (0, 1, 3, 4, 2)).reshape(-1, n_pool, C_HEADS * HEAD_DIM, PAGE)
    cc_vt = jnp.transpose(cache_c_v, (0, 1, 3, 4, 2)).reshape(-1, n_pool, C_HEADS * HEAD_DIM, PAGE)
    cc_lft = jnp.transpose(cache_c_logf, (0, 1, 3, 2))

    even_rows_p, even_rows_s, odd_rows_p, odd_rows_s, conv_p, conv_s = [], [], [], [], [], []
    for l in range(depth):
        if l % 2 == 0:
            e = l // 2
            lam_init = 0.8 - 0.6 * math.exp(-0.3 * l)
            ws = _split(w_in_even[e], EVEN_SPLIT)
            ws[5] = _pad_cols(ws[5], LANES)
            ws = [w.astype(BF16) for w in ws]
            wo = w_out_even[e].astype(BF16)
            wo_parts = [wo[:A_HEADS * HEAD_DIM], wo[A_HEADS * HEAD_DIM:]]
            lamp = jnp.stack([lambda_q1[e], lambda_k1[e], lambda_q2[e], lambda_k2[e]]).astype(F32)
            subln = diff_subln[e].reshape(1, B_VDIM)

            wq_a, wk_a, wv_a, wq_i, wk_i, ww_i, wq_b, wk_b, wv_b = ws
            items = [(wq_a, "f32"), (wk_a, "bf16"), (wk_a.T, "col"), (wv_a.T, "col"), (wq_i, "f32"),
                     (wk_i, "bf16"), (wk_i.T, "col"), (ww_i, "f32"), (wq_b, "f32"), (wk_b, "f32"), (wv_b, "f32")]
            q_a, k_a16, k_at, v_at, q_i, k_i16, k_it, w_i, q_b, k_b, v_b = _rms_proj(
                xp, norm_mix_pre[l], items, tm_p, seq=(bp, tp))
            r3 = lambda a: a.reshape(bp, tp, a.shape[1])
            o_a = _dsa_p(r3(q_a), r3(k_a16), v_at, r3(q_i), r3(k_i16), r3(w_i), bias_a_p, nsel_p)
            o_b = _flash2("diff", r3(q_b), r3(k_b), r3(v_b), (bias_b_p, lamp, subln), tb_attn, lam_init)
            xp = _proj_post(xp, norm_mix_post[l], [o_a.reshape(mp, -1), o_b.reshape(mp, -1)], wo_parts, tm_p)
            kv_rows = lambda a: jnp.transpose(a.reshape(bp, A_KV_HEADS, HEAD_DIM, tp), (0, 3, 1, 2))
            even_rows_p.append((kv_rows(k_at), kv_rows(v_at), jnp.transpose(k_it, (0, 2, 1)),
                                k_b.reshape(bp, tp, B_HEADS, 2 * HEAD_DIM), v_b.reshape(bp, tp, B_HEADS, B_VDIM)))

            q_a, k_a, v_a, q_i, k_i, w_i, q_b, k_b, v_b = _rms_proj(
                xs, norm_mix_pre[l], [(w, "f32") for w in ws], tm_s)
            s3 = lambda a: a.reshape(bs, ts, a.shape[1])
            qa4 = (q_a * (HEAD_DIM ** -0.5)).reshape(bs, ts, A_HEADS, HEAD_DIM)
            qa4 = jnp.pad(jnp.moveaxis(qa4, 1, 2), ((0, 0), (0, 0), (0, SUBLANES - ts), (0, 0)))
            zeros = jnp.zeros_like(qa4)
            qa_bd = jnp.concatenate([jnp.concatenate([qa4[:, :A_GROUP], zeros[:, :A_GROUP]], axis=-1),
                                     jnp.concatenate([zeros[:, A_GROUP:], qa4[:, A_GROUP:]], axis=-1)], axis=1)
            qa_bd = qa_bd.reshape(bs, A_HEADS * SUBLANES, LANES)
            qi4 = jnp.moveaxis(q_i.reshape(bs, ts, IDX_HEADS, IDX_DIM), 1, 2)
            qi32 = jnp.pad(qi4, ((0, 0), (0, 0), (0, SUBLANES - ts), (0, 0))).reshape(bs, IDX_HEADS * SUBLANES, IDX_DIM)
            w4 = jnp.moveaxis(w_i[:, :IDX_HEADS].reshape(bs, ts, IDX_HEADS), 1, 2) * (IDX_DIM ** -0.5 * IDX_HEADS ** -0.5)
            w32 = jnp.pad(w4, ((0, 0), (0, 0), (0, SUBLANES - ts))).reshape(bs, IDX_HEADS * SUBLANES, 1)
            w32 = jnp.broadcast_to(w32, (bs, IDX_HEADS * SUBLANES, PAGE))
            o_a = _dsa_s(qa_bd, qi32, w32, s3(k_a), s3(v_a), s3(k_i), ca_kt, ca_vt, ca_kit,
                         e, page_table, bias_a_s, nsel_s, ts, pp)
            o_b = _diff_s(s3(q_b), k_b.reshape(bs, ts * B_HEADS, 2 * HEAD_DIM), v_b.reshape(bs, ts * B_HEADS, B_VDIM),
                          cb_k, cb_v, e, page_table, bias_b_s, bias_b_new, lamp, subln, lam_init, pp)
            xs = _proj_post(xs, norm_mix_post[l], [o_a.reshape(ms, -1), o_b.reshape(ms, -1)], wo_parts, tm_s)
            even_rows_s.append((k_a.reshape(bs, ts, A_KV_HEADS, HEAD_DIM), v_a.reshape(bs, ts, A_KV_HEADS, HEAD_DIM),
                                k_i.reshape(bs, ts, IDX_DIM), k_b.reshape(bs, ts, B_HEADS, 2 * HEAD_DIM),
                                v_b.reshape(bs, ts, B_HEADS, B_VDIM)))
        else:
            o = l // 2
            ws = _split(w_in_odd[o], ODD_SPLIT)
            ws[3] = _pad_cols(ws[3], LANES)
            ws = [w.astype(BF16) for w in ws]
            bfp = _pad_cols(b_forget[o].reshape(1, C_HEADS), LANES)
            wo = [w_out_odd[o].astype(BF16)]

            wq, wk, wv, wf = ws
            items = [(wq, "f32"), (wk, "bf16"), (wk.T, "col"), (wv.T, "col"), (wf, "f32")]
            q, k16, k_t, v_t, lf = _rms_proj(xp, norm_mix_pre[l], items, tm_p, logsig_bias=bfp, seq=(bp, tp))
            r3 = lambda a: a.reshape(bp, tp, a.shape[1])
            cum, cumt = _cumsum(r3(lf), tb_attn)
            o_c = _flash2("fox", r3(q), r3(k16), v_t, (cum, cumt), tb_attn)
            xp = _proj_post(xp, norm_mix_post[l], [o_c.reshape(mp, -1)], wo, tm_p)
            kv_rows = lambda a: jnp.transpose(a.reshape(bp, C_HEADS, HEAD_DIM, tp), (0, 3, 1, 2))
            odd_rows_p.append((kv_rows(k_t), kv_rows(v_t), lf[:, :C_HEADS].reshape(bp, tp, C_HEADS)))

            q, k, v, lf = _rms_proj(xs, norm_mix_pre[l], [(w, "f32") for w in ws], tm_s, logsig_bias=bfp)
            s3 = lambda a: a.reshape(bs, ts, a.shape[1])
            o_c = _fox_s(s3(q), s3(k), s3(v), s3(lf), cc_kt, cc_vt, cc_lft, o, page_table, pp)
            xs = _proj_post(xs, norm_mix_post[l], [o_c.reshape(ms, -1)], wo, tm_s)
            odd_rows_s.append((k.reshape(bs, ts, C_HEADS, HEAD_DIM), v.reshape(bs, ts, C_HEADS, HEAD_DIM),
                               lf[:, :C_HEADS].reshape(bs, ts, C_HEADS)))

        wgu = w_gate_up[l].astype(BF16)
        wg, wu = wgu[:, :dff], wgu[:, dff:]
        wd = w_down[l].astype(BF16)
        tiles = tp // tm_p
        xp, tail = _ffn(xp, norm_ffn_pre[l], norm_ffn_post[l], wg, wu, w_conv[l], b_conv[l], wd,
                        tm_p, tf, tiles)
        conv_p.append(tail.reshape(bp, tiles, 8, dff)[:, tiles - 1, 8 - (CONV_W - 1):, :])
        st = state_conv[l]
        prev1 = jnp.concatenate([st[:, 1:2], jnp.zeros((bs, ts - 1, dff), F32)], axis=1).reshape(ms, dff)
        prev2 = jnp.concatenate([st, jnp.zeros((bs, ts - 2, dff), F32)], axis=1).reshape(ms, dff)
        xs, gfull = _ffn(xs, norm_ffn_pre[l], norm_ffn_post[l], wg, wu, w_conv[l], b_conv[l], wd,
                         tm_s, tf, ts, prev=(prev1, prev2))
        conv_s.append(gfull.reshape(bs, ts, dff)[:, ts - (CONV_W - 1):, :])

    def stack(rows, i):
        return jnp.stack([r[i] for r in rows])

    ev_p = [stack(even_rows_p, i) for i in range(5)]
    ev_s = [stack(even_rows_s, i) for i in range(5)]
    od_p = [stack(odd_rows_p, i) for i in range(3)]
    od_s = [stack(odd_rows_s, i) for i in range(3)]
    return (xp.reshape(bp, tp, d), xs.reshape(bs, ts, d),
            ev_p[0], ev_s[0], ev_p[1], ev_s[1], ev_p[2], ev_s[2], ev_p[3], ev_s[3], ev_p[4], ev_s[4],
            od_p[0], od_s[0], od_p[1], od_s[1], od_p[2], od_s[2],
            jnp.stack(conv_p), jnp.stack(conv_s))
```

```python
import functools
import math

import jax
import jax.numpy as jnp
import numpy as np
from jax import lax
from jax.experimental import pallas as pl
from jax.experimental.pallas import tpu as pltpu

F32 = jnp.float32
BF16 = jnp.bfloat16
HIGHEST = lax.Precision.HIGHEST

D_MODEL = 1024
HEAD_DIM = 64
A_HEADS = 8
A_KV_HEADS = 2
A_GROUP = A_HEADS // A_KV_HEADS
IDX_HEADS = 4
IDX_DIM = 64
TOPK_MAX = 256
B_HEADS = 4
B_VDIM = 128
C_HEADS = 16
N_BUCKETS = 32
MAX_DISTANCE = 128
D_FF = 2816
CONV_W = 3
EPS = 1e-6
PAGE = 128

LANES = 128
SUBLANES = 8
NEG = -1e30
INT_MIN = -2147483648
NEG_INF_CODE = 0x007FFFFF
VMEM_LIMIT = 48 * 1024 * 1024

EVEN_SPLIT = (512, 128, 128, 256, 64, 4, 512, 512, 512)
ODD_SPLIT = (1024, 1024, 1024, 16)

NT_DIMS = (((1,), (1,)), ((), ()))


def _cparams(sem):
    return pltpu.CompilerParams(dimension_semantics=sem, vmem_limit_bytes=VMEM_LIMIT)


def _rms(x, g):
    return x * lax.rsqrt(jnp.mean(x * x, axis=-1, keepdims=True) + EPS) * g


def _log_sigmoid(x):
    return -(jnp.maximum(-x, 0.0) + jnp.log1p(jnp.exp(-jnp.abs(x))))


def _gelu_tanh(x):
    c = math.sqrt(2.0 / math.pi)
    return x * (0.5 * (1.0 + jnp.tanh(c * (x + 0.044715 * (x * x * x)))))


def _dot(a, b):
    return jnp.dot(a, b, preferred_element_type=F32)


def _dot_nt(a, b):
    return lax.dot_general(a, b, NT_DIMS, preferred_element_type=F32)


def _dot_hi(a, b):
    return jnp.dot(a, b, preferred_element_type=F32, precision=HIGHEST)


def _softmax_rows(z, m_old, l_old, acc_old, pv):
    m_new = jnp.maximum(m_old, jnp.max(z, axis=1, keepdims=True))
    alpha = jnp.exp(m_old - m_new)
    p = jnp.exp(z - m_new)
    l_new = alpha * l_old + jnp.sum(p, axis=1, keepdims=True)
    acc_new = alpha * acc_old + pv(p.astype(BF16))
    return m_new, l_new, acc_new


def _softmax_cols(zt, m_old, l_old, acc_old, vt, shift=None):
    m_tile = jnp.max(zt, axis=0, keepdims=True)
    if shift is None:
        m_new = jnp.maximum(m_old, m_tile)
        p = jnp.exp(zt - m_new)
    else:
        m_new = jnp.maximum(m_old, m_tile + shift)
        p = jnp.exp(zt - (m_new - shift))
    alpha = jnp.exp(m_old - m_new)
    l_new = alpha * l_old + jnp.sum(p, axis=0, keepdims=True)
    acc_new = alpha * acc_old + _dot(vt, p.astype(BF16))
    return m_new, l_new, acc_new


def _ordered_value(code):
    key = code ^ jnp.int32(INT_MIN)
    return pltpu.bitcast(jnp.where(key < 0, key ^ jnp.int32(0x7FFFFFFF), key), F32)


def _kth_largest(count_ge, n_sel, shape):
    def below_neg_inf(code):
        return jnp.logical_and(code >= 0, code < jnp.int32(NEG_INF_CODE))

    def step(it, code):
        trial = code | lax.shift_left(jnp.int32(1), jnp.int32(31) - it)
        ok = jnp.logical_or(below_neg_inf(trial), count_ge(_ordered_value(trial)) >= n_sel)
        return jnp.where(ok, trial, code)

    code = lax.fori_loop(0, 32, step, jnp.zeros(shape, jnp.int32))
    return _ordered_value(jnp.where(below_neg_inf(code), jnp.int32(NEG_INF_CODE), code))


def _rms_proj_kernel(x_ref, g_ref, *refs, kinds, logsig_last):
    n_w = len(kinds)
    w_refs = refs[:n_w]
    if logsig_last:
        bias_ref = refs[n_w]
        o_refs = refs[n_w + 1:]
    else:
        o_refs = refs[n_w:]
    h = _rms(x_ref[...], g_ref[...]).astype(BF16)
    for idx, kind in enumerate(kinds):
        if kind == "col":
            o_refs[idx][0] = _dot_nt(w_refs[idx][...], h)
            continue
        y = _dot(h, w_refs[idx][...])
        if logsig_last and idx == n_w - 1:
            y = _log_sigmoid(y + bias_ref[...])
        o_refs[idx][...] = y.astype(o_refs[idx].dtype)


def _rms_proj(x2d, g, items, tm, logsig_bias=None, seq=None):
    m, d = x2d.shape
    kinds = tuple(kind for _, kind in items)
    in_specs = [pl.BlockSpec((tm, d), lambda i: (i, 0)), pl.BlockSpec((1, d), lambda i: (0, 0))]
    in_specs += [pl.BlockSpec(w.shape, lambda i: (0, 0)) for w, _ in items]
    args = [x2d, g.reshape(1, d)] + [w for w, _ in items]
    if logsig_bias is not None:
        in_specs.append(pl.BlockSpec(logsig_bias.shape, lambda i: (0, 0)))
        args.append(logsig_bias)
    out_shape, out_specs = [], []
    for w, kind in items:
        if kind == "col":
            nb, t = seq
            tiles = t // tm
            out_shape.append(jax.ShapeDtypeStruct((nb, w.shape[0], t), F32))
            out_specs.append(pl.BlockSpec((1, w.shape[0], tm), lambda i, tiles=tiles: (i // tiles, 0, i % tiles)))
        else:
            out_shape.append(jax.ShapeDtypeStruct((m, w.shape[1]), F32 if kind == "f32" else BF16))
            out_specs.append(pl.BlockSpec((tm, w.shape[1]), lambda i: (i, 0)))
    return pl.pallas_call(
        functools.partial(_rms_proj_kernel, kinds=kinds, logsig_last=logsig_bias is not None),
        grid=(m // tm,), in_specs=in_specs, out_specs=out_specs, out_shape=out_shape,
        compiler_params=_cparams(("parallel",)), name="rms_proj")(*args)


def _proj_post_kernel(x_ref, g_ref, *refs, n_in):
    o_refs = refs[:n_in]
    w_refs = refs[n_in:2 * n_in]
    out_ref = refs[2 * n_in]
    acc = None
    for o, w in zip(o_refs, w_refs):
        t = _dot(o[...].astype(BF16), w[...])
        acc = t if acc is None else acc + t
    out_ref[...] = x_ref[...] + _rms(acc, g_ref[...])


def _proj_post(x2d, g, o_list, w_list, tm):
    m, d = x2d.shape
    n_in = len(o_list)
    in_specs = [pl.BlockSpec((tm, d), lambda i: (i, 0)), pl.BlockSpec((1, d), lambda i: (0, 0))]
    in_specs += [pl.BlockSpec((tm, o.shape[1]), lambda i: (i, 0)) for o in o_list]
    in_specs += [pl.BlockSpec(w.shape, lambda i: (0, 0)) for w in w_list]
    return pl.pallas_call(
        functools.partial(_proj_post_kernel, n_in=n_in),
        grid=(m // tm,), in_specs=in_specs, out_specs=pl.BlockSpec((tm, d), lambda i: (i, 0)),
        out_shape=jax.ShapeDtypeStruct((m, d), F32),
        compiler_params=_cparams(("parallel",)), name="proj_post")(x2d, g.reshape(1, d), *o_list, *w_list)


def _ffn_kernel(*refs, tm, tiles_per_seq, sample, sub):
    if sample:
        (x_ref, gpre_ref, gpost_ref, wg_ref, wu_ref, wc_ref, bc_ref, wd_ref, p1_ref, p2_ref,
         y_ref, gout_ref, h_ref, acc_ref, gs_ref) = refs
    else:
        (x_ref, gpre_ref, gpost_ref, wg_ref, wu_ref, wc_ref, bc_ref, wd_ref,
         y_ref, gout_ref, h_ref, acc_ref, gs_ref, halo_ref) = refs
    i = pl.program_id(0)
    c = pl.program_id(1)

    @pl.when(c == 0)
    def _():
        h_ref[...] = _rms(x_ref[...], gpre_ref[...]).astype(BF16)
        acc_ref[...] = jnp.zeros_like(acc_ref)

    h = h_ref[...]
    tf = wg_ref.shape[1]
    if sample:
        gs_ref[0:8, :] = jnp.zeros((8, tf), F32)
    else:
        first = (i % tiles_per_seq) == 0

        @pl.when(first)
        def _():
            gs_ref[0:8, :] = jnp.zeros((8, tf), F32)

        @pl.when(jnp.logical_not(first))
        def _():
            gs_ref[0:8, :] = halo_ref[c]

    if sample:
        t = lax.broadcasted_iota(jnp.int32, (tm, 1), 0) % tiles_per_seq
    part = None
    for lo in range(0, tf, sub):
        hi = min(lo + sub, tf)
        g = _dot(h, wg_ref[:, lo:hi])
        u = _dot(h, wu_ref[:, lo:hi])
        gs_ref[8:tm + 8, lo:hi] = g
        g1 = gs_ref[7:tm + 7, lo:hi]
        g2 = gs_ref[6:tm + 6, lo:hi]
        if sample:
            g1 = jnp.where(t >= 1, g1, 0.0) + p1_ref[:, lo:hi]
            g2 = jnp.where(t >= 2, g2, 0.0) + p2_ref[:, lo:hi]
            gout_ref[:, lo:hi] = g
        w = wc_ref[:, lo:hi]
        gc = bc_ref[:, lo:hi] + w[0:1, :] * g2
        gc = gc + w[1:2, :] * g1
        gc = gc + w[2:3, :] * g
        a = _gelu_tanh(gc) * u
        d = _dot(a.astype(BF16), wd_ref[lo:hi, :])
        part = d if part is None else part + d
    acc_ref[...] += part
    if not sample:
        tail = gs_ref[tm:tm + 8, :]
        halo_ref[c] = tail
        gout_ref[0] = tail

    @pl.when(c == pl.num_programs(1) - 1)
    def _():
        y_ref[...] = x_ref[...] + _rms(acc_ref[...], gpost_ref[...])


def _ffn(x2d, gpre, gpost, wg, wu, wc, bc, wd, tm, tf, tiles_per_seq, prev=None):
    m, d = x2d.shape
    dff = wg.shape[1]
    nc = dff // tf
    sample = prev is not None
    in_specs = [
        pl.BlockSpec((tm, d), lambda i, c: (i, 0)),
        pl.BlockSpec((1, d), lambda i, c: (0, 0)),
        pl.BlockSpec((1, d), lambda i, c: (0, 0)),
        pl.BlockSpec((d, tf), lambda i, c: (0, c)),
        pl.BlockSpec((d, tf), lambda i, c: (0, c)),
        pl.BlockSpec((CONV_W, tf), lambda i, c: (0, c)),
        pl.BlockSpec((1, tf), lambda i, c: (0, c)),
        pl.BlockSpec((tf, d), lambda i, c: (c, 0)),
    ]
    args = [x2d, gpre.reshape(1, d), gpost.reshape(1, d), wg, wu, wc, bc.reshape(1, dff), wd]
    scratch = [pltpu.VMEM((tm, d), BF16), pltpu.VMEM((tm, d), F32), pltpu.VMEM((tm + 8, tf), F32)]
    if sample:
        in_specs += [pl.BlockSpec((tm, tf), lambda i, c: (i, c))] * 2
        args += list(prev)
        gout_shape = jax.ShapeDtypeStruct((m, dff), F32)
        gout_spec = pl.BlockSpec((tm, tf), lambda i, c: (i, c))
    else:
        gout_shape = jax.ShapeDtypeStruct((m // tm, 8, dff), F32)
        gout_spec = pl.BlockSpec((1, 8, tf), lambda i, c: (i, 0, c))
        scratch.append(pltpu.VMEM((nc, 8, tf), F32))
    return pl.pallas_call(
        functools.partial(_ffn_kernel, tm=tm, tiles_per_seq=tiles_per_seq, sample=sample, sub=2 * LANES),
        grid=(m // tm, nc), in_specs=in_specs,
        out_specs=[pl.BlockSpec((tm, d), lambda i, c: (i, 0)), gout_spec],
        out_shape=[jax.ShapeDtypeStruct((m, d), F32), gout_shape],
        scratch_shapes=scratch,
        compiler_params=_cparams(("arbitrary", "arbitrary")), name="conv_ffn")(*args)


def _cumsum_kernel(lf_ref, cum_ref, cumt_ref, *, tb, nchunk):
    row = lax.broadcasted_iota(jnp.int32, (tb, tb), 0)
    col = lax.broadcasted_iota(jnp.int32, (tb, tb), 1)
    lower = jnp.where(col <= row, 1.0, 0.0).astype(F32)
    carry = jnp.zeros((1, LANES), F32)
    for c in range(nchunk):
        x = lf_ref[0, c * tb:(c + 1) * tb, :]
        ct = _dot_hi(lower, x) + carry
        cumt_ref[0, c * tb:(c + 1) * tb, :] = ct
        carry = ct[tb - 1:tb, :]
        cum_ref[0, c] = ct.T[0:C_HEADS, :]


def _cumsum(lf3d, tb):
    b, t, _ = lf3d.shape
    nchunk = t // tb
    return pl.pallas_call(
        functools.partial(_cumsum_kernel, tb=tb, nchunk=nchunk),
        grid=(b,), in_specs=[pl.BlockSpec((1, t, LANES), lambda i: (i, 0, 0))],
        out_specs=[pl.BlockSpec((1, nchunk, C_HEADS, tb), lambda i: (i, 0, 0, 0)),
                   pl.BlockSpec((1, t, LANES), lambda i: (i, 0, 0))],
        out_shape=[jax.ShapeDtypeStruct((b, nchunk, C_HEADS, tb), F32),
                   jax.ShapeDtypeStruct((b, t, LANES), F32)],
        compiler_params=_cparams(("parallel",)), name="logf_cumsum")(lf3d)


def _flash2_kernel(*refs, mode, tb, nk, lam_init):
    if mode == "fox":
        (q_ref, k_ref, v_ref, cum_ref, cumt_ref, o_ref,
         kb_ref, vt_ref, m_ref, l_ref, acc_ref, ck_ref) = refs
    else:
        (q_ref, k_ref, v_ref, bias_ref, lamp_ref, subln_ref, o_ref,
         kb_ref, vt_ref, m_ref, l_ref, acc_ref) = refs
    g = pl.program_id(1)
    qi = pl.program_id(2)
    lane = lax.broadcasted_iota(jnp.int32, (tb, LANES), 1)
    krow = lax.broadcasted_iota(jnp.int32, (tb, tb), 0)
    qcol = lax.broadcasted_iota(jnp.int32, (tb, tb), 1)
    tri = krow <= qcol

    @pl.when(qi == 0)
    def _():
        if mode == "fox":
            for c in range(nk):
                vt_ref[c] = v_ref[0, :, c * tb:(c + 1) * tb].astype(BF16)
        else:
            kb_ref[...] = k_ref[0].astype(BF16)
            for c in range(nk):
                vt_ref[c] = v_ref[0, c * tb:(c + 1) * tb, :].T.astype(BF16)
        if mode == "fox":
            hrow = lax.broadcasted_iota(jnp.int32, (LANES, LANES), 0)
            for a in range(2):
                onehot = jnp.where(hrow == 2 * g + a, 1.0, 0.0).astype(F32)
                ck_ref[a] = _dot_hi(cumt_ref[0], onehot)

    q = q_ref[0] * (HEAD_DIM ** -0.5)
    qall = jnp.concatenate([jnp.where(lane < HEAD_DIM, q, 0.0), jnp.where(lane >= HEAD_DIM, q, 0.0)],
                           axis=0).astype(BF16)
    if mode == "fox":
        cq = jnp.concatenate([cum_ref[0, qi, pl.ds(2 * g + a, 1), :] for a in range(2)], axis=1)
    m_ref[...] = jnp.full(m_ref.shape, NEG, F32)
    l_ref[...] = jnp.zeros_like(l_ref)
    acc_ref[...] = jnp.zeros_like(acc_ref)
    rep = tb // LANES

    def chunk(c, diagonal):
        off = pl.multiple_of(c * tb, tb)
        if mode == "fox":
            zt = _dot_nt(k_ref[0, pl.ds(off, tb), :], qall)
            ck0 = ck_ref[0, pl.ds(off, tb), :]
            ck1 = ck_ref[1, pl.ds(off, tb), :]
            zt = zt - jnp.concatenate([ck0] * rep + [ck1] * rep, axis=1)
            shift = cq
        else:
            zt = _dot_nt(kb_ref[pl.ds(off, tb), :], qall)
            bt = bias_ref[0, jnp.minimum(qi - c, 2)]
            zt = zt + jnp.concatenate([bt, bt], axis=1)
            shift = None
        if diagonal:
            zt = jnp.where(jnp.concatenate([tri, tri], axis=1), zt, NEG)
        m_new, l_new, acc_new = _softmax_cols(zt, m_ref[...], l_ref[...], acc_ref[...], vt_ref[c], shift)
        m_ref[...] = m_new
        l_ref[...] = l_new
        acc_ref[...] = acc_new

    def body(c, carry):
        chunk(c, False)
        return carry

    lax.fori_loop(0, qi, body, 0)
    chunk(qi, True)
    o = acc_ref[...] / l_ref[...]
    o0 = o[:, 0:tb].T
    o1 = o[:, tb:2 * tb].T
    if mode == "fox":
        o_ref[0] = jnp.where(lane < HEAD_DIM, o0, o1)
    else:
        lp = lamp_ref[...]
        lam = (jnp.exp(jnp.sum(lp[0:1] * lp[1:2], axis=1, keepdims=True))
               - jnp.exp(jnp.sum(lp[2:3] * lp[3:4], axis=1, keepdims=True)) + lam_init)
        o_ref[0] = _rms(o0 - lam * o1, subln_ref[...]) * (1.0 - lam_init)


def _flash2(mode, q, k, v, extra, tb, lam_init=0.0):
    b, t, width = q.shape
    ng = width // LANES
    nq = t // tb
    in_specs = [pl.BlockSpec((1, tb, LANES), lambda i, g, j: (i, j, g)),
                pl.BlockSpec((1, t, LANES), lambda i, g, j: (i, 0, g)),
                pl.BlockSpec((1, t, LANES), lambda i, g, j: (i, 0, g))]
    scratch = [pltpu.VMEM((t, LANES), BF16), pltpu.VMEM((nq, LANES, tb), BF16),
               pltpu.VMEM((1, 2 * tb), F32), pltpu.VMEM((1, 2 * tb), F32), pltpu.VMEM((LANES, 2 * tb), F32)]
    if mode == "fox":
        cum, cumt = extra
        in_specs[2] = pl.BlockSpec((1, LANES, t), lambda i, g, j: (i, g, 0))
        in_specs += [pl.BlockSpec((1, nq, C_HEADS, tb), lambda i, g, j: (i, 0, 0, 0)),
                     pl.BlockSpec((1, t, LANES), lambda i, g, j: (i, 0, 0))]
        scratch.append(pltpu.VMEM((2, t, LANES), F32))
    else:
        bias, lamp, subln = extra
        in_specs += [pl.BlockSpec((1, 3, tb, tb), lambda i, g, j: (g, 0, 0, 0)),
                     pl.BlockSpec(lamp.shape, lambda i, g, j: (0, 0)),
                     pl.BlockSpec(subln.shape, lambda i, g, j: (0, 0))]
    return pl.pallas_call(
        functools.partial(_flash2_kernel, mode=mode, tb=tb, nk=nq, lam_init=lam_init),
        grid=(b, ng, nq), in_specs=in_specs,
        out_specs=pl.BlockSpec((1, tb, LANES), lambda i, g, j: (i, j, g)),
        out_shape=jax.ShapeDtypeStruct((b, t, width), F32),
        scratch_shapes=scratch,
        compiler_params=_cparams(("arbitrary", "arbitrary", "arbitrary")),
        name="attn_" + mode)(q, k, v, *extra)


def _dsa_p_kernel(qa_ref, ka_ref, vat_in_ref, qi_ref, ki_ref, wi_ref, bias_ref, o_ref,
                  vat_ref, key_ref, qm_ref, m_ref, l_ref, acc_ref, *, n_sel, nk):
    tb = LANES
    qi = pl.program_id(1)
    krow = lax.broadcasted_iota(jnp.int32, (tb, tb), 0)
    qcol = lax.broadcasted_iota(jnp.int32, (tb, tb), 1)
    tri = krow <= qcol

    @pl.when(qi == 0)
    def _():
        for c in range(nk // 2):
            vat_ref[c] = vat_in_ref[0, :, c * 2 * tb:(c + 1) * 2 * tb].astype(BF16)

    qidx = qi_ref[0]
    qh = jnp.concatenate([qidx[:, h * IDX_DIM:(h + 1) * IDX_DIM] for h in range(IDX_HEADS)],
                         axis=0).astype(BF16)
    wt = (wi_ref[0] * (IDX_DIM ** -0.5 * IDX_HEADS ** -0.5)).T
    wall = jnp.concatenate([wt[h:h + 1, :] for h in range(IDX_HEADS)], axis=1)

    npairs = (qi + 2) // 2

    def chunk_valid(c):
        return jnp.logical_or(c < qi, jnp.logical_and(c == qi, tri))

    def p1(p, carry):
        off = pl.multiple_of(p * 2 * tb, 2 * tb)
        sh = jnp.maximum(_dot_nt(ki_ref[0, pl.ds(off, 2 * tb), :], qh), 0.0) * wall
        s = sh[:, 0:tb]
        for h in range(1, IDX_HEADS):
            s = s + sh[:, h * tb:(h + 1) * tb]
        valid = jnp.concatenate([chunk_valid(2 * p), chunk_valid(2 * p + 1)], axis=0)
        key_ref[pl.ds(2 * p, 2)] = jnp.where(valid, s, -jnp.inf).reshape(2, tb, tb)
        return carry

    lax.fori_loop(0, npairs, p1, 0)

    def count(pred):
        def cb(p, acc):
            hit = jnp.where(pred(key_ref[pl.ds(2 * p, 2)]), 1.0, 0.0)
            return acc + (hit[0] + hit[1])
        acc = lax.fori_loop(0, npairs, cb, jnp.zeros((tb, tb), F32))
        return jnp.sum(acc, axis=0, keepdims=True)

    thr = _kth_largest(lambda trial: count(lambda kk: kk >= trial), n_sel, (1, tb))
    need = n_sel - count(lambda kk: kk > thr)
    lstrict = jnp.where(qcol < krow, 1.0, 0.0).astype(BF16)

    lane = qcol
    qa = qa_ref[0] * (HEAD_DIM ** -0.5)
    for j in range(A_HEADS):
        blk = qa[:, (j // 2) * LANES:(j // 2 + 1) * LANES]
        grp = j // A_GROUP
        if j % 2 != grp:
            blk = pltpu.roll(blk, HEAD_DIM, axis=1)
        keep = (lane < HEAD_DIM) if grp == 0 else (lane >= HEAD_DIM)
        qm_ref[j] = jnp.where(keep, blk, 0.0).astype(BF16)
    m_ref[...] = jnp.full(m_ref.shape, NEG, F32)
    l_ref[...] = jnp.zeros_like(l_ref)
    acc_ref[...] = jnp.zeros_like(acc_ref)

    qall = qm_ref[...].reshape(A_HEADS * tb, LANES)

    def chunk_mask(c, carry):
        kk = key_ref[c]
        eq = kk == thr
        eqf = jnp.where(eq, 1.0, 0.0)
        prefix = _dot(lstrict, eqf.astype(BF16)) + carry
        sel = jnp.logical_or(kk > thr, jnp.logical_and(eq, prefix < need))
        sel = jnp.logical_and(sel, chunk_valid(c))
        am = jnp.where(sel, 0.0, NEG)
        return jnp.concatenate([am] * A_HEADS, axis=1), carry + jnp.sum(eqf, axis=0, keepdims=True)

    def p4(p, carry):
        am_a, carry = chunk_mask(2 * p, carry)
        am_b, carry = chunk_mask(2 * p + 1, carry)
        bias = jnp.concatenate([bias_ref[jnp.clip(qi - 2 * p, 0, 2)], bias_ref[jnp.clip(qi - 2 * p - 1, 0, 2)]],
                               axis=0)
        off = pl.multiple_of(p * 2 * tb, 2 * tb)
        zt = (_dot_nt(ka_ref[0, pl.ds(off, 2 * tb), :], qall) + bias
              + jnp.concatenate([am_a, am_b], axis=0))
        m_new, l_new, acc_new = _softmax_cols(zt, m_ref[...], l_ref[...], acc_ref[...], vat_ref[p])
        m_ref[...] = m_new
        l_ref[...] = l_new
        acc_ref[...] = acc_new
        return carry

    lax.fori_loop(0, npairs, p4, jnp.zeros((1, tb), F32))
    oall = acc_ref[...] / l_ref[...]
    for c2 in range(A_HEADS // 2):
        grp = (2 * c2) // A_GROUP
        lo = oall[:, (2 * c2) * tb:(2 * c2 + 1) * tb].T
        hi = oall[:, (2 * c2 + 1) * tb:(2 * c2 + 2) * tb].T
        if grp == 0:
            hi = pltpu.roll(hi, HEAD_DIM, axis=1)
        else:
            lo = pltpu.roll(lo, HEAD_DIM, axis=1)
        o_ref[0, :, c2 * LANES:(c2 + 1) * LANES] = jnp.where(lane < HEAD_DIM, lo, hi)


def _dsa_p(q_a, k_a, v_at, q_i, k_i, w_i, bias, n_sel):
    b, t, _ = q_a.shape
    tb = LANES
    nq = t // tb
    in_specs = [pl.BlockSpec((1, tb, q_a.shape[2]), lambda i, j: (i, j, 0)),
                pl.BlockSpec((1, t, LANES), lambda i, j: (i, 0, 0)),
                pl.BlockSpec((1, LANES, t), lambda i, j: (i, 0, 0)),
                pl.BlockSpec((1, tb, q_i.shape[2]), lambda i, j: (i, j, 0)),
                pl.BlockSpec((1, t, IDX_DIM), lambda i, j: (i, 0, 0)),
                pl.BlockSpec((1, tb, LANES), lambda i, j: (i, j, 0)),
                pl.BlockSpec(bias.shape, lambda i, j: (0, 0, 0))]
    return pl.pallas_call(
        functools.partial(_dsa_p_kernel, n_sel=n_sel, nk=nq),
        grid=(b, nq), in_specs=in_specs,
        out_specs=pl.BlockSpec((1, tb, q_a.shape[2]), lambda i, j: (i, j, 0)),
        out_shape=jax.ShapeDtypeStruct(q_a.shape, F32),
        scratch_shapes=[pltpu.VMEM((nq // 2, LANES, 2 * tb), BF16),
                        pltpu.VMEM((nq, tb, tb), F32),
                        pltpu.VMEM((A_HEADS, tb, LANES), BF16),
                        pltpu.VMEM((1, A_HEADS * tb), F32), pltpu.VMEM((1, A_HEADS * tb), F32),
                        pltpu.VMEM((LANES, A_HEADS * tb), F32)],
        compiler_params=_cparams(("arbitrary", "arbitrary")), name="attn_dsa")(q_a, k_a, v_at, q_i, k_i, w_i, bias)


def _fox_s_kernel(pt_ref, q_ref, kn_ref, vn_ref, lfn_ref, *refs, ts, pp):
    kt_refs = refs[:pp]
    vt_refs = refs[pp:2 * pp]
    lft_refs = refs[2 * pp:3 * pp]
    o_ref = refs[3 * pp]
    qbd_ref, kpad_ref, vpad_ref, lfpad_ref, m_ref, l_ref, acc_ref, carry_ref, cq_ref = refs[3 * pp + 1:]
    b = pl.program_id(0)
    s = pl.program_id(1)
    nrow = ts * C_HEADS
    width = C_HEADS * HEAD_DIM
    row = lax.broadcasted_iota(jnp.int32, (PAGE, PAGE), 0)
    col = lax.broadcasted_iota(jnp.int32, (PAGE, PAGE), 1)

    @pl.when(jnp.logical_and(b == 0, s == 0))
    def _():
        kpad_ref[...] = jnp.zeros_like(kpad_ref)
        vpad_ref[...] = jnp.zeros_like(vpad_ref)

    def update(z, pv):
        m_new, l_new, acc_new = _softmax_rows(z, m_ref[...], l_ref[...], acc_ref[...], pv)
        m_ref[...] = m_new
        l_ref[...] = l_new
        acc_ref[...] = acc_new

    @pl.when(s == 0)
    def _():
        hmask = (lax.broadcasted_iota(jnp.int32, (C_HEADS, width), 1) // HEAD_DIM
                 == lax.broadcasted_iota(jnp.int32, (C_HEADS, width), 0))
        q = q_ref[0] * (HEAD_DIM ** -0.5)
        for i in range(ts):
            qbd_ref[i * C_HEADS:(i + 1) * C_HEADS, :] = jnp.where(
                hmask, jnp.broadcast_to(q[i:i + 1, :], (C_HEADS, width)), 0.0).astype(BF16)
        kpad_ref[0:ts, :] = kn_ref[0]
        vpad_ref[0:ts, :] = vn_ref[0]
        lfpad_ref[...] = jnp.zeros_like(lfpad_ref)
        lfpad_ref[0:ts, :] = lfn_ref[0]
        m_ref[...] = jnp.full(m_ref.shape, NEG, F32)
        l_ref[...] = jnp.zeros_like(l_ref)
        acc_ref[...] = jnp.zeros_like(acc_ref)
        carry_ref[...] = jnp.zeros_like(carry_ref)
        lft = lfpad_ref[...].T[0:C_HEADS, :]
        incl = jnp.where(row <= col, 1.0, 0.0).astype(F32)
        cnew = _dot_hi(lft, incl)
        for i in range(ts):
            cq_ref[i * C_HEADS:(i + 1) * C_HEADS, :] = jnp.broadcast_to(cnew[:, i:i + 1], (C_HEADS, PAGE))
        z = _dot_nt(qbd_ref[...], kpad_ref[...].astype(BF16))
        z = z + cq_ref[...] - jnp.concatenate([cnew] * ts, axis=0)
        rr = lax.broadcasted_iota(jnp.int32, (nrow, PAGE), 0)
        cc = lax.broadcasted_iota(jnp.int32, (nrow, PAGE), 1)
        z = jnp.where(cc * C_HEADS <= rr, z, NEG)
        vnew = vpad_ref[...].astype(BF16)
        update(z, lambda p: _dot(p, vnew))

    later = jnp.where(row > col, 1.0, 0.0).astype(F32)
    carry = carry_ref[...]
    sufs = []
    for k in range(pp):
        lft = lft_refs[k][...]
        sufs.append(_dot_hi(lft, later) + carry)
        carry = carry + jnp.sum(lft, axis=1, keepdims=True)
    carry_ref[...] = carry
    suf = jnp.concatenate(sufs, axis=1)
    kt = jnp.concatenate([kt_refs[k][...].astype(BF16) for k in range(pp)], axis=1)
    vt = jnp.concatenate([vt_refs[k][...].astype(BF16) for k in range(pp)], axis=1)
    z = _dot(qbd_ref[...], kt)
    z = z + jnp.concatenate([cq_ref[...]] * pp, axis=1) + jnp.concatenate([suf] * ts, axis=0)
    update(z, lambda p: _dot_nt(p, vt))

    @pl.when(s == pl.num_programs(1) - 1)
    def _():
        hmask = (lax.broadcasted_iota(jnp.int32, (C_HEADS, width), 1) // HEAD_DIM
                 == lax.broadcasted_iota(jnp.int32, (C_HEADS, width), 0))
        o = acc_ref[...] / l_ref[...]
        for i in range(ts):
            blk = jnp.where(hmask, o[i * C_HEADS:(i + 1) * C_HEADS, :], 0.0)
            o_ref[0, i:i + 1, :] = jnp.sum(blk, axis=0, keepdims=True)


def _fox_s(q, kn, vn, lfn, cache_kt, cache_vt, cache_lft, layer, page_table, pp):
    b, ts, width = q.shape
    npg = page_table.shape[1]
    nrow = ts * C_HEADS

    def page_idx(k):
        return lambda i, s, pt: (layer, pt[i, npg - 1 - (s * pp + k)], 0, 0)

    def seq_idx(i, s, pt):
        return (i, 0, 0)

    in_specs = [pl.BlockSpec((1, ts, width), seq_idx), pl.BlockSpec((1, ts, width), seq_idx),
                pl.BlockSpec((1, ts, width), seq_idx), pl.BlockSpec((1, ts, LANES), seq_idx)]
    in_specs += [pl.BlockSpec((None, None, width, PAGE), page_idx(k)) for k in range(pp)]
    in_specs += [pl.BlockSpec((None, None, width, PAGE), page_idx(k)) for k in range(pp)]
    in_specs += [pl.BlockSpec((None, None, C_HEADS, PAGE), page_idx(k)) for k in range(pp)]
    grid_spec = pltpu.PrefetchScalarGridSpec(
        num_scalar_prefetch=1, grid=(b, npg // pp), in_specs=in_specs,
        out_specs=pl.BlockSpec((1, ts, width), seq_idx),
        scratch_shapes=[pltpu.VMEM((nrow, width), BF16), pltpu.VMEM((PAGE, width), F32),
                        pltpu.VMEM((PAGE, width), F32), pltpu.VMEM((PAGE, LANES), F32),
                        pltpu.VMEM((nrow, 1), F32), pltpu.VMEM((nrow, 1), F32),
                        pltpu.VMEM((nrow, width), F32), pltpu.VMEM((C_HEADS, PAGE), F32),
                        pltpu.VMEM((nrow, PAGE), F32)])
    return pl.pallas_call(
        functools.partial(_fox_s_kernel, ts=ts, pp=pp), grid_spec=grid_spec,
        out_shape=jax.ShapeDtypeStruct((b, ts, width), F32),
        compiler_params=_cparams(("arbitrary", "arbitrary")), name="attn_fox_sample")(
            page_table, q, kn, vn, lfn, *([cache_kt] * pp), *([cache_vt] * pp), *([cache_lft] * pp))


def _diff_s_kernel(pt_ref, q_ref, kn_ref, vn_ref, *refs, ts, pp, npg, lam_init):
    k_refs = refs[:pp]
    v_refs = refs[pp:2 * pp]
    (bias_ref, biasn_ref, lamp_ref, subln_ref, o_ref,
     qx_ref, kpad_ref, vpad_ref, m_ref, l_ref, acc_ref) = refs[2 * pp:]
    b = pl.program_id(0)
    s = pl.program_id(1)
    hrows = B_HEADS * ts
    lane = lax.broadcasted_iota(jnp.int32, (ts, LANES), 1)

    @pl.when(jnp.logical_and(b == 0, s == 0))
    def _():
        kpad_ref[...] = jnp.zeros_like(kpad_ref)
        vpad_ref[...] = jnp.zeros_like(vpad_ref)

    @pl.when(s == 0)
    def _():
        q = q_ref[0] * (HEAD_DIM ** -0.5)
        for n in range(B_HEADS):
            blk = q[:, n * LANES:(n + 1) * LANES]
            qx_ref[n * ts:(n + 1) * ts, :] = jnp.where(lane < HEAD_DIM, blk, 0.0)
            qx_ref[hrows + n * ts:hrows + (n + 1) * ts, :] = jnp.where(lane >= HEAD_DIM, blk, 0.0)
        m_ref[...] = jnp.full(m_ref.shape, NEG, F32)
        l_ref[...] = jnp.zeros_like(l_ref)
        acc_ref[...] = jnp.zeros_like(acc_ref)

    def step(kall, vall, bias):
        z = _dot_nt(qx_ref[...].astype(BF16), kall) + bias
        m_new, l_new, acc_new = _softmax_rows(z, m_ref[...], l_ref[...], acc_ref[...], lambda p: _dot(p, vall))
        m_ref[...] = m_new
        l_ref[...] = l_new
        acc_ref[...] = acc_new

    step(jnp.concatenate([k_refs[k][...].astype(BF16) for k in range(pp)], axis=0),
         jnp.concatenate([v_refs[k][...].astype(BF16) for k in range(pp)], axis=0), bias_ref[s])

    @pl.when(s == pl.num_programs(1) - 1)
    def _():
        kpad_ref[0:ts * B_HEADS, :] = kn_ref[0]
        vpad_ref[0:ts * B_HEADS, :] = vn_ref[0]
        step(kpad_ref[...].astype(BF16), vpad_ref[...].astype(BF16), biasn_ref[...])
        lp = lamp_ref[...]
        lam = (jnp.exp(jnp.sum(lp[0:1] * lp[1:2], axis=1, keepdims=True))
               - jnp.exp(jnp.sum(lp[2:3] * lp[3:4], axis=1, keepdims=True)) + lam_init)
        o = acc_ref[...] / l_ref[...]
        od = o[0:hrows] - lam * o[hrows:2 * hrows]
        for n in range(B_HEADS):
            o_ref[0, :, n * B_VDIM:(n + 1) * B_VDIM] = (
                _rms(od[n * ts:(n + 1) * ts], subln_ref[...]) * (1.0 - lam_init))


def _diff_s(q, kn16, vn16, cache_k, cache_v, layer, page_table, bias, bias_new, lamp, subln, lam_init, pp):
    b, ts, width = q.shape
    npg = page_table.shape[1]
    nrow = 2 * B_HEADS * ts
    krows = PAGE * B_HEADS

    def page_idx(k):
        return lambda i, s, pt: (layer, pt[i, s * pp + k], 0, 0)

    def seq_idx(i, s, pt):
        return (i, 0, 0)

    in_specs = [pl.BlockSpec((1, ts, width), seq_idx), pl.BlockSpec((1,) + kn16.shape[1:], seq_idx),
                pl.BlockSpec((1,) + vn16.shape[1:], seq_idx)]
    in_specs += [pl.BlockSpec((None, None, krows, LANES), page_idx(k)) for k in range(pp)] * 2
    in_specs += [pl.BlockSpec(bias.shape, lambda i, s, pt: (0, 0, 0)),
                 pl.BlockSpec(bias_new.shape, lambda i, s, pt: (0, 0)),
                 pl.BlockSpec(lamp.shape, lambda i, s, pt: (0, 0)),
                 pl.BlockSpec(subln.shape, lambda i, s, pt: (0, 0))]
    grid_spec = pltpu.PrefetchScalarGridSpec(
        num_scalar_prefetch=1, grid=(b, npg // pp), in_specs=in_specs,
        out_specs=pl.BlockSpec((1, ts, width), seq_idx),
        scratch_shapes=[pltpu.VMEM((nrow, LANES), F32), pltpu.VMEM((krows, LANES), F32),
                        pltpu.VMEM((krows, LANES), F32), pltpu.VMEM((nrow, 1), F32),
                        pltpu.VMEM((nrow, 1), F32), pltpu.VMEM((nrow, LANES), F32)])
    return pl.pallas_call(
        functools.partial(_diff_s_kernel, ts=ts, pp=pp, npg=npg, lam_init=lam_init), grid_spec=grid_spec,
        out_shape=jax.ShapeDtypeStruct((b, ts, width), F32),
        compiler_params=_cparams(("arbitrary", "arbitrary")), name="attn_diff_sample")(
            page_table, q, kn16, vn16, *([cache_k] * pp), *([cache_v] * pp), bias, bias_new, lamp, subln)


def _dsa_index_s_kernel(pt_ref, qi_ref, wi_ref, kin_ref, *refs, ts, pp, nchunk):
    kit_refs = refs[:pp]
    key_ref, pad_ref = refs[pp:]
    b = pl.program_id(0)
    s = pl.program_id(1)
    last = nchunk - 1
    rpad = SUBLANES

    @pl.when(jnp.logical_and(b == 0, s == 0))
    def _():
        pad_ref[...] = jnp.zeros_like(pad_ref)

    def index_scores(kit):
        s32 = jnp.maximum(_dot(qi_ref[0].astype(BF16), kit.astype(BF16)), 0.0) * wi_ref[0]
        sc = s32[0:rpad]
        for h in range(1, IDX_HEADS):
            sc = sc + s32[h * rpad:(h + 1) * rpad]
        return sc

    for k in range(pp):
        key_ref[0, s * pp + k] = index_scores(kit_refs[k][...])

    @pl.when(s == pl.num_programs(1) - 1)
    def _():
        pad_ref[0:ts, 0:IDX_DIM] = kin_ref[0]
        rr = lax.broadcasted_iota(jnp.int32, (rpad, PAGE), 0)
        cc = lax.broadcasted_iota(jnp.int32, (rpad, PAGE), 1)
        causal = jnp.logical_and(cc <= rr, cc < ts)
        key_ref[0, last] = jnp.where(causal, index_scores(pad_ref[...].T[0:IDX_DIM, :]), -jnp.inf)


def _dsa_index_s(qi32, w32, kin, cache_kit, layer, page_table, ts, pp):
    b = qi32.shape[0]
    npg = page_table.shape[1]
    nchunk = npg + 1

    def page_idx(k):
        return lambda i, s, pt: (layer, pt[i, s * pp + k], 0, 0)

    def seq_idx(i, s, pt):
        return (i, 0, 0)

    in_specs = [pl.BlockSpec((1,) + qi32.shape[1:], seq_idx), pl.BlockSpec((1,) + w32.shape[1:], seq_idx),
                pl.BlockSpec((1, ts, IDX_DIM), seq_idx)]
    in_specs += [pl.BlockSpec((None, None, IDX_DIM, PAGE), page_idx(k)) for k in range(pp)]
    grid_spec = pltpu.PrefetchScalarGridSpec(
        num_scalar_prefetch=1, grid=(b, npg // pp), in_specs=in_specs,
        out_specs=pl.BlockSpec((1, nchunk, SUBLANES, PAGE), lambda i, s, pt: (i, 0, 0, 0)),
        scratch_shapes=[pltpu.VMEM((PAGE, LANES), F32)])
    return pl.pallas_call(
        functools.partial(_dsa_index_s_kernel, ts=ts, pp=pp, nchunk=nchunk), grid_spec=grid_spec,
        out_shape=jax.ShapeDtypeStruct((b, nchunk, SUBLANES, PAGE), F32),
        compiler_params=_cparams(("arbitrary", "arbitrary")), name="dsa_index_sample")(
            page_table, qi32, w32, kin, *([cache_kit] * pp))


def _select_kernel(keys_ref, thr_ref, need_ref, *, n_sel, nchunk):
    r = keys_ref.shape[1]
    wid = min(r, LANES)

    def count(pred, ref):
        out = []
        for lo in range(0, r, wid):
            refv = ref[:, lo:lo + wid]

            def cb(c, acc, lo=lo, refv=refv):
                off = pl.multiple_of(c * PAGE, PAGE)
                return acc + jnp.where(pred(keys_ref[pl.ds(off, PAGE), lo:lo + wid], refv), 1.0, 0.0)
            acc = lax.fori_loop(0, nchunk, cb, jnp.zeros((PAGE, wid), F32))
            out.append(jnp.sum(acc, axis=0, keepdims=True))
        return jnp.concatenate(out, axis=1)

    thr = _kth_largest(lambda trial: count(lambda kk, t: kk >= t, trial), n_sel, (1, r))
    thr_ref[...] = thr
    need_ref[...] = n_sel - count(lambda kk, t: kk > t, thr)


def _select(keys_t, n_sel):
    nkeys, r = keys_t.shape
    return pl.pallas_call(
        functools.partial(_select_kernel, n_sel=n_sel, nchunk=nkeys // PAGE),
        out_shape=[jax.ShapeDtypeStruct((1, r), F32), jax.ShapeDtypeStruct((1, r), F32)],
        compiler_params=pltpu.CompilerParams(vmem_limit_bytes=VMEM_LIMIT), name="dsa_select_sample")(keys_t)


def _dsa_attn_s_kernel(pt_ref, qa_ref, kn_ref, vn_ref, key_ref, thr_ref, need_ref, *refs, ts, pp, nchunk):
    kt_refs = refs[:pp]
    vt_refs = refs[pp:2 * pp]
    bias_ref, o_ref, kst_ref, vst_ref, pad_ref = refs[2 * pp:]
    b = pl.program_id(0)
    s = pl.program_id(1)
    last = nchunk - 1
    rpad = SUBLANES

    @pl.when(jnp.logical_and(b == 0, s == 0))
    def _():
        pad_ref[...] = jnp.zeros_like(pad_ref)

    for k in range(pp):
        kst_ref[s * pp + k] = kt_refs[k][...]
        vst_ref[s * pp + k] = vt_refs[k][...]

    @pl.when(s == pl.num_programs(1) - 1)
    def _():
        pad_ref[0, 0:ts, :] = kn_ref[0]
        pad_ref[1, 0:ts, :] = vn_ref[0]
        kst_ref[last] = pad_ref[0].T
        vst_ref[last] = pad_ref[1].T
        rr = lax.broadcasted_iota(jnp.int32, (rpad, PAGE), 0)
        cc = lax.broadcasted_iota(jnp.int32, (rpad, PAGE), 1)
        causal = jnp.logical_and(cc <= rr, cc < ts)
        thr = thr_ref[0]
        need = need_ref[0]
        r2 = lax.broadcasted_iota(jnp.int32, (PAGE, PAGE), 0)
        c2 = lax.broadcasted_iota(jnp.int32, (PAGE, PAGE), 1)
        ustrict = jnp.where(r2 < c2, 1.0, 0.0).astype(BF16)
        nrow = A_HEADS * rpad
        qbd = qa_ref[0].astype(BF16)
        carry = jnp.zeros((rpad, 1), F32)
        zs = []
        for c in range(nchunk):
            kk = key_ref[0, c]
            eq = kk == thr
            eqf = jnp.where(eq, 1.0, 0.0)
            prefix = _dot(eqf.astype(BF16), ustrict) + carry
            sel = jnp.logical_or(kk > thr, jnp.logical_and(eq, prefix < need))
            if c == last:
                sel = jnp.logical_and(sel, causal)
            am = jnp.where(sel, 0.0, NEG)
            carry = carry + jnp.sum(eqf, axis=1, keepdims=True)
            zs.append(_dot(qbd, kst_ref[c].astype(BF16)) + bias_ref[c] + jnp.concatenate([am] * A_HEADS, axis=0))
        zmax = zs[0]
        for c in range(1, nchunk):
            zmax = jnp.maximum(zmax, zs[c])
        m_row = jnp.max(zmax, axis=1, keepdims=True)
        psum = jnp.zeros((nrow, LANES), F32)
        acc = jnp.zeros((nrow, LANES), F32)
        for c in range(nchunk):
            p = jnp.exp(zs[c] - m_row)
            psum = psum + p
            acc = acc + _dot_nt(p.astype(BF16), vst_ref[c].astype(BF16))
        o = acc / jnp.sum(psum, axis=1, keepdims=True)
        lane = lax.broadcasted_iota(jnp.int32, (rpad, LANES), 1)
        for blk in range(A_HEADS // 2):
            grp = (2 * blk) // A_GROUP
            lo = o[(2 * blk) * rpad:(2 * blk + 1) * rpad]
            hi = o[(2 * blk + 1) * rpad:(2 * blk + 2) * rpad]
            if grp == 0:
                hi = pltpu.roll(hi, HEAD_DIM, axis=1)
            else:
                lo = pltpu.roll(lo, HEAD_DIM, axis=1)
            res = jnp.where(lane < HEAD_DIM, lo, hi)
            o_ref[0, :, blk * LANES:(blk + 1) * LANES] = res[0:ts]


def _dsa_attn_s(qa_bd, kn, vn, keys, thr, need, cache_kt, cache_vt, layer, page_table, bias, ts, pp):
    b = qa_bd.shape[0]
    npg = page_table.shape[1]
    nchunk = npg + 1
    width = A_HEADS * HEAD_DIM

    def page_idx(k):
        return lambda i, s, pt: (layer, pt[i, s * pp + k], 0, 0)

    def seq_idx(i, s, pt):
        return (i, 0, 0)

    in_specs = [pl.BlockSpec((1,) + qa_bd.shape[1:], seq_idx),
                pl.BlockSpec((1, ts, LANES), seq_idx), pl.BlockSpec((1, ts, LANES), seq_idx),
                pl.BlockSpec((1, nchunk, SUBLANES, PAGE), lambda i, s, pt: (i, 0, 0, 0)),
                pl.BlockSpec((1, SUBLANES, PAGE), seq_idx), pl.BlockSpec((1, SUBLANES, PAGE), seq_idx)]
    in_specs += [pl.BlockSpec((None, None, LANES, PAGE), page_idx(k)) for k in range(pp)] * 2
    in_specs += [pl.BlockSpec(bias.shape, lambda i, s, pt: (0, 0, 0))]
    grid_spec = pltpu.PrefetchScalarGridSpec(
        num_scalar_prefetch=1, grid=(b, npg // pp), in_specs=in_specs,
        out_specs=pl.BlockSpec((1, ts, width), seq_idx),
        scratch_shapes=[pltpu.VMEM((nchunk, LANES, PAGE), F32), pltpu.VMEM((nchunk, LANES, PAGE), F32),
                        pltpu.VMEM((2, PAGE, LANES), F32)])
    return pl.pallas_call(
        functools.partial(_dsa_attn_s_kernel, ts=ts, pp=pp, nchunk=nchunk), grid_spec=grid_spec,
        out_shape=jax.ShapeDtypeStruct((b, ts, width), F32),
        compiler_params=_cparams(("arbitrary", "arbitrary")), name="attn_dsa_sample")(
            page_table, qa_bd, kn, vn, keys, thr, need, *([cache_kt] * pp), *([cache_vt] * pp), bias)


def _bucket_starts():
    exact = N_BUCKETS // 2
    starts = list(range(exact))
    for j in range(N_BUCKETS - exact):
        starts.append(min(n for n in range(exact, MAX_DISTANCE + 1)
                          if int(math.log(n / exact) / math.log(MAX_DISTANCE / exact) * (N_BUCKETS - exact)) >= j))
    return starts


def _bias_of(tab, rel):
    n = jnp.maximum(rel, 0)[None]
    tab = tab.astype(F32)
    out = jnp.broadcast_to(tab[0].reshape((-1,) + (1,) * rel.ndim), (tab.shape[1],) + rel.shape)
    for bkt, start in enumerate(_bucket_starts()):
        if bkt:
            out = jnp.where(n >= start, tab[bkt].reshape((-1,) + (1,) * rel.ndim), out)
    return out


def _bias_tiles_t(tab, tb):
    j = jnp.arange(tb, dtype=jnp.int32)[:, None]
    i = jnp.arange(tb, dtype=jnp.int32)[None, :]
    return _bias_of(tab, jnp.stack([d * tb + i - j for d in range(3)]))


def _split(w, sizes):
    offs = np.cumsum((0,) + tuple(sizes))
    return [w[:, int(offs[i]):int(offs[i + 1])] for i in range(len(sizes))]


def _pad_cols(w, n):
    return jnp.pad(w, ((0, 0), (0, n - w.shape[1])))


def _row_tile(m, want):
    t = min(m, want)
    while m % t:
        t //= 2
    return t


def _pages_per_step(npg, want):
    pp = min(npg, want)
    while npg % pp:
        pp -= 1
    return pp


def kernel(x_prompt, x_sample, cache_a_k, cache_a_v, cache_a_kidx, cache_b_k, cache_b_v, cache_c_k, cache_c_v,
           cache_c_logf, state_conv, page_table, rel_bias_table, w_in_even, w_out_even, lambda_q1, lambda_k1,
           lambda_q2, lambda_k2, diff_subln, w_in_odd, b_forget, w_out_odd, norm_mix_pre, norm_mix_post,
           norm_ffn_pre, norm_ffn_post, w_gate_up, w_conv, b_conv, w_down):
    bp, tp, d = x_prompt.shape
    bs, ts, _ = x_sample.shape
    depth = w_gate_up.shape[0]
    n_pool = cache_a_k.shape[1]
    npg = page_table.shape[1]
    past = npg * PAGE
    mp, ms = bp * tp, bs * ts
    tb_dsa = LANES
    tb_attn = min(512, tp)
    nsel_p = min(TOPK_MAX, tp // 4)
    nsel_s = min(TOPK_MAX, (past + ts) // 4)
    tm_p = _row_tile(mp, 512)
    tm_s = _row_tile(ms, 512)
    tf = D_FF // 2
    dff = D_FF
    pp = _pages_per_step(npg, 8)

    xp = x_prompt.reshape(mp, d)
    xs = x_sample.reshape(ms, d)
    tab_a = rel_bias_table[:, :A_HEADS]
    tab_b = rel_bias_table[:, A_HEADS:]
    bias_a_p = _bias_tiles_t(tab_a, tb_dsa)
    bias_a_p = jnp.transpose(bias_a_p, (1, 2, 0, 3)).reshape(3, tb_dsa, A_HEADS * tb_dsa)
    bias_b_p = _bias_tiles_t(tab_b, tb_attn)
    rel_s = (past + jnp.arange(ts, dtype=jnp.int32))[:, None] - jnp.arange(past + PAGE, dtype=jnp.int32)[None, :]
    rows_a = _bias_of(tab_a, rel_s).reshape(A_HEADS, ts, npg + 1, PAGE)
    rows_b = _bias_of(tab_b, rel_s).reshape(B_HEADS, ts, npg + 1, PAGE)
    bias_a_s = jnp.pad(jnp.moveaxis(rows_a, 2, 0), ((0, 0), (0, 0), (0, SUBLANES - ts), (0, 0)))
    bias_a_s = bias_a_s.reshape(npg + 1, A_HEADS * SUBLANES, PAGE)
    kq_ok = rel_s.reshape(ts, npg + 1, PAGE) >= 0
    same = jnp.eye(B_HEADS, dtype=bool)
    bias_b_s = jnp.where(same[:, None, None, None, :] & kq_ok[None, :, :, :, None],
                         rows_b[..., None], NEG)
    bias_b_s = jnp.moveaxis(bias_b_s, 2, 0).reshape(npg + 1, B_HEADS * ts, PAGE * B_HEADS)
    bias_b_s = jnp.concatenate([bias_b_s, bias_b_s], axis=1)
    bias_b_new = bias_b_s[npg]
    bias_b_s = bias_b_s[:npg].reshape(npg // pp, pp, 2 * B_HEADS * ts, PAGE * B_HEADS)
    bias_b_s = jnp.swapaxes(bias_b_s, 1, 2).reshape(npg // pp, 2 * B_HEADS * ts, pp * PAGE * B_HEADS)

    ca_kt = jnp.transpose(cache_a_k, (0, 1, 3, 4, 2)).reshape(-1, n_pool, LANES, PAGE)
    ca_vt = jnp.transpose(cache_a_v, (0, 1, 3, 4, 2)).reshape(-1, n_pool, LANES, PAGE)
    ca_kit = jnp.transpose(cache_a_kidx, (0, 1, 3, 2))
    cb_k = cache_b_k.reshape(-1, n_pool, PAGE * B_HEADS, 2 * HEAD_DIM)
    cb_v = cache_b_v.reshape(-1, n_pool, PAGE * B_HEADS, B_VDIM)
    cc_kt = jnp.transpose(cache_c_k, (0, 1, 3, 4, 2)).reshape(-1, n_pool, C_HEADS * HEAD_DIM, PAGE)
    cc_vt = jnp.transpose(cache_c_v, (0, 1, 3, 4, 2)).reshape(-1, n_pool, C_HEADS * HEAD_DIM, PAGE)
    cc_lft = jnp.transpose(cache_c_logf, (0, 1, 3, 2))

    even_rows_p, even_rows_s, odd_rows_p, odd_rows_s, conv_p, conv_s = [], [], [], [], [], []
    for l in range(depth):
        if l % 2 == 0:
            e = l // 2
            lam_init = 0.8 - 0.6 * math.exp(-0.3 * l)
            ws = _split(w_in_even[e], EVEN_SPLIT)
            ws[5] = _pad_cols(ws[5], LANES)
            ws = [w.astype(BF16) for w in ws]
            wo = w_out_even[e].astype(BF16)
            wo_parts = [wo[:A_HEADS * HEAD_DIM], wo[A_HEADS * HEAD_DIM:]]
            lamp = jnp.stack([lambda_q1[e], lambda_k1[e], lambda_q2[e], lambda_k2[e]]).astype(F32)
            subln = diff_subln[e].reshape(1, B_VDIM)

            wq_a, wk_a, wv_a, wq_i, wk_i, ww_i, wq_b, wk_b, wv_b = ws
            items = [(wq_a, "f32"), (wk_a, "bf16"), (wk_a.T, "col"), (wv_a.T, "col"), (wq_i, "f32"),
                     (wk_i, "bf16"), (wk_i.T, "col"), (ww_i, "f32"), (wq_b, "f32"), (wk_b, "f32"), (wv_b, "f32")]
            q_a, k_a16, k_at, v_at, q_i, k_i16, k_it, w_i, q_b, k_b, v_b = _rms_proj(
                xp, norm_mix_pre[l], items, tm_p, seq=(bp, tp))
            r3 = lambda a: a.reshape(bp, tp, a.shape[1])
            o_a = _dsa_p(r3(q_a), r3(k_a16), v_at, r3(q_i), r3(k_i16), r3(w_i), bias_a_p, nsel_p)
            o_b = _flash2("diff", r3(q_b), r3(k_b), r3(v_b), (bias_b_p, lamp, subln), tb_attn, lam_init)
            xp = _proj_post(xp, norm_mix_post[l], [o_a.reshape(mp, -1), o_b.reshape(mp, -1)], wo_parts, tm_p)
            kv_rows = lambda a: jnp.transpose(a.reshape(bp, A_KV_HEADS, HEAD_DIM, tp), (0, 3, 1, 2))
            even_rows_p.append((kv_rows(k_at), kv_rows(v_at), jnp.transpose(k_it, (0, 2, 1)),
                                k_b.reshape(bp, tp, B_HEADS, 2 * HEAD_DIM), v_b.reshape(bp, tp, B_HEADS, B_VDIM)))

            q_a, k_a, v_a, q_i, k_i, w_i, q_b, k_b, v_b = _rms_proj(
                xs, norm_mix_pre[l], [(w, "f32") for w in ws], tm_s)
            s3 = lambda a: a.reshape(bs, ts, a.shape[1])
            qa4 = (q_a * (HEAD_DIM ** -0.5)).reshape(bs, ts, A_HEADS, HEAD_DIM)
            qa4 = jnp.pad(jnp.moveaxis(qa4, 1, 2), ((0, 0), (0, 0), (0, SUBLANES - ts), (0, 0)))
            zeros = jnp.zeros_like(qa4)
            qa_bd = jnp.concatenate([jnp.concatenate([qa4[:, :A_GROUP], zeros[:, :A_GROUP]], axis=-1),
                                     jnp.concatenate([zeros[:, A_GROUP:], qa4[:, A_GROUP:]], axis=-1)], axis=1)
            qa_bd = qa_bd.reshape(bs, A_HEADS * SUBLANES, LANES)
            qi4 = jnp.moveaxis(q_i.reshape(bs, ts, IDX_HEADS, IDX_DIM), 1, 2)
            qi32 = jnp.pad(qi4, ((0, 0), (0, 0), (0, SUBLANES - ts), (0, 0))).reshape(bs, IDX_HEADS * SUBLANES, IDX_DIM)
            w4 = jnp.moveaxis(w_i[:, :IDX_HEADS].reshape(bs, ts, IDX_HEADS), 1, 2) * (IDX_DIM ** -0.5 * IDX_HEADS ** -0.5)
            w32 = jnp.pad(w4, ((0, 0), (0, 0), (0, SUBLANES - ts))).reshape(bs, IDX_HEADS * SUBLANES, 1)
            w32 = jnp.broadcast_to(w32, (bs, IDX_HEADS * SUBLANES, PAGE))
            keys = _dsa_index_s(qi32, w32, s3(k_i), ca_kit, e, page_table, ts, pp)
            keys_t = jnp.transpose(keys, (1, 3, 0, 2)).reshape((npg + 1) * PAGE, bs * SUBLANES)
            thr, need = _select(keys_t, nsel_s)
            thr = jnp.broadcast_to(thr.reshape(bs, SUBLANES, 1), (bs, SUBLANES, PAGE))
            need = jnp.broadcast_to(need.reshape(bs, SUBLANES, 1), (bs, SUBLANES, PAGE))
            o_a = _dsa_attn_s(qa_bd, s3(k_a), s3(v_a), keys, thr, need, ca_kt, ca_vt,
                              e, page_table, bias_a_s, ts, pp)
            o_b = _diff_s(s3(q_b), k_b.reshape(bs, ts * B_HEADS, 2 * HEAD_DIM), v_b.reshape(bs, ts * B_HEADS, B_VDIM),
                          cb_k, cb_v, e, page_table, bias_b_s, bias_b_new, lamp, subln, lam_init, pp)
            xs = _proj_post(xs, norm_mix_post[l], [o_a.reshape(ms, -1), o_b.reshape(ms, -1)], wo_parts, tm_s)
            even_rows_s.append((k_a.reshape(bs, ts, A_KV_HEADS, HEAD_DIM), v_a.reshape(bs, ts, A_KV_HEADS, HEAD_DIM),
                                k_i.reshape(bs, ts, IDX_DIM), k_b.reshape(bs, ts, B_HEADS, 2 * HEAD_DIM),
                                v_b.reshape(bs, ts, B_HEADS, B_VDIM)))
        else:
            o = l // 2
            ws = _split(w_in_odd[o], ODD_SPLIT)
            ws[3] = _pad_cols(ws[3], LANES)
            ws = [w.astype(BF16) for w in ws]
            bfp = _pad_cols(b_forget[o].reshape(1, C_HEADS), LANES)
            wo = [w_out_odd[o].astype(BF16)]

            wq, wk, wv, wf = ws
            items = [(wq, "f32"), (wk, "bf16"), (wk.T, "col"), (wv.T, "col"), (wf, "f32")]
            q, k16, k_t, v_t, lf = _rms_proj(xp, norm_mix_pre[l], items, tm_p, logsig_bias=bfp, seq=(bp, tp))
            r3 = lambda a: a.reshape(bp, tp, a.shape[1])
            cum, cumt = _cumsum(r3(lf), tb_attn)
            o_c = _flash2("fox", r3(q), r3(k16), v_t, (cum, cumt), tb_attn)
            xp = _proj_post(xp, norm_mix_post[l], [o_c.reshape(mp, -1)], wo, tm_p)
            kv_rows = lambda a: jnp.transpose(a.reshape(bp, C_HEADS, HEAD_DIM, tp), (0, 3, 1, 2))
            odd_rows_p.append((kv_rows(k_t), kv_rows(v_t), lf[:, :C_HEADS].reshape(bp, tp, C_HEADS)))

            q, k, v, lf = _rms_proj(xs, norm_mix_pre[l], [(w, "f32") for w in ws], tm_s, logsig_bias=bfp)
            s3 = lambda a: a.reshape(bs, ts, a.shape[1])
            o_c = _fox_s(s3(q), s3(k), s3(v), s3(lf), cc_kt, cc_vt, cc_lft, o, page_table, pp)
            xs = _proj_post(xs, norm_mix_post[l], [o_c.reshape(ms, -1)], wo, tm_s)
            odd_rows_s.append((k.reshape(bs, ts, C_HEADS, HEAD_DIM), v.reshape(bs, ts, C_HEADS, HEAD_DIM),
                               lf[:, :C_HEADS].reshape(bs, ts, C_HEADS)))

        wgu = w_gate_up[l].astype(BF16)
        wg, wu = wgu[:, :dff], wgu[:, dff:]
        wd = w_down[l].astype(BF16)
        tiles = tp // tm_p
        xp, tail = _ffn(xp, norm_ffn_pre[l], norm_ffn_post[l], wg, wu, w_conv[l], b_conv[l], wd,
                        tm_p, tf, tiles)
        conv_p.append(tail.reshape(bp, tiles, 8, dff)[:, tiles - 1, 8 - (CONV_W - 1):, :])
        st = state_conv[l]
        prev1 = jnp.concatenate([st[:, 1:2], jnp.zeros((bs, ts - 1, dff), F32)], axis=1).reshape(ms, dff)
        prev2 = jnp.concatenate([st, jnp.zeros((bs, ts - 2, dff), F32)], axis=1).reshape(ms, dff)
        xs, gfull = _ffn(xs, norm_ffn_pre[l], norm_ffn_post[l], wg, wu, w_conv[l], b_conv[l], wd,
                         tm_s, tf, ts, prev=(prev1, prev2))
        conv_s.append(gfull.reshape(bs, ts, dff)[:, ts - (CONV_W - 1):, :])

    def stack(rows, i):
        return jnp.stack([r[i] for r in rows])

    ev_p = [stack(even_rows_p, i) for i in range(5)]
    ev_s = [stack(even_rows_s, i) for i in range(5)]
    od_p = [stack(odd_rows_p, i) for i in range(3)]
    od_s = [stack(odd_rows_s, i) for i in range(3)]
    return (xp.reshape(bp, tp, d), xs.reshape(bs, ts, d),
            ev_p[0], ev_s[0], ev_p[1], ev_s[1], ev_p[2], ev_s[2], ev_p[3], ev_s[3], ev_p[4], ev_s[4],
            od_p[0], od_s[0], od_p[1], od_s[1], od_p[2], od_s[2],
            jnp.stack(conv_p), jnp.stack(conv_s))
```

```python
import functools
import math

import jax
import jax.numpy as jnp
import numpy as np
from jax import lax
from jax.experimental import pallas as pl
from jax.experimental.pallas import tpu as pltpu

F32 = jnp.float32
BF16 = jnp.bfloat16
HIGHEST = lax.Precision.HIGHEST

D_MODEL = 1024
HEAD_DIM = 64
A_HEADS = 8
A_KV_HEADS = 2
A_GROUP = A_HEADS // A_KV_HEADS
IDX_HEADS = 4
IDX_DIM = 64
TOPK_MAX = 256
B_HEADS = 4
B_VDIM = 128
C_HEADS = 16
N_BUCKETS = 32
MAX_DISTANCE = 128
D_FF = 2816
CONV_W = 3
EPS = 1e-6
PAGE = 128

LANES = 128
SUBLANES = 8
NEG = -1e30
INT_MIN = -2147483648
NEG_INF_CODE = 0x007FFFFF
VMEM_LIMIT = 48 * 1024 * 1024

EVEN_SPLIT = (512, 128, 128, 256, 64, 4, 512, 512, 512)
ODD_SPLIT = (1024, 1024, 1024, 16)

NT_DIMS = (((1,), (1,)), ((), ()))


def _cparams(sem):
    return pltpu.CompilerParams(dimension_semantics=sem, vmem_limit_bytes=VMEM_LIMIT)


def _rms(x, g):
    return x * lax.rsqrt(jnp.mean(x * x, axis=-1, keepdims=True) + EPS) * g


def _log_sigmoid(x):
    return -(jnp.maximum(-x, 0.0) + jnp.log1p(jnp.exp(-jnp.abs(x))))


def _gelu_tanh(x):
    c = math.sqrt(2.0 / math.pi)
    return x * (0.5 * (1.0 + jnp.tanh(c * (x + 0.044715 * (x * x * x)))))


def _dot(a, b):
    return jnp.dot(a, b, preferred_element_type=F32)


def _dot_nt(a, b):
    return lax.dot_general(a, b, NT_DIMS, preferred_element_type=F32)


def _dot_hi(a, b):
    return jnp.dot(a, b, preferred_element_type=F32, precision=HIGHEST)


def _softmax_rows(z, m_old, l_old, acc_old, pv):
    m_new = jnp.maximum(m_old, jnp.max(z, axis=1, keepdims=True))
    alpha = jnp.exp(m_old - m_new)
    p = jnp.exp(z - m_new)
    l_new = alpha * l_old + jnp.sum(p, axis=1, keepdims=True)
    acc_new = alpha * acc_old + pv(p.astype(BF16))
    return m_new, l_new, acc_new


def _softmax_cols(zt, m_old, l_old, acc_old, vt, shift=None):
    m_tile = jnp.max(zt, axis=0, keepdims=True)
    if shift is None:
        m_new = jnp.maximum(m_old, m_tile)
        p = jnp.exp(zt - m_new)
    else:
        m_new = jnp.maximum(m_old, m_tile + shift)
        p = jnp.exp(zt - (m_new - shift))
    alpha = jnp.exp(m_old - m_new)
    l_new = alpha * l_old + jnp.sum(p, axis=0, keepdims=True)
    acc_new = alpha * acc_old + _dot(vt, p.astype(BF16))
    return m_new, l_new, acc_new


def _ordered_value(code):
    key = code ^ jnp.int32(INT_MIN)
    return pltpu.bitcast(jnp.where(key < 0, key ^ jnp.int32(0x7FFFFFFF), key), F32)


def _kth_largest(count_ge, n_sel, shape):
    def below_neg_inf(code):
        return jnp.logical_and(code >= 0, code < jnp.int32(NEG_INF_CODE))

    def step(it, code):
        trial = code | lax.shift_left(jnp.int32(1), jnp.int32(31) - it)
        ok = jnp.logical_or(below_neg_inf(trial), count_ge(_ordered_value(trial)) >= n_sel)
        return jnp.where(ok, trial, code)

    code = lax.fori_loop(0, 32, step, jnp.zeros(shape, jnp.int32))
    return _ordered_value(jnp.where(below_neg_inf(code), jnp.int32(NEG_INF_CODE), code))


def _rms_proj_kernel(x_ref, g_ref, *refs, kinds, logsig_last):
    n_w = len(kinds)
    w_refs = refs[:n_w]
    if logsig_last:
        bias_ref = refs[n_w]
        o_refs = refs[n_w + 1:]
    else:
        o_refs = refs[n_w:]
    h = _rms(x_ref[...], g_ref[...]).astype(BF16)
    for idx, kind in enumerate(kinds):
        if kind == "col":
            o_refs[idx][0] = _dot_nt(w_refs[idx][...], h)
            continue
        y = _dot(h, w_refs[idx][...])
        if logsig_last and idx == n_w - 1:
            y = _log_sigmoid(y + bias_ref[...])
        o_refs[idx][...] = y.astype(o_refs[idx].dtype)


def _rms_proj(x2d, g, items, tm, logsig_bias=None, seq=None):
    m, d = x2d.shape
    kinds = tuple(kind for _, kind in items)
    in_specs = [pl.BlockSpec((tm, d), lambda i: (i, 0)), pl.BlockSpec((1, d), lambda i: (0, 0))]
    in_specs += [pl.BlockSpec(w.shape, lambda i: (0, 0)) for w, _ in items]
    args = [x2d, g.reshape(1, d)] + [w for w, _ in items]
    if logsig_bias is not None:
        in_specs.append(pl.BlockSpec(logsig_bias.shape, lambda i: (0, 0)))
        args.append(logsig_bias)
    out_shape, out_specs = [], []
    for w, kind in items:
        if kind == "col":
            nb, t = seq
            tiles = t // tm
            out_shape.append(jax.ShapeDtypeStruct((nb, w.shape[0], t), F32))
            out_specs.append(pl.BlockSpec((1, w.shape[0], tm), lambda i, tiles=tiles: (i // tiles, 0, i % tiles)))
        else:
            out_shape.append(jax.ShapeDtypeStruct((m, w.shape[1]), F32 if kind == "f32" else BF16))
            out_specs.append(pl.BlockSpec((tm, w.shape[1]), lambda i: (i, 0)))
    return pl.pallas_call(
        functools.partial(_rms_proj_kernel, kinds=kinds, logsig_last=logsig_bias is not None),
        grid=(m // tm,), in_specs=in_specs, out_specs=out_specs, out_shape=out_shape,
        compiler_params=_cparams(("parallel",)), name="rms_proj")(*args)


def _proj_post_kernel(x_ref, g_ref, *refs, n_in):
    o_refs = refs[:n_in]
    w_refs = refs[n_in:2 * n_in]
    out_ref = refs[2 * n_in]
    acc = None
    for o, w in zip(o_refs, w_refs):
        t = _dot(o[...].astype(BF16), w[...])
        acc = t if acc is None else acc + t
    out_ref[...] = x_ref[...] + _rms(acc, g_ref[...])


def _proj_post(x2d, g, o_list, w_list, tm):
    m, d = x2d.shape
    n_in = len(o_list)
    in_specs = [pl.BlockSpec((tm, d), lambda i: (i, 0)), pl.BlockSpec((1, d), lambda i: (0, 0))]
    in_specs += [pl.BlockSpec((tm, o.shape[1]), lambda i: (i, 0)) for o in o_list]
    in_specs += [pl.BlockSpec(w.shape, lambda i: (0, 0)) for w in w_list]
    return pl.pallas_call(
        functools.partial(_proj_post_kernel, n_in=n_in),
        grid=(m // tm,), in_specs=in_specs, out_specs=pl.BlockSpec((tm, d), lambda i: (i, 0)),
        out_shape=jax.ShapeDtypeStruct((m, d), F32),
        compiler_params=_cparams(("parallel",)), name="proj_post")(x2d, g.reshape(1, d), *o_list, *w_list)


def _ffn_kernel(*refs, tm, tiles_per_seq, sample, sub):
    if sample:
        (x_ref, gpre_ref, gpost_ref, wg_ref, wu_ref, wc_ref, bc_ref, wd_ref, p1_ref, p2_ref,
         y_ref, gout_ref, h_ref, acc_ref, gs_ref) = refs
    else:
        (x_ref, gpre_ref, gpost_ref, wg_ref, wu_ref, wc_ref, bc_ref, wd_ref,
         y_ref, gout_ref, h_ref, acc_ref, gs_ref, halo_ref) = refs
    i = pl.program_id(0)
    c = pl.program_id(1)

    @pl.when(c == 0)
    def _():
        h_ref[...] = _rms(x_ref[...], gpre_ref[...]).astype(BF16)
        acc_ref[...] = jnp.zeros_like(acc_ref)

    h = h_ref[...]
    tf = wg_ref.shape[1]
    if sample:
        gs_ref[0:8, :] = jnp.zeros((8, tf), F32)
    else:
        first = (i % tiles_per_seq) == 0

        @pl.when(first)
        def _():
            gs_ref[0:8, :] = jnp.zeros((8, tf), F32)

        @pl.when(jnp.logical_not(first))
        def _():
            gs_ref[0:8, :] = halo_ref[c]

    if sample:
        t = lax.broadcasted_iota(jnp.int32, (tm, 1), 0) % tiles_per_seq
    part = None
    for lo in range(0, tf, sub):
        hi = min(lo + sub, tf)
        g = _dot(h, wg_ref[:, lo:hi])
        u = _dot(h, wu_ref[:, lo:hi])
        gs_ref[8:tm + 8, lo:hi] = g
        g1 = gs_ref[7:tm + 7, lo:hi]
        g2 = gs_ref[6:tm + 6, lo:hi]
        if sample:
            g1 = jnp.where(t >= 1, g1, 0.0) + p1_ref[:, lo:hi]
            g2 = jnp.where(t >= 2, g2, 0.0) + p2_ref[:, lo:hi]
            gout_ref[:, lo:hi] = g
        w = wc_ref[:, lo:hi]
        gc = bc_ref[:, lo:hi] + w[0:1, :] * g2
        gc = gc + w[1:2, :] * g1
        gc = gc + w[2:3, :] * g
        a = _gelu_tanh(gc) * u
        d = _dot(a.astype(BF16), wd_ref[lo:hi, :])
        part = d if part is None else part + d
    acc_ref[...] += part
    if not sample:
        tail = gs_ref[tm:tm + 8, :]
        halo_ref[c] = tail
        gout_ref[0] = tail

    @pl.when(c == pl.num_programs(1) - 1)
    def _():
        y_ref[...] = x_ref[...] + _rms(acc_ref[...], gpost_ref[...])


def _ffn(x2d, gpre, gpost, wg, wu, wc, bc, wd, tm, tf, tiles_per_seq, prev=None):
    m, d = x2d.shape
    dff = wg.shape[1]
    nc = dff // tf
    sample = prev is not None
    wmode = pl.Buffered(1) if nc == 1 else None
    in_specs = [
        pl.BlockSpec((tm, d), lambda i, c: (i, 0)),
        pl.BlockSpec((1, d), lambda i, c: (0, 0)),
        pl.BlockSpec((1, d), lambda i, c: (0, 0)),
        pl.BlockSpec((d, tf), lambda i, c: (0, c), pipeline_mode=wmode),
        pl.BlockSpec((d, tf), lambda i, c: (0, c), pipeline_mode=wmode),
        pl.BlockSpec((CONV_W, tf), lambda i, c: (0, c)),
        pl.BlockSpec((1, tf), lambda i, c: (0, c)),
        pl.BlockSpec((tf, d), lambda i, c: (c, 0), pipeline_mode=wmode),
    ]
    args = [x2d, gpre.reshape(1, d), gpost.reshape(1, d), wg, wu, wc, bc.reshape(1, dff), wd]
    scratch = [pltpu.VMEM((tm, d), BF16), pltpu.VMEM((tm, d), F32), pltpu.VMEM((tm + 8, tf), F32)]
    if sample:
        in_specs += [pl.BlockSpec((tm, tf), lambda i, c: (i, c))] * 2
        args += list(prev)
        gout_shape = jax.ShapeDtypeStruct((m, dff), F32)
        gout_spec = pl.BlockSpec((tm, tf), lambda i, c: (i, c))
    else:
        gout_shape = jax.ShapeDtypeStruct((m // tm, 8, dff), F32)
        gout_spec = pl.BlockSpec((1, 8, tf), lambda i, c: (i, 0, c))
        scratch.append(pltpu.VMEM((nc, 8, tf), F32))
    return pl.pallas_call(
        functools.partial(_ffn_kernel, tm=tm, tiles_per_seq=tiles_per_seq, sample=sample, sub=2 * LANES),
        grid=(m // tm, nc), in_specs=in_specs,
        out_specs=[pl.BlockSpec((tm, d), lambda i, c: (i, 0)), gout_spec],
        out_shape=[jax.ShapeDtypeStruct((m, d), F32), gout_shape],
        scratch_shapes=scratch,
        compiler_params=_cparams(("arbitrary", "arbitrary")), name="conv_ffn")(*args)


def _cumsum_kernel(lf_ref, cum_ref, cumt_ref, *, tb, nchunk):
    row = lax.broadcasted_iota(jnp.int32, (tb, tb), 0)
    col = lax.broadcasted_iota(jnp.int32, (tb, tb), 1)
    lower = jnp.where(col <= row, 1.0, 0.0).astype(F32)
    carry = jnp.zeros((1, LANES), F32)
    for c in range(nchunk):
        x = lf_ref[0, c * tb:(c + 1) * tb, :]
        ct = _dot_hi(lower, x) + carry
        cumt_ref[0, c * tb:(c + 1) * tb, :] = ct
        carry = ct[tb - 1:tb, :]
        cum_ref[0, c] = ct.T[0:C_HEADS, :]


def _cumsum(lf3d, tb):
    b, t, _ = lf3d.shape
    nchunk = t // tb
    return pl.pallas_call(
        functools.partial(_cumsum_kernel, tb=tb, nchunk=nchunk),
        grid=(b,), in_specs=[pl.BlockSpec((1, t, LANES), lambda i: (i, 0, 0))],
        out_specs=[pl.BlockSpec((1, nchunk, C_HEADS, tb), lambda i: (i, 0, 0, 0)),
                   pl.BlockSpec((1, t, LANES), lambda i: (i, 0, 0))],
        out_shape=[jax.ShapeDtypeStruct((b, nchunk, C_HEADS, tb), F32),
                   jax.ShapeDtypeStruct((b, t, LANES), F32)],
        compiler_params=_cparams(("parallel",)), name="logf_cumsum")(lf3d)


def _flash2_kernel(*refs, mode, tb, nk, lam_init):
    if mode == "fox":
        (q_ref, k_ref, v_ref, cum_ref, cumt_ref, o_ref,
         kb_ref, vt_ref, m_ref, l_ref, acc_ref, ck_ref) = refs
    else:
        (q_ref, k_ref, v_ref, bias_ref, lamp_ref, subln_ref, o_ref,
         kb_ref, vt_ref, m_ref, l_ref, acc_ref) = refs
    g = pl.program_id(1)
    qi = pl.program_id(2)
    lane = lax.broadcasted_iota(jnp.int32, (tb, LANES), 1)
    krow = lax.broadcasted_iota(jnp.int32, (tb, tb), 0)
    qcol = lax.broadcasted_iota(jnp.int32, (tb, tb), 1)
    tri = krow <= qcol

    @pl.when(qi == 0)
    def _():
        if mode == "fox":
            for c in range(nk):
                vt_ref[c] = v_ref[0, :, c * tb:(c + 1) * tb].astype(BF16)
        else:
            kb_ref[...] = k_ref[0].astype(BF16)
            for c in range(nk):
                vt_ref[c] = v_ref[0, c * tb:(c + 1) * tb, :].T.astype(BF16)
        if mode == "fox":
            hrow = lax.broadcasted_iota(jnp.int32, (LANES, LANES), 0)
            for a in range(2):
                onehot = jnp.where(hrow == 2 * g + a, 1.0, 0.0).astype(F32)
                ck_ref[a] = _dot_hi(cumt_ref[0], onehot)

    q = q_ref[0] * (HEAD_DIM ** -0.5)
    qall = jnp.concatenate([jnp.where(lane < HEAD_DIM, q, 0.0), jnp.where(lane >= HEAD_DIM, q, 0.0)],
                           axis=0).astype(BF16)
    if mode == "fox":
        cq = jnp.concatenate([cum_ref[0, qi, pl.ds(2 * g + a, 1), :] for a in range(2)], axis=1)
    m_ref[...] = jnp.full(m_ref.shape, NEG, F32)
    l_ref[...] = jnp.zeros_like(l_ref)
    acc_ref[...] = jnp.zeros_like(acc_ref)
    rep = tb // LANES

    def chunk(c, diagonal):
        off = pl.multiple_of(c * tb, tb)
        if mode == "fox":
            zt = _dot_nt(k_ref[0, pl.ds(off, tb), :], qall)
            ck0 = ck_ref[0, pl.ds(off, tb), :]
            ck1 = ck_ref[1, pl.ds(off, tb), :]
            zt = zt - jnp.concatenate([ck0] * rep + [ck1] * rep, axis=1)
            shift = cq
        else:
            zt = _dot_nt(kb_ref[pl.ds(off, tb), :], qall)
            bt = bias_ref[0, jnp.minimum(qi - c, 2)]
            zt = zt + jnp.concatenate([bt, bt], axis=1)
            shift = None
        if diagonal:
            zt = jnp.where(jnp.concatenate([tri, tri], axis=1), zt, NEG)
        m_new, l_new, acc_new = _softmax_cols(zt, m_ref[...], l_ref[...], acc_ref[...], vt_ref[c], shift)
        m_ref[...] = m_new
        l_ref[...] = l_new
        acc_ref[...] = acc_new

    def body(c, carry):
        chunk(c, False)
        return carry

    lax.fori_loop(0, qi, body, 0)
    chunk(qi, True)
    o = acc_ref[...] / l_ref[...]
    o0 = o[:, 0:tb].T
    o1 = o[:, tb:2 * tb].T
    if mode == "fox":
        o_ref[0] = jnp.where(lane < HEAD_DIM, o0, o1)
    else:
        lp = lamp_ref[...]
        lam = (jnp.exp(jnp.sum(lp[0:1] * lp[1:2], axis=1, keepdims=True))
               - jnp.exp(jnp.sum(lp[2:3] * lp[3:4], axis=1, keepdims=True)) + lam_init)
        o_ref[0] = _rms(o0 - lam * o1, subln_ref[...]) * (1.0 - lam_init)


def _flash2(mode, q, k, v, extra, tb, lam_init=0.0):
    b, t, width = q.shape
    ng = width // LANES
    nq = t // tb
    in_specs = [pl.BlockSpec((1, tb, LANES), lambda i, g, j: (i, j, g)),
                pl.BlockSpec((1, t, LANES), lambda i, g, j: (i, 0, g)),
                pl.BlockSpec((1, t, LANES), lambda i, g, j: (i, 0, g))]
    scratch = [pltpu.VMEM((t, LANES), BF16), pltpu.VMEM((nq, LANES, tb), BF16),
               pltpu.VMEM((1, 2 * tb), F32), pltpu.VMEM((1, 2 * tb), F32), pltpu.VMEM((LANES, 2 * tb), F32)]
    if mode == "fox":
        cum, cumt = extra
        in_specs[2] = pl.BlockSpec((1, LANES, t), lambda i, g, j: (i, g, 0))
        in_specs += [pl.BlockSpec((1, nq, C_HEADS, tb), lambda i, g, j: (i, 0, 0, 0)),
                     pl.BlockSpec((1, t, LANES), lambda i, g, j: (i, 0, 0))]
        scratch.append(pltpu.VMEM((2, t, LANES), F32))
    else:
        bias, lamp, subln = extra
        in_specs += [pl.BlockSpec((1, 3, tb, tb), lambda i, g, j: (g, 0, 0, 0)),
                     pl.BlockSpec(lamp.shape, lambda i, g, j: (0, 0)),
                     pl.BlockSpec(subln.shape, lambda i, g, j: (0, 0))]
    return pl.pallas_call(
        functools.partial(_flash2_kernel, mode=mode, tb=tb, nk=nq, lam_init=lam_init),
        grid=(b, ng, nq), in_specs=in_specs,
        out_specs=pl.BlockSpec((1, tb, LANES), lambda i, g, j: (i, j, g)),
        out_shape=jax.ShapeDtypeStruct((b, t, width), F32),
        scratch_shapes=scratch,
        compiler_params=_cparams(("arbitrary", "arbitrary", "arbitrary")),
        name="attn_" + mode)(q, k, v, *extra)


def _dsa_p_kernel(qa_ref, ka_ref, vat_in_ref, qi_ref, ki_ref, wi_ref, bias_ref, o_ref,
                  vat_ref, key_ref, qm_ref, m_ref, l_ref, acc_ref, *, n_sel, nk):
    tb = LANES
    qi = pl.program_id(1)
    krow = lax.broadcasted_iota(jnp.int32, (tb, tb), 0)
    qcol = lax.broadcasted_iota(jnp.int32, (tb, tb), 1)
    tri = krow <= qcol

    @pl.when(qi == 0)
    def _():
        for c in range(nk // 2):
            vat_ref[c] = vat_in_ref[0, :, c * 2 * tb:(c + 1) * 2 * tb].astype(BF16)

    qidx = qi_ref[0]
    qh = jnp.concatenate([qidx[:, h * IDX_DIM:(h + 1) * IDX_DIM] for h in range(IDX_HEADS)],
                         axis=0).astype(BF16)
    wt = (wi_ref[0] * (IDX_DIM ** -0.5 * IDX_HEADS ** -0.5)).T
    wall = jnp.concatenate([wt[h:h + 1, :] for h in range(IDX_HEADS)], axis=1)

    npairs = (qi + 2) // 2

    def chunk_valid(c):
        return jnp.logical_or(c < qi, jnp.logical_and(c == qi, tri))

    def p1(p, carry):
        off = pl.multiple_of(p * 2 * tb, 2 * tb)
        sh = jnp.maximum(_dot_nt(ki_ref[0, pl.ds(off, 2 * tb), :], qh), 0.0) * wall
        s = sh[:, 0:tb]
        for h in range(1, IDX_HEADS):
            s = s + sh[:, h * tb:(h + 1) * tb]
        valid = jnp.concatenate([chunk_valid(2 * p), chunk_valid(2 * p + 1)], axis=0)
        key_ref[pl.ds(2 * p, 2)] = jnp.where(valid, s, -jnp.inf).reshape(2, tb, tb)
        return carry

    lax.fori_loop(0, npairs, p1, 0)

    def count(pred):
        def cb(p, acc):
            hit = jnp.where(pred(key_ref[pl.ds(2 * p, 2)]), 1.0, 0.0)
            return acc + (hit[0] + hit[1])
        acc = lax.fori_loop(0, npairs, cb, jnp.zeros((tb, tb), F32))
        return jnp.sum(acc, axis=0, keepdims=True)

    thr = _kth_largest(lambda trial: count(lambda kk: kk >= trial), n_sel, (1, tb))
    need = n_sel - count(lambda kk: kk > thr)
    lstrict = jnp.where(qcol < krow, 1.0, 0.0).astype(BF16)

    lane = qcol
    qa = qa_ref[0] * (HEAD_DIM ** -0.5)
    for j in range(A_HEADS):
        blk = qa[:, (j // 2) * LANES:(j // 2 + 1) * LANES]
        grp = j // A_GROUP
        if j % 2 != grp:
            blk = pltpu.roll(blk, HEAD_DIM, axis=1)
        keep = (lane < HEAD_DIM) if grp == 0 else (lane >= HEAD_DIM)
        qm_ref[j] = jnp.where(keep, blk, 0.0).astype(BF16)
    m_ref[...] = jnp.full(m_ref.shape, NEG, F32)
    l_ref[...] = jnp.zeros_like(l_ref)
    acc_ref[...] = jnp.zeros_like(acc_ref)

    qall = qm_ref[...].reshape(A_HEADS * tb, LANES)

    def chunk_mask(c, carry):
        kk = key_ref[c]
        eq = kk == thr
        eqf = jnp.where(eq, 1.0, 0.0)
        prefix = _dot(lstrict, eqf.astype(BF16)) + carry
        sel = jnp.logical_or(kk > thr, jnp.logical_and(eq, prefix < need))
        sel = jnp.logical_and(sel, chunk_valid(c))
        am = jnp.where(sel, 0.0, NEG)
        return jnp.concatenate([am] * A_HEADS, axis=1), carry + jnp.sum(eqf, axis=0, keepdims=True)

    def p4(p, carry):
        am_a, carry = chunk_mask(2 * p, carry)
        am_b, carry = chunk_mask(2 * p + 1, carry)
        bias = jnp.concatenate([bias_ref[jnp.clip(qi - 2 * p, 0, 2)], bias_ref[jnp.clip(qi - 2 * p - 1, 0, 2)]],
                               axis=0)
        off = pl.multiple_of(p * 2 * tb, 2 * tb)
        zt = (_dot_nt(ka_ref[0, pl.ds(off, 2 * tb), :], qall) + bias
              + jnp.concatenate([am_a, am_b], axis=0))
        m_new, l_new, acc_new = _softmax_cols(zt, m_ref[...], l_ref[...], acc_ref[...], vat_ref[p])
        m_ref[...] = m_new
        l_ref[...] = l_new
        acc_ref[...] = acc_new
        return carry

    lax.fori_loop(0, npairs, p4, jnp.zeros((1, tb), F32))
    oall = acc_ref[...] / l_ref[...]
    for c2 in range(A_HEADS // 2):
        grp = (2 * c2) // A_GROUP
        lo = oall[:, (2 * c2) * tb:(2 * c2 + 1) * tb].T
        hi = oall[:, (2 * c2 + 1) * tb:(2 * c2 + 2) * tb].T
        if grp == 0:
            hi = pltpu.roll(hi, HEAD_DIM, axis=1)
        else:
            lo = pltpu.roll(lo, HEAD_DIM, axis=1)
        o_ref[0, :, c2 * LANES:(c2 + 1) * LANES] = jnp.where(lane < HEAD_DIM, lo, hi)


def _dsa_p(q_a, k_a, v_at, q_i, k_i, w_i, bias, n_sel):
    b, t, _ = q_a.shape
    tb = LANES
    nq = t // tb
    in_specs = [pl.BlockSpec((1, tb, q_a.shape[2]), lambda i, j: (i, j, 0)),
                pl.BlockSpec((1, t, LANES), lambda i, j: (i, 0, 0)),
                pl.BlockSpec((1, LANES, t), lambda i, j: (i, 0, 0)),
                pl.BlockSpec((1, tb, q_i.shape[2]), lambda i, j: (i, j, 0)),
                pl.BlockSpec((1, t, IDX_DIM), lambda i, j: (i, 0, 0)),
                pl.BlockSpec((1, tb, LANES), lambda i, j: (i, j, 0)),
                pl.BlockSpec(bias.shape, lambda i, j: (0, 0, 0))]
    return pl.pallas_call(
        functools.partial(_dsa_p_kernel, n_sel=n_sel, nk=nq),
        grid=(b, nq), in_specs=in_specs,
        out_specs=pl.BlockSpec((1, tb, q_a.shape[2]), lambda i, j: (i, j, 0)),
        out_shape=jax.ShapeDtypeStruct(q_a.shape, F32),
        scratch_shapes=[pltpu.VMEM((nq // 2, LANES, 2 * tb), BF16),
                        pltpu.VMEM((nq, tb, tb), F32),
                        pltpu.VMEM((A_HEADS, tb, LANES), BF16),
                        pltpu.VMEM((1, A_HEADS * tb), F32), pltpu.VMEM((1, A_HEADS * tb), F32),
                        pltpu.VMEM((LANES, A_HEADS * tb), F32)],
        compiler_params=_cparams(("arbitrary", "arbitrary")), name="attn_dsa")(q_a, k_a, v_at, q_i, k_i, w_i, bias)


def _fox_s_kernel(pt_ref, q_ref, kn_ref, vn_ref, lfn_ref, *refs, ts, pp):
    kt_refs = refs[:pp]
    vt_refs = refs[pp:2 * pp]
    lft_refs = refs[2 * pp:3 * pp]
    o_ref = refs[3 * pp]
    qbd_ref, kpad_ref, vpad_ref, lfpad_ref, m_ref, l_ref, acc_ref, carry_ref, cq_ref = refs[3 * pp + 1:]
    b = pl.program_id(0)
    s = pl.program_id(1)
    nrow = ts * C_HEADS
    width = C_HEADS * HEAD_DIM
    row = lax.broadcasted_iota(jnp.int32, (PAGE, PAGE), 0)
    col = lax.broadcasted_iota(jnp.int32, (PAGE, PAGE), 1)

    @pl.when(jnp.logical_and(b == 0, s == 0))
    def _():
        kpad_ref[...] = jnp.zeros_like(kpad_ref)
        vpad_ref[...] = jnp.zeros_like(vpad_ref)

    def update(z, pv):
        m_new, l_new, acc_new = _softmax_rows(z, m_ref[...], l_ref[...], acc_ref[...], pv)
        m_ref[...] = m_new
        l_ref[...] = l_new
        acc_ref[...] = acc_new

    @pl.when(s == 0)
    def _():
        hmask = (lax.broadcasted_iota(jnp.int32, (C_HEADS, width), 1) // HEAD_DIM
                 == lax.broadcasted_iota(jnp.int32, (C_HEADS, width), 0))
        q = q_ref[0] * (HEAD_DIM ** -0.5)
        for i in range(ts):
            qbd_ref[i * C_HEADS:(i + 1) * C_HEADS, :] = jnp.where(
                hmask, jnp.broadcast_to(q[i:i + 1, :], (C_HEADS, width)), 0.0).astype(BF16)
        kpad_ref[0:ts, :] = kn_ref[0]
        vpad_ref[0:ts, :] = vn_ref[0]
        lfpad_ref[...] = jnp.zeros_like(lfpad_ref)
        lfpad_ref[0:ts, :] = lfn_ref[0]
        m_ref[...] = jnp.full(m_ref.shape, NEG, F32)
        l_ref[...] = jnp.zeros_like(l_ref)
        acc_ref[...] = jnp.zeros_like(acc_ref)
        carry_ref[...] = jnp.zeros_like(carry_ref)
        lft = lfpad_ref[...].T[0:C_HEADS, :]
        incl = jnp.where(row <= col, 1.0, 0.0).astype(F32)
        cnew = _dot_hi(lft, incl)
        for i in range(ts):
            cq_ref[i * C_HEADS:(i + 1) * C_HEADS, :] = jnp.broadcast_to(cnew[:, i:i + 1], (C_HEADS, PAGE))
        z = _dot_nt(qbd_ref[...], kpad_ref[...].astype(BF16))
        z = z + cq_ref[...] - jnp.concatenate([cnew] * ts, axis=0)
        rr = lax.broadcasted_iota(jnp.int32, (nrow, PAGE), 0)
        cc = lax.broadcasted_iota(jnp.int32, (nrow, PAGE), 1)
        z = jnp.where(cc * C_HEADS <= rr, z, NEG)
        vnew = vpad_ref[...].astype(BF16)
        update(z, lambda p: _dot(p, vnew))

    later = jnp.where(row > col, 1.0, 0.0).astype(F32)
    carry = carry_ref[...]
    sufs = []
    for k in range(pp):
        lft = lft_refs[k][...]
        sufs.append(_dot_hi(lft, later) + carry)
        carry = carry + jnp.sum(lft, axis=1, keepdims=True)
    carry_ref[...] = carry
    suf = jnp.concatenate(sufs, axis=1)
    kt = jnp.concatenate([kt_refs[k][...].astype(BF16) for k in range(pp)], axis=1)
    vt = jnp.concatenate([vt_refs[k][...].astype(BF16) for k in range(pp)], axis=1)
    z = _dot(qbd_ref[...], kt)
    z = z + jnp.concatenate([cq_ref[...]] * pp, axis=1) + jnp.concatenate([suf] * ts, axis=0)
    update(z, lambda p: _dot_nt(p, vt))

    @pl.when(s == pl.num_programs(1) - 1)
    def _():
        hmask = (lax.broadcasted_iota(jnp.int32, (C_HEADS, width), 1) // HEAD_DIM
                 == lax.broadcasted_iota(jnp.int32, (C_HEADS, width), 0))
        o = acc_ref[...] / l_ref[...]
        for i in range(ts):
            blk = jnp.where(hmask, o[i * C_HEADS:(i + 1) * C_HEADS, :], 0.0)
            o_ref[0, i:i + 1, :] = jnp.sum(blk, axis=0, keepdims=True)


def _fox_s(q, kn, vn, lfn, cache_kt, cache_vt, cache_lft, layer, page_table, pp):
    b, ts, width = q.shape
    npg = page_table.shape[1]
    nrow = ts * C_HEADS

    def page_idx(k):
        return lambda i, s, pt: (layer, pt[i, npg - 1 - (s * pp + k)], 0, 0)

    def seq_idx(i, s, pt):
        return (i, 0, 0)

    in_specs = [pl.BlockSpec((1, ts, width), seq_idx), pl.BlockSpec((1, ts, width), seq_idx),
                pl.BlockSpec((1, ts, width), seq_idx), pl.BlockSpec((1, ts, LANES), seq_idx)]
    in_specs += [pl.BlockSpec((None, None, width, PAGE), page_idx(k)) for k in range(pp)]
    in_specs += [pl.BlockSpec((None, None, width, PAGE), page_idx(k)) for k in range(pp)]
    in_specs += [pl.BlockSpec((None, None, C_HEADS, PAGE), page_idx(k)) for k in range(pp)]
    grid_spec = pltpu.PrefetchScalarGridSpec(
        num_scalar_prefetch=1, grid=(b, npg // pp), in_specs=in_specs,
        out_specs=pl.BlockSpec((1, ts, width), seq_idx),
        scratch_shapes=[pltpu.VMEM((nrow, width), BF16), pltpu.VMEM((PAGE, width), F32),
                        pltpu.VMEM((PAGE, width), F32), pltpu.VMEM((PAGE, LANES), F32),
                        pltpu.VMEM((nrow, 1), F32), pltpu.VMEM((nrow, 1), F32),
                        pltpu.VMEM((nrow, width), F32), pltpu.VMEM((C_HEADS, PAGE), F32),
                        pltpu.VMEM((nrow, PAGE), F32)])
    return pl.pallas_call(
        functools.partial(_fox_s_kernel, ts=ts, pp=pp), grid_spec=grid_spec,
        out_shape=jax.ShapeDtypeStruct((b, ts, width), F32),
        compiler_params=_cparams(("arbitrary", "arbitrary")), name="attn_fox_sample")(
            page_table, q, kn, vn, lfn, *([cache_kt] * pp), *([cache_vt] * pp), *([cache_lft] * pp))


def _diff_s_kernel(pt_ref, q_ref, kn_ref, vn_ref, *refs, ts, pp, npg, lam_init):
    k_refs = refs[:pp]
    v_refs = refs[pp:2 * pp]
    (bias_ref, biasn_ref, lamp_ref, subln_ref, o_ref,
     qx_ref, kpad_ref, vpad_ref, m_ref, l_ref, acc_ref) = refs[2 * pp:]
    b = pl.program_id(0)
    s = pl.program_id(1)
    hrows = B_HEADS * ts
    lane = lax.broadcasted_iota(jnp.int32, (ts, LANES), 1)

    @pl.when(jnp.logical_and(b == 0, s == 0))
    def _():
        kpad_ref[...] = jnp.zeros_like(kpad_ref)
        vpad_ref[...] = jnp.zeros_like(vpad_ref)

    @pl.when(s == 0)
    def _():
        q = q_ref[0] * (HEAD_DIM ** -0.5)
        for n in range(B_HEADS):
            blk = q[:, n * LANES:(n + 1) * LANES]
            qx_ref[n * ts:(n + 1) * ts, :] = jnp.where(lane < HEAD_DIM, blk, 0.0)
            qx_ref[hrows + n * ts:hrows + (n + 1) * ts, :] = jnp.where(lane >= HEAD_DIM, blk, 0.0)
        m_ref[...] = jnp.full(m_ref.shape, NEG, F32)
        l_ref[...] = jnp.zeros_like(l_ref)
        acc_ref[...] = jnp.zeros_like(acc_ref)

    def step(kall, vall, bias):
        z = _dot_nt(qx_ref[...].astype(BF16), kall) + bias
        m_new, l_new, acc_new = _softmax_rows(z, m_ref[...], l_ref[...], acc_ref[...], lambda p: _dot(p, vall))
        m_ref[...] = m_new
        l_ref[...] = l_new
        acc_ref[...] = acc_new

    step(jnp.concatenate([k_refs[k][...].astype(BF16) for k in range(pp)], axis=0),
         jnp.concatenate([v_refs[k][...].astype(BF16) for k in range(pp)], axis=0), bias_ref[s])

    @pl.when(s == pl.num_programs(1) - 1)
    def _():
        kpad_ref[0:ts * B_HEADS, :] = kn_ref[0]
        vpad_ref[0:ts * B_HEADS, :] = vn_ref[0]
        step(kpad_ref[...].astype(BF16), vpad_ref[...].astype(BF16), biasn_ref[...])
        lp = lamp_ref[...]
        lam = (jnp.exp(jnp.sum(lp[0:1] * lp[1:2], axis=1, keepdims=True))
               - jnp.exp(jnp.sum(lp[2:3] * lp[3:4], axis=1, keepdims=True)) + lam_init)
        o = acc_ref[...] / l_ref[...]
        od = o[0:hrows] - lam * o[hrows:2 * hrows]
        for n in range(B_HEADS):
            o_ref[0, :, n * B_VDIM:(n + 1) * B_VDIM] = (
                _rms(od[n * ts:(n + 1) * ts], subln_ref[...]) * (1.0 - lam_init))


def _diff_s(q, kn16, vn16, cache_k, cache_v, layer, page_table, bias, bias_new, lamp, subln, lam_init, pp):
    b, ts, width = q.shape
    npg = page_table.shape[1]
    nrow = 2 * B_HEADS * ts
    krows = PAGE * B_HEADS

    def page_idx(k):
        return lambda i, s, pt: (layer, pt[i, s * pp + k], 0, 0)

    def seq_idx(i, s, pt):
        return (i, 0, 0)

    in_specs = [pl.BlockSpec((1, ts, width), seq_idx), pl.BlockSpec((1,) + kn16.shape[1:], seq_idx),
                pl.BlockSpec((1,) + vn16.shape[1:], seq_idx)]
    in_specs += [pl.BlockSpec((None, None, krows, LANES), page_idx(k)) for k in range(pp)] * 2
    in_specs += [pl.BlockSpec(bias.shape, lambda i, s, pt: (0, 0, 0)),
                 pl.BlockSpec(bias_new.shape, lambda i, s, pt: (0, 0)),
                 pl.BlockSpec(lamp.shape, lambda i, s, pt: (0, 0)),
                 pl.BlockSpec(subln.shape, lambda i, s, pt: (0, 0))]
    grid_spec = pltpu.PrefetchScalarGridSpec(
        num_scalar_prefetch=1, grid=(b, npg // pp), in_specs=in_specs,
        out_specs=pl.BlockSpec((1, ts, width), seq_idx),
        scratch_shapes=[pltpu.VMEM((nrow, LANES), F32), pltpu.VMEM((krows, LANES), F32),
                        pltpu.VMEM((krows, LANES), F32), pltpu.VMEM((nrow, 1), F32),
                        pltpu.VMEM((nrow, 1), F32), pltpu.VMEM((nrow, LANES), F32)])
    return pl.pallas_call(
        functools.partial(_diff_s_kernel, ts=ts, pp=pp, npg=npg, lam_init=lam_init), grid_spec=grid_spec,
        out_shape=jax.ShapeDtypeStruct((b, ts, width), F32),
        compiler_params=_cparams(("arbitrary", "arbitrary")), name="attn_diff_sample")(
            page_table, q, kn16, vn16, *([cache_k] * pp), *([cache_v] * pp), bias, bias_new, lamp, subln)


def _dsa_index_s_kernel(pt_ref, qi_ref, wi_ref, kin_ref, *refs, ts, pp, nchunk):
    kit_refs = refs[:pp]
    key_ref, pad_ref = refs[pp:]
    b = pl.program_id(0)
    s = pl.program_id(1)
    last = nchunk - 1
    rpad = SUBLANES

    @pl.when(jnp.logical_and(b == 0, s == 0))
    def _():
        pad_ref[...] = jnp.zeros_like(pad_ref)

    def index_scores(kit):
        s32 = jnp.maximum(_dot(qi_ref[0].astype(BF16), kit.astype(BF16)), 0.0) * wi_ref[0]
        sc = s32[0:rpad]
        for h in range(1, IDX_HEADS):
            sc = sc + s32[h * rpad:(h + 1) * rpad]
        return sc

    for k in range(pp):
        key_ref[0, s * pp + k] = index_scores(kit_refs[k][...])

    @pl.when(s == pl.num_programs(1) - 1)
    def _():
        pad_ref[0:ts, 0:IDX_DIM] = kin_ref[0]
        rr = lax.broadcasted_iota(jnp.int32, (rpad, PAGE), 0)
        cc = lax.broadcasted_iota(jnp.int32, (rpad, PAGE), 1)
        causal = jnp.logical_and(cc <= rr, cc < ts)
        key_ref[0, last] = jnp.where(causal, index_scores(pad_ref[...].T[0:IDX_DIM, :]), -jnp.inf)


def _dsa_index_s(qi32, w32, kin, cache_kit, layer, page_table, ts, pp):
    b = qi32.shape[0]
    npg = page_table.shape[1]
    nchunk = npg + 1

    def page_idx(k):
        return lambda i, s, pt: (layer, pt[i, s * pp + k], 0, 0)

    def seq_idx(i, s, pt):
        return (i, 0, 0)

    in_specs = [pl.BlockSpec((1,) + qi32.shape[1:], seq_idx), pl.BlockSpec((1,) + w32.shape[1:], seq_idx),
                pl.BlockSpec((1, ts, IDX_DIM), seq_idx)]
    in_specs += [pl.BlockSpec((None, None, IDX_DIM, PAGE), page_idx(k)) for k in range(pp)]
    grid_spec = pltpu.PrefetchScalarGridSpec(
        num_scalar_prefetch=1, grid=(b, npg // pp), in_specs=in_specs,
        out_specs=pl.BlockSpec((1, nchunk, SUBLANES, PAGE), lambda i, s, pt: (i, 0, 0, 0)),
        scratch_shapes=[pltpu.VMEM((PAGE, LANES), F32)])
    return pl.pallas_call(
        functools.partial(_dsa_index_s_kernel, ts=ts, pp=pp, nchunk=nchunk), grid_spec=grid_spec,
        out_shape=jax.ShapeDtypeStruct((b, nchunk, SUBLANES, PAGE), F32),
        compiler_params=_cparams(("arbitrary", "arbitrary")), name="dsa_index_sample")(
            page_table, qi32, w32, kin, *([cache_kit] * pp))


def _select_kernel(keys_ref, thr_ref, need_ref, *, n_sel, nchunk):
    r = keys_ref.shape[1]
    wid = min(r, LANES)

    def count(pred, ref):
        out = []
        for lo in range(0, r, wid):
            refv = ref[:, lo:lo + wid]

            def cb(c, acc, lo=lo, refv=refv):
                off = pl.multiple_of(c * PAGE, PAGE)
                return acc + jnp.where(pred(keys_ref[pl.ds(off, PAGE), lo:lo + wid], refv), 1.0, 0.0)
            acc = lax.fori_loop(0, nchunk, cb, jnp.zeros((PAGE, wid), F32))
            out.append(jnp.sum(acc, axis=0, keepdims=True))
        return jnp.concatenate(out, axis=1)

    thr = _kth_largest(lambda trial: count(lambda kk, t: kk >= t, trial), n_sel, (1, r))
    thr_ref[...] = thr
    need_ref[...] = n_sel - count(lambda kk, t: kk > t, thr)


def _select(keys_t, n_sel):
    nkeys, r = keys_t.shape
    return pl.pallas_call(
        functools.partial(_select_kernel, n_sel=n_sel, nchunk=nkeys // PAGE),
        out_shape=[jax.ShapeDtypeStruct((1, r), F32), jax.ShapeDtypeStruct((1, r), F32)],
        compiler_params=pltpu.CompilerParams(vmem_limit_bytes=VMEM_LIMIT), name="dsa_select_sample")(keys_t)


def _dsa_attn_s_kernel(pt_ref, qa_ref, kn_ref, vn_ref, key_ref, thr_ref, need_ref, *refs, ts, pp, nchunk):
    kt_refs = refs[:pp]
    vt_refs = refs[pp:2 * pp]
    bias_ref, o_ref, kst_ref, vst_ref, pad_ref = refs[2 * pp:]
    b = pl.program_id(0)
    s = pl.program_id(1)
    last = nchunk - 1
    rpad = SUBLANES

    @pl.when(jnp.logical_and(b == 0, s == 0))
    def _():
        pad_ref[...] = jnp.zeros_like(pad_ref)

    for k in range(pp):
        kst_ref[s * pp + k] = kt_refs[k][...]
        vst_ref[s * pp + k] = vt_refs[k][...]

    @pl.when(s == pl.num_programs(1) - 1)
    def _():
        pad_ref[0, 0:ts, :] = kn_ref[0]
        pad_ref[1, 0:ts, :] = vn_ref[0]
        kst_ref[last] = pad_ref[0].T
        vst_ref[last] = pad_ref[1].T
        rr = lax.broadcasted_iota(jnp.int32, (rpad, PAGE), 0)
        cc = lax.broadcasted_iota(jnp.int32, (rpad, PAGE), 1)
        causal = jnp.logical_and(cc <= rr, cc < ts)
        thr = thr_ref[0]
        need = need_ref[0]
        r2 = lax.broadcasted_iota(jnp.int32, (PAGE, PAGE), 0)
        c2 = lax.broadcasted_iota(jnp.int32, (PAGE, PAGE), 1)
        ustrict = jnp.where(r2 < c2, 1.0, 0.0).astype(BF16)
        nrow = A_HEADS * rpad
        qbd = qa_ref[0].astype(BF16)
        carry = jnp.zeros((rpad, 1), F32)
        zs = []
        for c in range(nchunk):
            kk = key_ref[0, c]
            eq = kk == thr
            eqf = jnp.where(eq, 1.0, 0.0)
            prefix = _dot(eqf.astype(BF16), ustrict) + carry
            sel = jnp.logical_or(kk > thr, jnp.logical_and(eq, prefix < need))
            if c == last:
                sel = jnp.logical_and(sel, causal)
            am = jnp.where(sel, 0.0, NEG)
            carry = carry + jnp.sum(eqf, axis=1, keepdims=True)
            zs.append(_dot(qbd, kst_ref[c].astype(BF16)) + bias_ref[c] + jnp.concatenate([am] * A_HEADS, axis=0))
        zmax = zs[0]
        for c in range(1, nchunk):
            zmax = jnp.maximum(zmax, zs[c])
        m_row = jnp.max(zmax, axis=1, keepdims=True)
        psum = jnp.zeros((nrow, LANES), F32)
        acc = jnp.zeros((nrow, LANES), F32)
        for c in range(nchunk):
            p = jnp.exp(zs[c] - m_row)
            psum = psum + p
            acc = acc + _dot_nt(p.astype(BF16), vst_ref[c].astype(BF16))
        o = acc / jnp.sum(psum, axis=1, keepdims=True)
        lane = lax.broadcasted_iota(jnp.int32, (rpad, LANES), 1)
        for blk in range(A_HEADS // 2):
            grp = (2 * blk) // A_GROUP
            lo = o[(2 * blk) * rpad:(2 * blk + 1) * rpad]
            hi = o[(2 * blk + 1) * rpad:(2 * blk + 2) * rpad]
            if grp == 0:
                hi = pltpu.roll(hi, HEAD_DIM, axis=1)
            else:
                lo = pltpu.roll(lo, HEAD_DIM, axis=1)
            res = jnp.where(lane < HEAD_DIM, lo, hi)
            o_ref[0, :, blk * LANES:(blk + 1) * LANES] = res[0:ts]


def _dsa_attn_s(qa_bd, kn, vn, keys, thr, need, cache_kt, cache_vt, layer, page_table, bias, ts, pp):
    b = qa_bd.shape[0]
    npg = page_table.shape[1]
    nchunk = npg + 1
    width = A_HEADS * HEAD_DIM

    def page_idx(k):
        return lambda i, s, pt: (layer, pt[i, s * pp + k], 0, 0)

    def seq_idx(i, s, pt):
        return (i, 0, 0)

    in_specs = [pl.BlockSpec((1,) + qa_bd.shape[1:], seq_idx),
                pl.BlockSpec((1, ts, LANES), seq_idx), pl.BlockSpec((1, ts, LANES), seq_idx),
                pl.BlockSpec((1, nchunk, SUBLANES, PAGE), lambda i, s, pt: (i, 0, 0, 0)),
                pl.BlockSpec((1, SUBLANES, PAGE), seq_idx), pl.BlockSpec((1, SUBLANES, PAGE), seq_idx)]
    in_specs += [pl.BlockSpec((None, None, LANES, PAGE), page_idx(k)) for k in range(pp)] * 2
    in_specs += [pl.BlockSpec(bias.shape, lambda i, s, pt: (0, 0, 0))]
    grid_spec = pltpu.PrefetchScalarGridSpec(
        num_scalar_prefetch=1, grid=(b, npg // pp), in_specs=in_specs,
        out_specs=pl.BlockSpec((1, ts, width), seq_idx),
        scratch_shapes=[pltpu.VMEM((nchunk, LANES, PAGE), F32), pltpu.VMEM((nchunk, LANES, PAGE), F32),
                        pltpu.VMEM((2, PAGE, LANES), F32)])
    return pl.pallas_call(
        functools.partial(_dsa_attn_s_kernel, ts=ts, pp=pp, nchunk=nchunk), grid_spec=grid_spec,
        out_shape=jax.ShapeDtypeStruct((b, ts, width), F32),
        compiler_params=_cparams(("arbitrary", "arbitrary")), name="attn_dsa_sample")(
            page_table, qa_bd, kn, vn, keys, thr, need, *([cache_kt] * pp), *([cache_vt] * pp), bias)


def _bucket_starts():
    exact = N_BUCKETS // 2
    starts = list(range(exact))
    for j in range(N_BUCKETS - exact):
        starts.append(min(n for n in range(exact, MAX_DISTANCE + 1)
                          if int(math.log(n / exact) / math.log(MAX_DISTANCE / exact) * (N_BUCKETS - exact)) >= j))
    return starts


def _bias_of(tab, rel):
    n = jnp.maximum(rel, 0)
    bucket = jnp.zeros(rel.shape, jnp.int32)
    for start in _bucket_starts()[1:]:
        bucket = bucket + (n >= start).astype(jnp.int32)
    ones = (1,) * rel.ndim
    hit = bucket[None, None] == jnp.arange(N_BUCKETS, dtype=jnp.int32).reshape((N_BUCKETS, 1) + ones)
    return jnp.sum(jnp.where(hit, tab.astype(F32).reshape(tab.shape + ones), 0.0), axis=0)


def _bias_tiles_t(tab, tb):
    j = jnp.arange(tb, dtype=jnp.int32)[:, None]
    i = jnp.arange(tb, dtype=jnp.int32)[None, :]
    return _bias_of(tab, jnp.stack([d * tb + i - j for d in range(3)]))


def _split(w, sizes):
    offs = np.cumsum((0,) + tuple(sizes))
    return [w[:, int(offs[i]):int(offs[i + 1])] for i in range(len(sizes))]


def _pad_cols(w, n):
    return jnp.pad(w, ((0, 0), (0, n - w.shape[1])))


def _row_tile(m, want):
    t = min(m, want)
    while m % t:
        t //= 2
    return t


def _pages_per_step(npg, want):
    pp = min(npg, want)
    while npg % pp:
        pp -= 1
    return pp


def kernel(x_prompt, x_sample, cache_a_k, cache_a_v, cache_a_kidx, cache_b_k, cache_b_v, cache_c_k, cache_c_v,
           cache_c_logf, state_conv, page_table, rel_bias_table, w_in_even, w_out_even, lambda_q1, lambda_k1,
           lambda_q2, lambda_k2, diff_subln, w_in_odd, b_forget, w_out_odd, norm_mix_pre, norm_mix_post,
           norm_ffn_pre, norm_ffn_post, w_gate_up, w_conv, b_conv, w_down):
    bp, tp, d = x_prompt.shape
    bs, ts, _ = x_sample.shape
    depth = w_gate_up.shape[0]
    n_pool = cache_a_k.shape[1]
    npg = page_table.shape[1]
    past = npg * PAGE
    mp, ms = bp * tp, bs * ts
    tb_dsa = LANES
    tb_attn = min(512, tp)
    nsel_p = min(TOPK_MAX, tp // 4)
    nsel_s = min(TOPK_MAX, (past + ts) // 4)
    tm_p = _row_tile(mp, 512)
    tm_s = _row_tile(ms, 512)
    tf = D_FF
    dff = D_FF
    pp = _pages_per_step(npg, 8)

    xp = x_prompt.reshape(mp, d)
    xs = x_sample.reshape(ms, d)
    tab_a = rel_bias_table[:, :A_HEADS]
    tab_b = rel_bias_table[:, A_HEADS:]
    bias_a_p = _bias_tiles_t(tab_a, tb_dsa)
    bias_a_p = jnp.transpose(bias_a_p, (1, 2, 0, 3)).reshape(3, tb_dsa, A_HEADS * tb_dsa)
    bias_b_p = _bias_tiles_t(tab_b, tb_attn)
    rel_s = (past + jnp.arange(ts, dtype=jnp.int32))[:, None] - jnp.arange(past + PAGE, dtype=jnp.int32)[None, :]
    rows_a = _bias_of(tab_a, rel_s).reshape(A_HEADS, ts, npg + 1, PAGE)
    rows_b = _bias_of(tab_b, rel_s).reshape(B_HEADS, ts, npg + 1, PAGE)
    bias_a_s = jnp.pad(jnp.moveaxis(rows_a, 2, 0), ((0, 0), (0, 0), (0, SUBLANES - ts), (0, 0)))
    bias_a_s = bias_a_s.reshape(npg + 1, A_HEADS * SUBLANES, PAGE)
    kq_ok = rel_s.reshape(ts, npg + 1, PAGE) >= 0
    same = jnp.eye(B_HEADS, dtype=bool)
    bias_b_s = jnp.where(same[:, None, None, None, :] & kq_ok[None, :, :, :, None],
                         rows_b[..., None], NEG)
    bias_b_s = jnp.moveaxis(bias_b_s, 2, 0).reshape(npg + 1, B_HEADS * ts, PAGE * B_HEADS)
    bias_b_s = jnp.concatenate([bias_b_s, bias_b_s], axis=1)
    bias_b_new = bias_b_s[npg]
    bias_b_s = bias_b_s[:npg].reshape(npg // pp, pp, 2 * B_HEADS * ts, PAGE * B_HEADS)
    bias_b_s = jnp.swapaxes(bias_b_s, 1, 2).reshape(npg // pp, 2 * B_HEADS * ts, pp * PAGE * B_HEADS)

    ca_kt = jnp.transpose(cache_a_k, (0, 1, 3, 4, 2)).reshape(-1, n_pool, LANES, PAGE)
    ca_vt = jnp.transpose(cache_a_v, (0, 1, 3, 4, 2)).reshape(-1, n_pool, LANES, PAGE)
    ca_kit = jnp.transpose(cache_a_kidx, (0, 1, 3, 2))
    cb_k = cache_b_k.reshape(-1, n_pool, PAGE * B_HEADS, 2 * HEAD_DIM)
    cb_v = cache_b_v.reshape(-1, n_pool, PAGE * B_HEADS, B_VDIM)
    cc_kt = jnp.transpose(cache_c_k, (0, 1, 3, 4, 2)).reshape(-1, n_pool, C_HEADS * HEAD_DIM, PAGE)
    cc_vt = jnp.transpose(cache_c_v, (0, 1, 3, 4, 2)).reshape(-1, n_pool, C_HEADS * HEAD_DIM, PAGE)
    cc_lft = jnp.transpose(cache_c_logf, (0, 1, 3, 2))

    even_rows_p, even_rows_s, odd_rows_p, odd_rows_s, conv_p, conv_s = [], [], [], [], [], []
    for l in range(depth):
        if l % 2 == 0:
            e = l // 2
            lam_init = 0.8 - 0.6 * math.exp(-0.3 * l)
            ws = _split(w_in_even[e], EVEN_SPLIT)
            ws[5] = _pad_cols(ws[5], LANES)
            ws = [w.astype(BF16) for w in ws]
            wo = w_out_even[e].astype(BF16)
            wo_parts = [wo[:A_HEADS * HEAD_DIM], wo[A_HEADS * HEAD_DIM:]]
            lamp = jnp.stack([lambda_q1[e], lambda_k1[e], lambda_q2[e], lambda_k2[e]]).astype(F32)
            subln = diff_subln[e].reshape(1, B_VDIM)

            wq_a, wk_a, wv_a, wq_i, wk_i, ww_i, wq_b, wk_b, wv_b = ws
            items = [(wq_a, "f32"), (wk_a, "bf16"), (wk_a.T, "col"), (wv_a.T, "col"), (wq_i, "f32"),
                     (wk_i, "bf16"), (wk_i.T, "col"), (ww_i, "f32"), (wq_b, "f32"), (wk_b, "f32"), (wv_b, "f32")]
            q_a, k_a16, k_at, v_at, q_i, k_i16, k_it, w_i, q_b, k_b, v_b = _rms_proj(
                xp, norm_mix_pre[l], items, tm_p, seq=(bp, tp))
            r3 = lambda a: a.reshape(bp, tp, a.shape[1])
            o_a = _dsa_p(r3(q_a), r3(k_a16), v_at, r3(q_i), r3(k_i16), r3(w_i), bias_a_p, nsel_p)
            o_b = _flash2("diff", r3(q_b), r3(k_b), r3(v_b), (bias_b_p, lamp, subln), tb_attn, lam_init)
            xp = _proj_post(xp, norm_mix_post[l], [o_a.reshape(mp, -1), o_b.reshape(mp, -1)], wo_parts, tm_p)
            kv_rows = lambda a: jnp.transpose(a.reshape(bp, A_KV_HEADS, HEAD_DIM, tp), (0, 3, 1, 2))
            even_rows_p.append((kv_rows(k_at), kv_rows(v_at), jnp.transpose(k_it, (0, 2, 1)),
                                k_b.reshape(bp, tp, B_HEADS, 2 * HEAD_DIM), v_b.reshape(bp, tp, B_HEADS, B_VDIM)))

            q_a, k_a, v_a, q_i, k_i, w_i, q_b, k_b, v_b = _rms_proj(
                xs, norm_mix_pre[l], [(w, "f32") for w in ws], tm_s)
            s3 = lambda a: a.reshape(bs, ts, a.shape[1])
            qa4 = (q_a * (HEAD_DIM ** -0.5)).reshape(bs, ts, A_HEADS, HEAD_DIM)
            qa4 = jnp.pad(jnp.moveaxis(qa4, 1, 2), ((0, 0), (0, 0), (0, SUBLANES - ts), (0, 0)))
            zeros = jnp.zeros_like(qa4)
            qa_bd = jnp.concatenate([jnp.concatenate([qa4[:, :A_GROUP], zeros[:, :A_GROUP]], axis=-1),
                                     jnp.concatenate([zeros[:, A_GROUP:], qa4[:, A_GROUP:]], axis=-1)], axis=1)
            qa_bd = qa_bd.reshape(bs, A_HEADS * SUBLANES, LANES)
            qi4 = jnp.moveaxis(q_i.reshape(bs, ts, IDX_HEADS, IDX_DIM), 1, 2)
            qi32 = jnp.pad(qi4, ((0, 0), (0, 0), (0, SUBLANES - ts), (0, 0))).reshape(bs, IDX_HEADS * SUBLANES, IDX_DIM)
            w4 = jnp.moveaxis(w_i[:, :IDX_HEADS].reshape(bs, ts, IDX_HEADS), 1, 2) * (IDX_DIM ** -0.5 * IDX_HEADS ** -0.5)
            w32 = jnp.pad(w4, ((0, 0), (0, 0), (0, SUBLANES - ts))).reshape(bs, IDX_HEADS * SUBLANES, 1)
            w32 = jnp.broadcast_to(w32, (bs, IDX_HEADS * SUBLANES, PAGE))
            keys = _dsa_index_s(qi32, w32, s3(k_i), ca_kit, e, page_table, ts, _pages_per_step(npg, 16))
            keys_t = jnp.transpose(keys, (1, 3, 0, 2)).reshape((npg + 1) * PAGE, bs * SUBLANES)
            thr, need = _select(keys_t, nsel_s)
            thr = jnp.broadcast_to(thr.reshape(bs, SUBLANES, 1), (bs, SUBLANES, PAGE))
            need = jnp.broadcast_to(need.reshape(bs, SUBLANES, 1), (bs, SUBLANES, PAGE))
            o_a = _dsa_attn_s(qa_bd, s3(k_a), s3(v_a), keys, thr, need, ca_kt, ca_vt,
                              e, page_table, bias_a_s, ts, pp)
            o_b = _diff_s(s3(q_b), k_b.reshape(bs, ts * B_HEADS, 2 * HEAD_DIM), v_b.reshape(bs, ts * B_HEADS, B_VDIM),
                          cb_k, cb_v, e, page_table, bias_b_s, bias_b_new, lamp, subln, lam_init, pp)
            xs = _proj_post(xs, norm_mix_post[l], [o_a.reshape(ms, -1), o_b.reshape(ms, -1)], wo_parts, tm_s)
            even_rows_s.append((k_a.reshape(bs, ts, A_KV_HEADS, HEAD_DIM), v_a.reshape(bs, ts, A_KV_HEADS, HEAD_DIM),
                                k_i.reshape(bs, ts, IDX_DIM), k_b.reshape(bs, ts, B_HEADS, 2 * HEAD_DIM),
                                v_b.reshape(bs, ts, B_HEADS, B_VDIM)))
        else:
            o = l // 2
            ws = _split(w_in_odd[o], ODD_SPLIT)
            ws[3] = _pad_cols(ws[3], LANES)
            ws = [w.astype(BF16) for w in ws]
            bfp = _pad_cols(b_forget[o].reshape(1, C_HEADS), LANES)
            wo = [w_out_odd[o].astype(BF16)]

            wq, wk, wv, wf = ws
            items = [(wq, "f32"), (wk, "bf16"), (wk.T, "col"), (wv.T, "col"), (wf, "f32")]
            q, k16, k_t, v_t, lf = _rms_proj(xp, norm_mix_pre[l], items, tm_p, logsig_bias=bfp, seq=(bp, tp))
            r3 = lambda a: a.reshape(bp, tp, a.shape[1])
            cum, cumt = _cumsum(r3(lf), tb_attn)
            o_c = _flash2("fox", r3(q), r3(k16), v_t, (cum, cumt), tb_attn)
            xp = _proj_post(xp, norm_mix_post[l], [o_c.reshape(mp, -1)], wo, tm_p)
            kv_rows = lambda a: jnp.transpose(a.reshape(bp, C_HEADS, HEAD_DIM, tp), (0, 3, 1, 2))
            odd_rows_p.append((kv_rows(k_t), kv_rows(v_t), lf[:, :C_HEADS].reshape(bp, tp, C_HEADS)))

            q, k, v, lf = _rms_proj(xs, norm_mix_pre[l], [(w, "f32") for w in ws], tm_s, logsig_bias=bfp)
            s3 = lambda a: a.reshape(bs, ts, a.shape[1])
            o_c = _fox_s(s3(q), s3(k), s3(v), s3(lf), cc_kt, cc_vt, cc_lft, o, page_table, pp)
            xs = _proj_post(xs, norm_mix_post[l], [o_c.reshape(ms, -1)], wo, tm_s)
            odd_rows_s.append((k.reshape(bs, ts, C_HEADS, HEAD_DIM), v.reshape(bs, ts, C_HEADS, HEAD_DIM),
                               lf[:, :C_HEADS].reshape(bs, ts, C_HEADS)))

        wgu = w_gate_up[l].astype(BF16)
        wg, wu = wgu[:, :dff], wgu[:, dff:]
        wd = w_down[l].astype(BF16)
        tiles = tp // tm_p
        xp, tail = _ffn(xp, norm_ffn_pre[l], norm_ffn_post[l], wg, wu, w_conv[l], b_conv[l], wd,
                        tm_p, tf, tiles)
        conv_p.append(tail.reshape(bp, tiles, 8, dff)[:, tiles - 1, 8 - (CONV_W - 1):, :])
        st = state_conv[l]
        prev1 = jnp.concatenate([st[:, 1:2], jnp.zeros((bs, ts - 1, dff), F32)], axis=1).reshape(ms, dff)
        prev2 = jnp.concatenate([st, jnp.zeros((bs, ts - 2, dff), F32)], axis=1).reshape(ms, dff)
        xs, gfull = _ffn(xs, norm_ffn_pre[l], norm_ffn_post[l], wg, wu, w_conv[l], b_conv[l], wd,
                         tm_s, tf, ts, prev=(prev1, prev2))
        conv_s.append(gfull.reshape(bs, ts, dff)[:, ts - (CONV_W - 1):, :])

    def stack(rows, i):
        return jnp.stack([r[i] for r in rows])

    ev_p = [stack(even_rows_p, i) for i in range(5)]
    ev_s = [stack(even_rows_s, i) for i in range(5)]
    od_p = [stack(odd_rows_p, i) for i in range(3)]
    od_s = [stack(odd_rows_s, i) for i in range(3)]
    return (xp.reshape(bp, tp, d), xs.reshape(bs, ts, d),
            ev_p[0], ev_s[0], ev_p[1], ev_s[1], ev_p[2], ev_s[2], ev_p[3], ev_s[3], ev_p[4], ev_s[4],
            od_p[0], od_s[0], od_p[1], od_s[1], od_p[2], od_s[2],
            jnp.stack(conv_p), jnp.stack(conv_s))
```

```python
import functools
import math

import jax
import jax.numpy as jnp
import numpy as np
from jax import lax
from jax.experimental import pallas as pl
from jax.experimental.pallas import tpu as pltpu

F32 = jnp.float32
BF16 = jnp.bfloat16
HIGHEST = lax.Precision.HIGHEST

D_MODEL = 1024
HEAD_DIM = 64
A_HEADS = 8
A_KV_HEADS = 2
A_GROUP = A_HEADS // A_KV_HEADS
IDX_HEADS = 4
IDX_DIM = 64
TOPK_MAX = 256
B_HEADS = 4
B_VDIM = 128
C_HEADS = 16
N_BUCKETS = 32
MAX_DISTANCE = 128
D_FF = 2816
CONV_W = 3
EPS = 1e-6
PAGE = 128

LANES = 128
SUBLANES = 8
NEG = -1e30
INT_MIN = -2147483648
NEG_INF_CODE = 0x007FFFFF
VMEM_LIMIT = 48 * 1024 * 1024

EVEN_SPLIT = (512, 128, 128, 256, 64, 4, 512, 512, 512)
ODD_SPLIT = (1024, 1024, 1024, 16)

NT_DIMS = (((1,), (1,)), ((), ()))


def _cparams(sem):
    return pltpu.CompilerParams(dimension_semantics=sem, vmem_limit_bytes=VMEM_LIMIT)


def _rms(x, g):
    return x * lax.rsqrt(jnp.mean(x * x, axis=-1, keepdims=True) + EPS) * g


def _log_sigmoid(x):
    return -(jnp.maximum(-x, 0.0) + jnp.log1p(jnp.exp(-jnp.abs(x))))


def _gelu_tanh(x):
    c = math.sqrt(2.0 / math.pi)
    return x * (0.5 * (1.0 + jnp.tanh(c * (x + 0.044715 * (x * x * x)))))


def _dot(a, b):
    return jnp.dot(a, b, preferred_element_type=F32)


def _dot_nt(a, b):
    return lax.dot_general(a, b, NT_DIMS, preferred_element_type=F32)


def _dot_hi(a, b):
    return jnp.dot(a, b, preferred_element_type=F32, precision=HIGHEST)


def _softmax_rows(z, m_old, l_old, acc_old, pv):
    m_new = jnp.maximum(m_old, jnp.max(z, axis=1, keepdims=True))
    alpha = jnp.exp(m_old - m_new)
    p = jnp.exp(z - m_new)
    l_new = alpha * l_old + jnp.sum(p, axis=1, keepdims=True)
    acc_new = alpha * acc_old + pv(p.astype(BF16))
    return m_new, l_new, acc_new


def _softmax_cols(zt, m_old, l_old, acc_old, vt, shift=None):
    m_tile = jnp.max(zt, axis=0, keepdims=True)
    if shift is None:
        m_new = jnp.maximum(m_old, m_tile)
        p = jnp.exp(zt - m_new)
    else:
        m_new = jnp.maximum(m_old, m_tile + shift)
        p = jnp.exp(zt - (m_new - shift))
    alpha = jnp.exp(m_old - m_new)
    l_new = alpha * l_old + jnp.sum(p, axis=0, keepdims=True)
    acc_new = alpha * acc_old + _dot(vt, p.astype(BF16))
    return m_new, l_new, acc_new


def _ordered_value(code):
    key = code ^ jnp.int32(INT_MIN)
    return pltpu.bitcast(jnp.where(key < 0, key ^ jnp.int32(0x7FFFFFFF), key), F32)


def _kth_largest(count_ge, n_sel, shape):
    def below_neg_inf(code):
        return jnp.logical_and(code >= 0, code < jnp.int32(NEG_INF_CODE))

    def step(it, code):
        trial = code | lax.shift_left(jnp.int32(1), jnp.int32(31) - it)
        ok = jnp.logical_or(below_neg_inf(trial), count_ge(_ordered_value(trial)) >= n_sel)
        return jnp.where(ok, trial, code)

    code = lax.fori_loop(0, 32, step, jnp.zeros(shape, jnp.int32))
    return _ordered_value(jnp.where(below_neg_inf(code), jnp.int32(NEG_INF_CODE), code))


def _rms_proj_kernel(x_ref, g_ref, *refs, kinds, logsig_last):
    n_w = len(kinds)
    w_refs = refs[:n_w]
    if logsig_last:
        bias_ref = refs[n_w]
        o_refs = refs[n_w + 1:]
    else:
        o_refs = refs[n_w:]
    h = _rms(x_ref[...], g_ref[...]).astype(BF16)
    for idx, kind in enumerate(kinds):
        if kind == "col":
            o_refs[idx][0] = _dot_nt(w_refs[idx][...], h)
            continue
        y = _dot(h, w_refs[idx][...])
        if logsig_last and idx == n_w - 1:
            y = _log_sigmoid(y + bias_ref[...])
        o_refs[idx][...] = y.astype(o_refs[idx].dtype)


def _rms_proj(x2d, g, items, tm, logsig_bias=None, seq=None):
    m, d = x2d.shape
    kinds = tuple(kind for _, kind in items)
    in_specs = [pl.BlockSpec((tm, d), lambda i: (i, 0)), pl.BlockSpec((1, d), lambda i: (0, 0))]
    in_specs += [pl.BlockSpec(w.shape, lambda i: (0, 0)) for w, _ in items]
    args = [x2d, g.reshape(1, d)] + [w for w, _ in items]
    if logsig_bias is not None:
        in_specs.append(pl.BlockSpec(logsig_bias.shape, lambda i: (0, 0)))
        args.append(logsig_bias)
    out_shape, out_specs = [], []
    for w, kind in items:
        if kind == "col":
            nb, t = seq
            tiles = t // tm
            out_shape.append(jax.ShapeDtypeStruct((nb, w.shape[0], t), F32))
            out_specs.append(pl.BlockSpec((1, w.shape[0], tm), lambda i, tiles=tiles: (i // tiles, 0, i % tiles)))
        else:
            out_shape.append(jax.ShapeDtypeStruct((m, w.shape[1]), F32 if kind == "f32" else BF16))
            out_specs.append(pl.BlockSpec((tm, w.shape[1]), lambda i: (i, 0)))
    return pl.pallas_call(
        functools.partial(_rms_proj_kernel, kinds=kinds, logsig_last=logsig_bias is not None),
        grid=(m // tm,), in_specs=in_specs, out_specs=out_specs, out_shape=out_shape,
        compiler_params=_cparams(("parallel",)), name="rms_proj")(*args)


def _proj_post_kernel(x_ref, g_ref, *refs, n_in):
    o_refs = refs[:n_in]
    w_refs = refs[n_in:2 * n_in]
    out_ref = refs[2 * n_in]
    acc = None
    for o, w in zip(o_refs, w_refs):
        t = _dot(o[...].astype(BF16), w[...])
        acc = t if acc is None else acc + t
    out_ref[...] = x_ref[...] + _rms(acc, g_ref[...])


def _proj_post(x2d, g, o_list, w_list, tm):
    m, d = x2d.shape
    n_in = len(o_list)
    in_specs = [pl.BlockSpec((tm, d), lambda i: (i, 0)), pl.BlockSpec((1, d), lambda i: (0, 0))]
    in_specs += [pl.BlockSpec((tm, o.shape[1]), lambda i: (i, 0)) for o in o_list]
    in_specs += [pl.BlockSpec(w.shape, lambda i: (0, 0)) for w in w_list]
    return pl.pallas_call(
        functools.partial(_proj_post_kernel, n_in=n_in),
        grid=(m // tm,), in_specs=in_specs, out_specs=pl.BlockSpec((tm, d), lambda i: (i, 0)),
        out_shape=jax.ShapeDtypeStruct((m, d), F32),
        compiler_params=_cparams(("parallel",)), name="proj_post")(x2d, g.reshape(1, d), *o_list, *w_list)


def _ffn_kernel(*refs, tm, tiles_per_seq, sample, sub):
    if sample:
        (x_ref, gpre_ref, gpost_ref, wg_ref, wu_ref, wc_ref, bc_ref, wd_ref, p1_ref, p2_ref,
         y_ref, gout_ref, h_ref, acc_ref, gs_ref) = refs
    else:
        (x_ref, gpre_ref, gpost_ref, wg_ref, wu_ref, wc_ref, bc_ref, wd_ref,
         y_ref, gout_ref, h_ref, acc_ref, gs_ref, halo_ref) = refs
    i = pl.program_id(0)
    c = pl.program_id(1)

    @pl.when(c == 0)
    def _():
        h_ref[...] = _rms(x_ref[...], gpre_ref[...]).astype(BF16)
        acc_ref[...] = jnp.zeros_like(acc_ref)

    h = h_ref[...]
    tf = wg_ref.shape[1]
    if sample:
        gs_ref[0:8, :] = jnp.zeros((8, tf), F32)
    else:
        first = (i % tiles_per_seq) == 0

        @pl.when(first)
        def _():
            gs_ref[0:8, :] = jnp.zeros((8, tf), F32)

        @pl.when(jnp.logical_not(first))
        def _():
            gs_ref[0:8, :] = halo_ref[c]

    if sample:
        t = lax.broadcasted_iota(jnp.int32, (tm, 1), 0) % tiles_per_seq
    part = None
    bounds = [(lo, min(lo + sub, tf)) for lo in range(0, tf, sub)]
    nxt = (_dot(h, wg_ref[:, bounds[0][0]:bounds[0][1]]), _dot(h, wu_ref[:, bounds[0][0]:bounds[0][1]]))
    for idx, (lo, hi) in enumerate(bounds):
        g, u = nxt
        if idx + 1 < len(bounds):
            nlo, nhi = bounds[idx + 1]
            nxt = (_dot(h, wg_ref[:, nlo:nhi]), _dot(h, wu_ref[:, nlo:nhi]))
        gs_ref[8:tm + 8, lo:hi] = g
        g1 = gs_ref[7:tm + 7, lo:hi]
        g2 = gs_ref[6:tm + 6, lo:hi]
        if sample:
            g1 = jnp.where(t >= 1, g1, 0.0) + p1_ref[:, lo:hi]
            g2 = jnp.where(t >= 2, g2, 0.0) + p2_ref[:, lo:hi]
            gout_ref[:, lo:hi] = g
        w = wc_ref[:, lo:hi]
        gc = bc_ref[:, lo:hi] + w[0:1, :] * g2
        gc = gc + w[1:2, :] * g1
        gc = gc + w[2:3, :] * g
        a = _gelu_tanh(gc) * u
        d = _dot(a.astype(BF16), wd_ref[lo:hi, :])
        part = d if part is None else part + d
    acc_ref[...] += part
    if not sample:
        tail = gs_ref[tm:tm + 8, :]
        halo_ref[c] = tail
        gout_ref[0] = tail

    @pl.when(c == pl.num_programs(1) - 1)
    def _():
        y_ref[...] = x_ref[...] + _rms(acc_ref[...], gpost_ref[...])


def _ffn(x2d, gpre, gpost, wg, wu, wc, bc, wd, tm, tf, tiles_per_seq, prev=None):
    m, d = x2d.shape
    dff = wg.shape[1]
    nc = dff // tf
    sample = prev is not None
    wmode = pl.Buffered(1) if nc == 1 else None
    in_specs = [
        pl.BlockSpec((tm, d), lambda i, c: (i, 0)),
        pl.BlockSpec((1, d), lambda i, c: (0, 0)),
        pl.BlockSpec((1, d), lambda i, c: (0, 0)),
        pl.BlockSpec((d, tf), lambda i, c: (0, c), pipeline_mode=wmode),
        pl.BlockSpec((d, tf), lambda i, c: (0, c), pipeline_mode=wmode),
        pl.BlockSpec((CONV_W, tf), lambda i, c: (0, c)),
        pl.BlockSpec((1, tf), lambda i, c: (0, c)),
        pl.BlockSpec((tf, d), lambda i, c: (c, 0), pipeline_mode=wmode),
    ]
    args = [x2d, gpre.reshape(1, d), gpost.reshape(1, d), wg, wu, wc, bc.reshape(1, dff), wd]
    scratch = [pltpu.VMEM((tm, d), BF16), pltpu.VMEM((tm, d), F32), pltpu.VMEM((tm + 8, tf), F32)]
    if sample:
        in_specs += [pl.BlockSpec((tm, tf), lambda i, c: (i, c))] * 2
        args += list(prev)
        gout_shape = jax.ShapeDtypeStruct((m, dff), F32)
        gout_spec = pl.BlockSpec((tm, tf), lambda i, c: (i, c))
    else:
        gout_shape = jax.ShapeDtypeStruct((m // tm, 8, dff), F32)
        gout_spec = pl.BlockSpec((1, 8, tf), lambda i, c: (i, 0, c))
        scratch.append(pltpu.VMEM((nc, 8, tf), F32))
    return pl.pallas_call(
        functools.partial(_ffn_kernel, tm=tm, tiles_per_seq=tiles_per_seq, sample=sample, sub=6 * LANES),
        grid=(m // tm, nc), in_specs=in_specs,
        out_specs=[pl.BlockSpec((tm, d), lambda i, c: (i, 0)), gout_spec],
        out_shape=[jax.ShapeDtypeStruct((m, d), F32), gout_shape],
        scratch_shapes=scratch,
        compiler_params=_cparams(("arbitrary", "arbitrary")), name="conv_ffn")(*args)


def _cumsum_kernel(lf_ref, cum_ref, cumt_ref, *, tb, nchunk):
    row = lax.broadcasted_iota(jnp.int32, (tb, tb), 0)
    col = lax.broadcasted_iota(jnp.int32, (tb, tb), 1)
    lower = jnp.where(col <= row, 1.0, 0.0).astype(F32)
    carry = jnp.zeros((1, LANES), F32)
    for c in range(nchunk):
        x = lf_ref[0, c * tb:(c + 1) * tb, :]
        ct = _dot_hi(lower, x) + carry
        cumt_ref[0, c * tb:(c + 1) * tb, :] = ct
        carry = ct[tb - 1:tb, :]
        cum_ref[0, c] = ct.T[0:C_HEADS, :]


def _cumsum(lf3d, tb):
    b, t, _ = lf3d.shape
    nchunk = t // tb
    return pl.pallas_call(
        functools.partial(_cumsum_kernel, tb=tb, nchunk=nchunk),
        grid=(b,), in_specs=[pl.BlockSpec((1, t, LANES), lambda i: (i, 0, 0))],
        out_specs=[pl.BlockSpec((1, nchunk, C_HEADS, tb), lambda i: (i, 0, 0, 0)),
                   pl.BlockSpec((1, t, LANES), lambda i: (i, 0, 0))],
        out_shape=[jax.ShapeDtypeStruct((b, nchunk, C_HEADS, tb), F32),
                   jax.ShapeDtypeStruct((b, t, LANES), F32)],
        compiler_params=_cparams(("parallel",)), name="logf_cumsum")(lf3d)


def _flash2_kernel(*refs, mode, tb, nk, lam_init):
    if mode == "fox":
        q_ref, k_ref, v_ref, cum_ref, cumt_ref, o_ref, vt_ref, ck_ref = refs
    else:
        q_ref, k_ref, v_ref, bias_ref, lamp_ref, subln_ref, o_ref, vt_ref, kb_ref = refs
    g = pl.program_id(1)
    qi = pl.program_id(2)
    lane = lax.broadcasted_iota(jnp.int32, (tb, LANES), 1)
    krow = lax.broadcasted_iota(jnp.int32, (tb, tb), 0)
    qcol = lax.broadcasted_iota(jnp.int32, (tb, tb), 1)
    tri = krow <= qcol

    @pl.when(qi == 0)
    def _():
        if mode == "fox":
            for c in range(nk):
                vt_ref[c] = v_ref[0, :, c * tb:(c + 1) * tb].astype(BF16)
        else:
            kb_ref[...] = k_ref[0].astype(BF16)
            for c in range(nk):
                vt_ref[c] = v_ref[0, c * tb:(c + 1) * tb, :].T.astype(BF16)
        if mode == "fox":
            hrow = lax.broadcasted_iota(jnp.int32, (LANES, LANES), 0)
            for a in range(2):
                onehot = jnp.where(hrow == 2 * g + a, 1.0, 0.0).astype(F32)
                ck_ref[a] = _dot_hi(cumt_ref[0], onehot)

    q = q_ref[0] * (HEAD_DIM ** -0.5)
    qall = jnp.concatenate([jnp.where(lane < HEAD_DIM, q, 0.0), jnp.where(lane >= HEAD_DIM, q, 0.0)],
                           axis=0).astype(BF16)
    if mode == "fox":
        cq = jnp.concatenate([cum_ref[0, qi, pl.ds(2 * g + a, 1), :] for a in range(2)], axis=1)
    rep = tb // LANES

    def qk(c):
        krows = k_ref[0, c * tb:(c + 1) * tb, :] if mode == "fox" else kb_ref[c * tb:(c + 1) * tb, :]
        return _dot_nt(krows, qall)

    def attend(nfull):
        m = jnp.full((1, 2 * tb), NEG, F32)
        l = jnp.zeros((1, 2 * tb), F32)
        acc = jnp.zeros((LANES, 2 * tb), F32)
        zt = qk(0)
        for c in range(nfull + 1):
            zt_next = qk(c + 1) if c < nfull else None
            if mode == "fox":
                ck0 = ck_ref[0, c * tb:(c + 1) * tb, :]
                ck1 = ck_ref[1, c * tb:(c + 1) * tb, :]
                zt = zt - jnp.concatenate([ck0] * rep + [ck1] * rep, axis=1)
                shift = cq
            else:
                bt = bias_ref[0, min(nfull - c, 2)]
                zt = zt + jnp.concatenate([bt, bt], axis=1)
                shift = None
            if c == nfull:
                zt = jnp.where(jnp.concatenate([tri, tri], axis=1), zt, NEG)
            m, l, acc = _softmax_cols(zt, m, l, acc, vt_ref[c], shift)
            zt = zt_next
        o = acc / l
        o0 = o[:, 0:tb].T
        o1 = o[:, tb:2 * tb].T
        if mode == "fox":
            o_ref[0] = jnp.where(lane < HEAD_DIM, o0, o1)
        else:
            lp = lamp_ref[...]
            lam = (jnp.exp(jnp.sum(lp[0:1] * lp[1:2], axis=1, keepdims=True))
                   - jnp.exp(jnp.sum(lp[2:3] * lp[3:4], axis=1, keepdims=True)) + lam_init)
            o_ref[0] = _rms(o0 - lam * o1, subln_ref[...]) * (1.0 - lam_init)

    for blk in range(nk):
        pl.when(qi == blk)(functools.partial(attend, blk))


def _flash2(mode, q, k, v, extra, tb, lam_init=0.0):
    b, t, width = q.shape
    ng = width // LANES
    nq = t // tb
    in_specs = [pl.BlockSpec((1, tb, LANES), lambda i, g, j: (i, j, g)),
                pl.BlockSpec((1, t, LANES), lambda i, g, j: (i, 0, g)),
                pl.BlockSpec((1, t, LANES), lambda i, g, j: (i, 0, g))]
    scratch = [pltpu.VMEM((nq, LANES, tb), BF16)]
    if mode == "fox":
        cum, cumt = extra
        in_specs[2] = pl.BlockSpec((1, LANES, t), lambda i, g, j: (i, g, 0))
        in_specs += [pl.BlockSpec((1, nq, C_HEADS, tb), lambda i, g, j: (i, 0, 0, 0)),
                     pl.BlockSpec((1, t, LANES), lambda i, g, j: (i, 0, 0))]
        scratch.append(pltpu.VMEM((2, t, LANES), F32))
    else:
        bias, lamp, subln = extra
        in_specs += [pl.BlockSpec((1, 3, tb, tb), lambda i, g, j: (g, 0, 0, 0)),
                     pl.BlockSpec(lamp.shape, lambda i, g, j: (0, 0)),
                     pl.BlockSpec(subln.shape, lambda i, g, j: (0, 0))]
        scratch.append(pltpu.VMEM((t, LANES), BF16))
    return pl.pallas_call(
        functools.partial(_flash2_kernel, mode=mode, tb=tb, nk=nq, lam_init=lam_init),
        grid=(b, ng, nq), in_specs=in_specs,
        out_specs=pl.BlockSpec((1, tb, LANES), lambda i, g, j: (i, j, g)),
        out_shape=jax.ShapeDtypeStruct((b, t, width), F32),
        scratch_shapes=scratch,
        compiler_params=_cparams(("arbitrary", "arbitrary", "arbitrary")),
        name="attn_" + mode)(q, k, v, *extra)


def _dsa_p_kernel(qa_ref, ka_ref, vat_in_ref, qi_ref, ki_ref, wi_ref, bias_ref, o_ref,
                  vat_ref, key_ref, qm_ref, m_ref, l_ref, acc_ref, *, n_sel, nk):
    tb = LANES
    qi = pl.program_id(1)
    krow = lax.broadcasted_iota(jnp.int32, (tb, tb), 0)
    qcol = lax.broadcasted_iota(jnp.int32, (tb, tb), 1)
    tri = krow <= qcol

    @pl.when(qi == 0)
    def _():
        for c in range(nk // 2):
            vat_ref[c] = vat_in_ref[0, :, c * 2 * tb:(c + 1) * 2 * tb].astype(BF16)

    qidx = qi_ref[0]
    qh = jnp.concatenate([qidx[:, h * IDX_DIM:(h + 1) * IDX_DIM] for h in range(IDX_HEADS)],
                         axis=0).astype(BF16)
    wt = (wi_ref[0] * (IDX_DIM ** -0.5 * IDX_HEADS ** -0.5)).T
    wall = jnp.concatenate([wt[h:h + 1, :] for h in range(IDX_HEADS)], axis=1)

    npairs = (qi + 2) // 2

    def chunk_valid(c):
        return jnp.logical_or(c < qi, jnp.logical_and(c == qi, tri))

    def p1(p, carry):
        off = pl.multiple_of(p * 2 * tb, 2 * tb)
        sh = jnp.maximum(_dot_nt(ki_ref[0, pl.ds(off, 2 * tb), :], qh), 0.0) * wall
        s = sh[:, 0:tb]
        for h in range(1, IDX_HEADS):
            s = s + sh[:, h * tb:(h + 1) * tb]
        valid = jnp.concatenate([chunk_valid(2 * p), chunk_valid(2 * p + 1)], axis=0)
        key_ref[pl.ds(2 * p, 2)] = jnp.where(valid, s, -jnp.inf).reshape(2, tb, tb)
        return carry

    lax.fori_loop(0, npairs, p1, 0)

    def count(pred):
        def cb(p, acc):
            hit = jnp.where(pred(key_ref[pl.ds(2 * p, 2)]), 1.0, 0.0)
            return acc + (hit[0] + hit[1])
        acc = lax.fori_loop(0, npairs, cb, jnp.zeros((tb, tb), F32))
        return jnp.sum(acc, axis=0, keepdims=True)

    thr = _kth_largest(lambda trial: count(lambda kk: kk >= trial), n_sel, (1, tb))
    need = n_sel - count(lambda kk: kk > thr)
    lstrict = jnp.where(qcol < krow, 1.0, 0.0).astype(BF16)

    lane = qcol
    qa = qa_ref[0] * (HEAD_DIM ** -0.5)
    for j in range(A_HEADS):
        blk = qa[:, (j // 2) * LANES:(j // 2 + 1) * LANES]
        grp = j // A_GROUP
        if j % 2 != grp:
            blk = pltpu.roll(blk, HEAD_DIM, axis=1)
        keep = (lane < HEAD_DIM) if grp == 0 else (lane >= HEAD_DIM)
        qm_ref[j] = jnp.where(keep, blk, 0.0).astype(BF16)
    m_ref[...] = jnp.full(m_ref.shape, NEG, F32)
    l_ref[...] = jnp.zeros_like(l_ref)
    acc_ref[...] = jnp.zeros_like(acc_ref)

    qall = qm_ref[...].reshape(A_HEADS * tb, LANES)

    def chunk_mask(c, carry):
        kk = key_ref[c]
        eq = kk == thr
        eqf = jnp.where(eq, 1.0, 0.0)
        prefix = _dot(lstrict, eqf.astype(BF16)) + carry
        sel = jnp.logical_or(kk > thr, jnp.logical_and(eq, prefix < need))
        sel = jnp.logical_and(sel, chunk_valid(c))
        am = jnp.where(sel, 0.0, NEG)
        return jnp.concatenate([am] * A_HEADS, axis=1), carry + jnp.sum(eqf, axis=0, keepdims=True)

    def p4(p, carry):
        am_a, carry = chunk_mask(2 * p, carry)
        am_b, carry = chunk_mask(2 * p + 1, carry)
        bias = jnp.concatenate([bias_ref[jnp.clip(qi - 2 * p, 0, 2)], bias_ref[jnp.clip(qi - 2 * p - 1, 0, 2)]],
                               axis=0)
        off = pl.multiple_of(p * 2 * tb, 2 * tb)
        zt = (_dot_nt(ka_ref[0, pl.ds(off, 2 * tb), :], qall) + bias
              + jnp.concatenate([am_a, am_b], axis=0))
        m_new, l_new, acc_new = _softmax_cols(zt, m_ref[...], l_ref[...], acc_ref[...], vat_ref[p])
        m_ref[...] = m_new
        l_ref[...] = l_new
        acc_ref[...] = acc_new
        return carry

    lax.fori_loop(0, npairs, p4, jnp.zeros((1, tb), F32))
    oall = acc_ref[...] / l_ref[...]
    for c2 in range(A_HEADS // 2):
        grp = (2 * c2) // A_GROUP
        lo = oall[:, (2 * c2) * tb:(2 * c2 + 1) * tb].T
        hi = oall[:, (2 * c2 + 1) * tb:(2 * c2 + 2) * tb].T
        if grp == 0:
            hi = pltpu.roll(hi, HEAD_DIM, axis=1)
        else:
            lo = pltpu.roll(lo, HEAD_DIM, axis=1)
        o_ref[0, :, c2 * LANES:(c2 + 1) * LANES] = jnp.where(lane < HEAD_DIM, lo, hi)


def _dsa_p(q_a, k_a, v_at, q_i, k_i, w_i, bias, n_sel):
    b, t, _ = q_a.shape
    tb = LANES
    nq = t // tb
    in_specs = [pl.BlockSpec((1, tb, q_a.shape[2]), lambda i, j: (i, j, 0)),
                pl.BlockSpec((1, t, LANES), lambda i, j: (i, 0, 0)),
                pl.BlockSpec((1, LANES, t), lambda i, j: (i, 0, 0)),
                pl.BlockSpec((1, tb, q_i.shape[2]), lambda i, j: (i, j, 0)),
                pl.BlockSpec((1, t, IDX_DIM), lambda i, j: (i, 0, 0)),
                pl.BlockSpec((1, tb, LANES), lambda i, j: (i, j, 0)),
                pl.BlockSpec(bias.shape, lambda i, j: (0, 0, 0))]
    return pl.pallas_call(
        functools.partial(_dsa_p_kernel, n_sel=n_sel, nk=nq),
        grid=(b, nq), in_specs=in_specs,
        out_specs=pl.BlockSpec((1, tb, q_a.shape[2]), lambda i, j: (i, j, 0)),
        out_shape=jax.ShapeDtypeStruct(q_a.shape, F32),
        scratch_shapes=[pltpu.VMEM((nq // 2, LANES, 2 * tb), BF16),
                        pltpu.VMEM((nq, tb, tb), F32),
                        pltpu.VMEM((A_HEADS, tb, LANES), BF16),
                        pltpu.VMEM((1, A_HEADS * tb), F32), pltpu.VMEM((1, A_HEADS * tb), F32),
                        pltpu.VMEM((LANES, A_HEADS * tb), F32)],
        compiler_params=_cparams(("arbitrary", "arbitrary")), name="attn_dsa")(q_a, k_a, v_at, q_i, k_i, w_i, bias)


def _fox_s_kernel(pt_ref, q_ref, kn_ref, vn_ref, lfn_ref, *refs, ts, pp):
    kt_refs = refs[:pp]
    vt_refs = refs[pp:2 * pp]
    lft_refs = refs[2 * pp:3 * pp]
    o_ref = refs[3 * pp]
    qbd_ref, kpad_ref, vpad_ref, lfpad_ref, m_ref, l_ref, acc_ref, carry_ref, cq_ref = refs[3 * pp + 1:]
    b = pl.program_id(0)
    s = pl.program_id(1)
    nrow = ts * C_HEADS
    width = C_HEADS * HEAD_DIM
    row = lax.broadcasted_iota(jnp.int32, (PAGE, PAGE), 0)
    col = lax.broadcasted_iota(jnp.int32, (PAGE, PAGE), 1)

    @pl.when(jnp.logical_and(b == 0, s == 0))
    def _():
        kpad_ref[...] = jnp.zeros_like(kpad_ref)
        vpad_ref[...] = jnp.zeros_like(vpad_ref)

    def update(z, pv):
        m_new, l_new, acc_new = _softmax_rows(z, m_ref[...], l_ref[...], acc_ref[...], pv)
        m_ref[...] = m_new
        l_ref[...] = l_new
        acc_ref[...] = acc_new

    @pl.when(s == 0)
    def _():
        hmask = (lax.broadcasted_iota(jnp.int32, (C_HEADS, width), 1) // HEAD_DIM
                 == lax.broadcasted_iota(jnp.int32, (C_HEADS, width), 0))
        q = q_ref[0] * (HEAD_DIM ** -0.5)
        for i in range(ts):
            qbd_ref[i * C_HEADS:(i + 1) * C_HEADS, :] = jnp.where(
                hmask, jnp.broadcast_to(q[i:i + 1, :], (C_HEADS, width)), 0.0).astype(BF16)
        kpad_ref[0:ts, :] = kn_ref[0]
        vpad_ref[0:ts, :] = vn_ref[0]
        lfpad_ref[...] = jnp.zeros_like(lfpad_ref)
        lfpad_ref[0:ts, :] = lfn_ref[0]
        m_ref[...] = jnp.full(m_ref.shape, NEG, F32)
        l_ref[...] = jnp.zeros_like(l_ref)
        acc_ref[...] = jnp.zeros_like(acc_ref)
        carry_ref[...] = jnp.zeros_like(carry_ref)
        lft = lfpad_ref[...].T[0:C_HEADS, :]
        incl = jnp.where(row <= col, 1.0, 0.0).astype(F32)
        cnew = _dot_hi(lft, incl)
        for i in range(ts):
            cq_ref[i * C_HEADS:(i + 1) * C_HEADS, :] = jnp.broadcast_to(cnew[:, i:i + 1], (C_HEADS, PAGE))
        z = _dot_nt(qbd_ref[...], kpad_ref[...].astype(BF16))
        z = z + cq_ref[...] - jnp.concatenate([cnew] * ts, axis=0)
        rr = lax.broadcasted_iota(jnp.int32, (nrow, PAGE), 0)
        cc = lax.broadcasted_iota(jnp.int32, (nrow, PAGE), 1)
        z = jnp.where(cc * C_HEADS <= rr, z, NEG)
        vnew = vpad_ref[...].astype(BF16)
        update(z, lambda p: _dot(p, vnew))

    later = jnp.where(row > col, 1.0, 0.0).astype(F32)
    carry = carry_ref[...]
    sufs = []
    for k in range(pp):
        lft = lft_refs[k][...]
        sufs.append(_dot_hi(lft, later) + carry)
        carry = carry + jnp.sum(lft, axis=1, keepdims=True)
    carry_ref[...] = carry
    suf = jnp.concatenate(sufs, axis=1)
    kt = jnp.concatenate([kt_refs[k][...].astype(BF16) for k in range(pp)], axis=1)
    vt = jnp.concatenate([vt_refs[k][...].astype(BF16) for k in range(pp)], axis=1)
    z = _dot(qbd_ref[...], kt)
    z = z + jnp.concatenate([cq_ref[...]] * pp, axis=1) + jnp.concatenate([suf] * ts, axis=0)
    update(z, lambda p: _dot_nt(p, vt))

    @pl.when(s == pl.num_programs(1) - 1)
    def _():
        hmask = (lax.broadcasted_iota(jnp.int32, (C_HEADS, width), 1) // HEAD_DIM
                 == lax.broadcasted_iota(jnp.int32, (C_HEADS, width), 0))
        o = acc_ref[...] / l_ref[...]
        for i in range(ts):
            blk = jnp.where(hmask, o[i * C_HEADS:(i + 1) * C_HEADS, :], 0.0)
            o_ref[0, i:i + 1, :] = jnp.sum(blk, axis=0, keepdims=True)


def _fox_s(q, kn, vn, lfn, cache_kt, cache_vt, cache_lft, layer, page_table, pp):
    b, ts, width = q.shape
    npg = page_table.shape[1]
    nrow = ts * C_HEADS

    def page_idx(k):
        return lambda i, s, pt: (layer, pt[i, npg - 1 - (s * pp + k)], 0, 0)

    def seq_idx(i, s, pt):
        return (i, 0, 0)

    in_specs = [pl.BlockSpec((1, ts, width), seq_idx), pl.BlockSpec((1, ts, width), seq_idx),
                pl.BlockSpec((1, ts, width), seq_idx), pl.BlockSpec((1, ts, LANES), seq_idx)]
    in_specs += [pl.BlockSpec((None, None, width, PAGE), page_idx(k)) for k in range(pp)]
    in_specs += [pl.BlockSpec((None, None, width, PAGE), page_idx(k)) for k in range(pp)]
    in_specs += [pl.BlockSpec((None, None, C_HEADS, PAGE), page_idx(k)) for k in range(pp)]
    grid_spec = pltpu.PrefetchScalarGridSpec(
        num_scalar_prefetch=1, grid=(b, npg // pp), in_specs=in_specs,
        out_specs=pl.BlockSpec((1, ts, width), seq_idx),
        scratch_shapes=[pltpu.VMEM((nrow, width), BF16), pltpu.VMEM((PAGE, width), F32),
                        pltpu.VMEM((PAGE, width), F32), pltpu.VMEM((PAGE, LANES), F32),
                        pltpu.VMEM((nrow, 1), F32), pltpu.VMEM((nrow, 1), F32),
                        pltpu.VMEM((nrow, width), F32), pltpu.VMEM((C_HEADS, PAGE), F32),
                        pltpu.VMEM((nrow, PAGE), F32)])
    return pl.pallas_call(
        functools.partial(_fox_s_kernel, ts=ts, pp=pp), grid_spec=grid_spec,
        out_shape=jax.ShapeDtypeStruct((b, ts, width), F32),
        compiler_params=_cparams(("arbitrary", "arbitrary")), name="attn_fox_sample")(
            page_table, q, kn, vn, lfn, *([cache_kt] * pp), *([cache_vt] * pp), *([cache_lft] * pp))


def _diff_s_kernel(pt_ref, q_ref, kn_ref, vn_ref, *refs, ts, pp, npg, lam_init):
    k_refs = refs[:pp]
    v_refs = refs[pp:2 * pp]
    (bias_ref, biasn_ref, lamp_ref, subln_ref, o_ref,
     qx_ref, kpad_ref, vpad_ref, m_ref, l_ref, acc_ref) = refs[2 * pp:]
    b = pl.program_id(0)
    s = pl.program_id(1)
    hrows = B_HEADS * ts
    lane = lax.broadcasted_iota(jnp.int32, (ts, LANES), 1)

    @pl.when(jnp.logical_and(b == 0, s == 0))
    def _():
        kpad_ref[...] = jnp.zeros_like(kpad_ref)
        vpad_ref[...] = jnp.zeros_like(vpad_ref)

    @pl.when(s == 0)
    def _():
        q = q_ref[0] * (HEAD_DIM ** -0.5)
        for n in range(B_HEADS):
            blk = q[:, n * LANES:(n + 1) * LANES]
            qx_ref[n * ts:(n + 1) * ts, :] = jnp.where(lane < HEAD_DIM, blk, 0.0)
            qx_ref[hrows + n * ts:hrows + (n + 1) * ts, :] = jnp.where(lane >= HEAD_DIM, blk, 0.0)
        m_ref[...] = jnp.full(m_ref.shape, NEG, F32)
        l_ref[...] = jnp.zeros_like(l_ref)
        acc_ref[...] = jnp.zeros_like(acc_ref)

    def step(kall, vall, bias):
        z = _dot_nt(qx_ref[...].astype(BF16), kall) + bias
        m_new, l_new, acc_new = _softmax_rows(z, m_ref[...], l_ref[...], acc_ref[...], lambda p: _dot(p, vall))
        m_ref[...] = m_new
        l_ref[...] = l_new
        acc_ref[...] = acc_new

    step(jnp.concatenate([k_refs[k][...].astype(BF16) for k in range(pp)], axis=0),
         jnp.concatenate([v_refs[k][...].astype(BF16) for k in range(pp)], axis=0), bias_ref[s])

    @pl.when(s == pl.num_programs(1) - 1)
    def _():
        kpad_ref[0:ts * B_HEADS, :] = kn_ref[0]
        vpad_ref[0:ts * B_HEADS, :] = vn_ref[0]
        step(kpad_ref[...].astype(BF16), vpad_ref[...].astype(BF16), biasn_ref[...])
        lp = lamp_ref[...]
        lam = (jnp.exp(jnp.sum(lp[0:1] * lp[1:2], axis=1, keepdims=True))
               - jnp.exp(jnp.sum(lp[2:3] * lp[3:4], axis=1, keepdims=True)) + lam_init)
        o = acc_ref[...] / l_ref[...]
        od = o[0:hrows] - lam * o[hrows:2 * hrows]
        for n in range(B_HEADS):
            o_ref[0, :, n * B_VDIM:(n + 1) * B_VDIM] = (
                _rms(od[n * ts:(n + 1) * ts], subln_ref[...]) * (1.0 - lam_init))


def _diff_s(q, kn16, vn16, cache_k, cache_v, layer, page_table, bias, bias_new, lamp, subln, lam_init, pp):
    b, ts, width = q.shape
    npg = page_table.shape[1]
    nrow = 2 * B_HEADS * ts
    krows = PAGE * B_HEADS

    def page_idx(k):
        return lambda i, s, pt: (layer, pt[i, s * pp + k], 0, 0)

    def seq_idx(i, s, pt):
        return (i, 0, 0)

    in_specs = [pl.BlockSpec((1, ts, width), seq_idx), pl.BlockSpec((1,) + kn16.shape[1:], seq_idx),
                pl.BlockSpec((1,) + vn16.shape[1:], seq_idx)]
    in_specs += [pl.BlockSpec((None, None, krows, LANES), page_idx(k)) for k in range(pp)] * 2
    in_specs += [pl.BlockSpec(bias.shape, lambda i, s, pt: (0, 0, 0)),
                 pl.BlockSpec(bias_new.shape, lambda i, s, pt: (0, 0)),
                 pl.BlockSpec(lamp.shape, lambda i, s, pt: (0, 0)),
                 pl.BlockSpec(subln.shape, lambda i, s, pt: (0, 0))]
    grid_spec = pltpu.PrefetchScalarGridSpec(
        num_scalar_prefetch=1, grid=(b, npg // pp), in_specs=in_specs,
        out_specs=pl.BlockSpec((1, ts, width), seq_idx),
        scratch_shapes=[pltpu.VMEM((nrow, LANES), F32), pltpu.VMEM((krows, LANES), F32),
                        pltpu.VMEM((krows, LANES), F32), pltpu.VMEM((nrow, 1), F32),
                        pltpu.VMEM((nrow, 1), F32), pltpu.VMEM((nrow, LANES), F32)])
    return pl.pallas_call(
        functools.partial(_diff_s_kernel, ts=ts, pp=pp, npg=npg, lam_init=lam_init), grid_spec=grid_spec,
        out_shape=jax.ShapeDtypeStruct((b, ts, width), F32),
        compiler_params=_cparams(("arbitrary", "arbitrary")), name="attn_diff_sample")(
            page_table, q, kn16, vn16, *([cache_k] * pp), *([cache_v] * pp), bias, bias_new, lamp, subln)


def _dsa_index_s_kernel(pt_ref, qi_ref, wi_ref, kin_ref, *refs, ts, pp, nchunk):
    kit_refs = refs[:pp]
    key_ref, pad_ref = refs[pp:]
    b = pl.program_id(0)
    s = pl.program_id(1)
    last = nchunk - 1
    rpad = SUBLANES

    @pl.when(jnp.logical_and(b == 0, s == 0))
    def _():
        pad_ref[...] = jnp.zeros_like(pad_ref)

    def index_scores(kit):
        s32 = jnp.maximum(_dot(qi_ref[0].astype(BF16), kit.astype(BF16)), 0.0) * wi_ref[0]
        sc = s32[0:rpad]
        for h in range(1, IDX_HEADS):
            sc = sc + s32[h * rpad:(h + 1) * rpad]
        return sc

    for k in range(pp):
        key_ref[0, s * pp + k] = index_scores(kit_refs[k][...])

    @pl.when(s == pl.num_programs(1) - 1)
    def _():
        pad_ref[0:ts, 0:IDX_DIM] = kin_ref[0]
        rr = lax.broadcasted_iota(jnp.int32, (rpad, PAGE), 0)
        cc = lax.broadcasted_iota(jnp.int32, (rpad, PAGE), 1)
        causal = jnp.logical_and(cc <= rr, cc < ts)
        key_ref[0, last] = jnp.where(causal, index_scores(pad_ref[...].T[0:IDX_DIM, :]), -jnp.inf)


def _dsa_index_s(qi32, w32, kin, cache_kit, layer, page_table, ts, pp):
    b = qi32.shape[0]
    npg = page_table.shape[1]
    nchunk = npg + 1

    def page_idx(k):
        return lambda i, s, pt: (layer, pt[i, s * pp + k], 0, 0)

    def seq_idx(i, s, pt):
        return (i, 0, 0)

    in_specs = [pl.BlockSpec((1,) + qi32.shape[1:], seq_idx), pl.BlockSpec((1,) + w32.shape[1:], seq_idx),
                pl.BlockSpec((1, ts, IDX_DIM), seq_idx)]
    in_specs += [pl.BlockSpec((None, None, IDX_DIM, PAGE), page_idx(k)) for k in range(pp)]
    grid_spec = pltpu.PrefetchScalarGridSpec(
        num_scalar_prefetch=1, grid=(b, npg // pp), in_specs=in_specs,
        out_specs=pl.BlockSpec((1, nchunk, SUBLANES, PAGE), lambda i, s, pt: (i, 0, 0, 0)),
        scratch_shapes=[pltpu.VMEM((PAGE, LANES), F32)])
    return pl.pallas_call(
        functools.partial(_dsa_index_s_kernel, ts=ts, pp=pp, nchunk=nchunk), grid_spec=grid_spec,
        out_shape=jax.ShapeDtypeStruct((b, nchunk, SUBLANES, PAGE), F32),
        compiler_params=_cparams(("arbitrary", "arbitrary")), name="dsa_index_sample")(
            page_table, qi32, w32, kin, *([cache_kit] * pp))


def _select_kernel(keys_ref, thr_ref, need_ref, *, n_sel, nchunk):
    r = keys_ref.shape[1]
    wid = min(r, LANES)

    def count(pred, ref):
        out = []
        for lo in range(0, r, wid):
            refv = ref[:, lo:lo + wid]

            def cb(c, acc, lo=lo, refv=refv):
                off = pl.multiple_of(c * PAGE, PAGE)
                return acc + jnp.where(pred(keys_ref[pl.ds(off, PAGE), lo:lo + wid], refv), 1.0, 0.0)
            acc = lax.fori_loop(0, nchunk, cb, jnp.zeros((PAGE, wid), F32))
            out.append(jnp.sum(acc, axis=0, keepdims=True))
        return jnp.concatenate(out, axis=1)

    thr = _kth_largest(lambda trial: count(lambda kk, t: kk >= t, trial), n_sel, (1, r))
    thr_ref[...] = thr
    need_ref[...] = n_sel - count(lambda kk, t: kk > t, thr)


def _select(keys_t, n_sel):
    nkeys, r = keys_t.shape
    return pl.pallas_call(
        functools.partial(_select_kernel, n_sel=n_sel, nchunk=nkeys // PAGE),
        out_shape=[jax.ShapeDtypeStruct((1, r), F32), jax.ShapeDtypeStruct((1, r), F32)],
        compiler_params=pltpu.CompilerParams(vmem_limit_bytes=VMEM_LIMIT), name="dsa_select_sample")(keys_t)


def _dsa_attn_s_kernel(pt_ref, qa_ref, kn_ref, vn_ref, key_ref, thr_ref, need_ref, *refs, ts, pp, nchunk):
    kt_refs = refs[:pp]
    vt_refs = refs[pp:2 * pp]
    bias_ref, o_ref, kst_ref, vst_ref, pad_ref = refs[2 * pp:]
    b = pl.program_id(0)
    s = pl.program_id(1)
    last = nchunk - 1
    rpad = SUBLANES

    @pl.when(jnp.logical_and(b == 0, s == 0))
    def _():
        pad_ref[...] = jnp.zeros_like(pad_ref)

    for k in range(pp):
        kst_ref[s * pp + k] = kt_refs[k][...]
        vst_ref[s * pp + k] = vt_refs[k][...]

    @pl.when(s == pl.num_programs(1) - 1)
    def _():
        pad_ref[0, 0:ts, :] = kn_ref[0]
        pad_ref[1, 0:ts, :] = vn_ref[0]
        kst_ref[last] = pad_ref[0].T
        vst_ref[last] = pad_ref[1].T
        rr = lax.broadcasted_iota(jnp.int32, (rpad, PAGE), 0)
        cc = lax.broadcasted_iota(jnp.int32, (rpad, PAGE), 1)
        causal = jnp.logical_and(cc <= rr, cc < ts)
        thr = thr_ref[0]
        need = need_ref[0]
        r2 = lax.broadcasted_iota(jnp.int32, (PAGE, PAGE), 0)
        c2 = lax.broadcasted_iota(jnp.int32, (PAGE, PAGE), 1)
        ustrict = jnp.where(r2 < c2, 1.0, 0.0).astype(BF16)
        nrow = A_HEADS * rpad
        qbd = qa_ref[0].astype(BF16)
        carry = jnp.zeros((rpad, 1), F32)
        zs = []
        for c in range(nchunk):
            kk = key_ref[0, c]
            eq = kk == thr
            eqf = jnp.where(eq, 1.0, 0.0)
            prefix = _dot(eqf.astype(BF16), ustrict) + carry
            sel = jnp.logical_or(kk > thr, jnp.logical_and(eq, prefix < need))
            if c == last:
                sel = jnp.logical_and(sel, causal)
            am = jnp.where(sel, 0.0, NEG)
            carry = carry + jnp.sum(eqf, axis=1, keepdims=True)
            zs.append(_dot(qbd, kst_ref[c].astype(BF16)) + bias_ref[c] + jnp.concatenate([am] * A_HEADS, axis=0))
        zmax = zs[0]
        for c in range(1, nchunk):
            zmax = jnp.maximum(zmax, zs[c])
        m_row = jnp.max(zmax, axis=1, keepdims=True)
        psum = jnp.zeros((nrow, LANES), F32)
        acc = jnp.zeros((nrow, LANES), F32)
        for c in range(nchunk):
            p = jnp.exp(zs[c] - m_row)
            psum = psum + p
            acc = acc + _dot_nt(p.astype(BF16), vst_ref[c].astype(BF16))
        o = acc / jnp.sum(psum, axis=1, keepdims=True)
        lane = lax.broadcasted_iota(jnp.int32, (rpad, LANES), 1)
        for blk in range(A_HEADS // 2):
            grp = (2 * blk) // A_GROUP
            lo = o[(2 * blk) * rpad:(2 * blk + 1) * rpad]
            hi = o[(2 * blk + 1) * rpad:(2 * blk + 2) * rpad]
            if grp == 0:
                hi = pltpu.roll(hi, HEAD_DIM, axis=1)
            else:
                lo = pltpu.roll(lo, HEAD_DIM, axis=1)
            res = jnp.where(lane < HEAD_DIM, lo, hi)
            o_ref[0, :, blk * LANES:(blk + 1) * LANES] = res[0:ts]


def _dsa_attn_s(qa_bd, kn, vn, keys, thr, need, cache_kt, cache_vt, layer, page_table, bias, ts, pp):
    b = qa_bd.shape[0]
    npg = page_table.shape[1]
    nchunk = npg + 1
    width = A_HEADS * HEAD_DIM

    def page_idx(k):
        return lambda i, s, pt: (layer, pt[i, s * pp + k], 0, 0)

    def seq_idx(i, s, pt):
        return (i, 0, 0)

    in_specs = [pl.BlockSpec((1,) + qa_bd.shape[1:], seq_idx),
                pl.BlockSpec((1, ts, LANES), seq_idx), pl.BlockSpec((1, ts, LANES), seq_idx),
                pl.BlockSpec((1, nchunk, SUBLANES, PAGE), lambda i, s, pt: (i, 0, 0, 0)),
                pl.BlockSpec((1, SUBLANES, PAGE), seq_idx), pl.BlockSpec((1, SUBLANES, PAGE), seq_idx)]
    in_specs += [pl.BlockSpec((None, None, LANES, PAGE), page_idx(k)) for k in range(pp)] * 2
    in_specs += [pl.BlockSpec(bias.shape, lambda i, s, pt: (0, 0, 0))]
    grid_spec = pltpu.PrefetchScalarGridSpec(
        num_scalar_prefetch=1, grid=(b, npg // pp), in_specs=in_specs,
        out_specs=pl.BlockSpec((1, ts, width), seq_idx),
        scratch_shapes=[pltpu.VMEM((nchunk, LANES, PAGE), F32), pltpu.VMEM((nchunk, LANES, PAGE), F32),
                        pltpu.VMEM((2, PAGE, LANES), F32)])
    return pl.pallas_call(
        functools.partial(_dsa_attn_s_kernel, ts=ts, pp=pp, nchunk=nchunk), grid_spec=grid_spec,
        out_shape=jax.ShapeDtypeStruct((b, ts, width), F32),
        compiler_params=_cparams(("arbitrary", "arbitrary")), name="attn_dsa_sample")(
            page_table, qa_bd, kn, vn, keys, thr, need, *([cache_kt] * pp), *([cache_vt] * pp), bias)


def _bucket_starts():
    exact = N_BUCKETS // 2
    starts = list(range(exact))
    for j in range(N_BUCKETS - exact):
        starts.append(min(n for n in range(exact, MAX_DISTANCE + 1)
                          if int(math.log(n / exact) / math.log(MAX_DISTANCE / exact) * (N_BUCKETS - exact)) >= j))
    return starts


def _bias_of(tab, rel):
    n = jnp.maximum(rel, 0)
    bucket = jnp.zeros(rel.shape, jnp.int32)
    for start in _bucket_starts()[1:]:
        bucket = bucket + (n >= start).astype(jnp.int32)
    ones = (1,) * rel.ndim
    hit = bucket[None, None] == jnp.arange(N_BUCKETS, dtype=jnp.int32).reshape((N_BUCKETS, 1) + ones)
    return jnp.sum(jnp.where(hit, tab.astype(F32).reshape(tab.shape + ones), 0.0), axis=0)


def _bias_tiles_t(tab, tb):
    j = jnp.arange(tb, dtype=jnp.int32)[:, None]
    i = jnp.arange(tb, dtype=jnp.int32)[None, :]
    return _bias_of(tab, jnp.stack([d * tb + i - j for d in range(3)]))


def _split(w, sizes):
    offs = np.cumsum((0,) + tuple(sizes))
    return [w[:, int(offs[i]):int(offs[i + 1])] for i in range(len(sizes))]


def _pad_cols(w, n):
    return jnp.pad(w, ((0, 0), (0, n - w.shape[1])))


def _row_tile(m, want):
    t = min(m, want)
    while m % t:
        t //= 2
    return t


def _pages_per_step(npg, want):
    pp = min(npg, want)
    while npg % pp:
        pp -= 1
    return pp


def kernel(x_prompt, x_sample, cache_a_k, cache_a_v, cache_a_kidx, cache_b_k, cache_b_v, cache_c_k, cache_c_v,
           cache_c_logf, state_conv, page_table, rel_bias_table, w_in_even, w_out_even, lambda_q1, lambda_k1,
           lambda_q2, lambda_k2, diff_subln, w_in_odd, b_forget, w_out_odd, norm_mix_pre, norm_mix_post,
           norm_ffn_pre, norm_ffn_post, w_gate_up, w_conv, b_conv, w_down):
    bp, tp, d = x_prompt.shape
    bs, ts, _ = x_sample.shape
    depth = w_gate_up.shape[0]
    n_pool = cache_a_k.shape[1]
    npg = page_table.shape[1]
    past = npg * PAGE
    mp, ms = bp * tp, bs * ts
    tb_dsa = LANES
    tb_attn = min(512, tp)
    nsel_p = min(TOPK_MAX, tp // 4)
    nsel_s = min(TOPK_MAX, (past + ts) // 4)
    tm_p = _row_tile(mp, 512)
    tm_s = _row_tile(ms, 512)
    tf = D_FF
    dff = D_FF
    pp = _pages_per_step(npg, 8)

    xp = x_prompt.reshape(mp, d)
    xs = x_sample.reshape(ms, d)
    tab_a = rel_bias_table[:, :A_HEADS]
    tab_b = rel_bias_table[:, A_HEADS:]
    bias_a_p = _bias_tiles_t(tab_a, tb_dsa)
    bias_a_p = jnp.transpose(bias_a_p, (1, 2, 0, 3)).reshape(3, tb_dsa, A_HEADS * tb_dsa)
    bias_b_p = _bias_tiles_t(tab_b, tb_attn)
    rel_s = (past + jnp.arange(ts, dtype=jnp.int32))[:, None] - jnp.arange(past + PAGE, dtype=jnp.int32)[None, :]
    rows_a = _bias_of(tab_a, rel_s).reshape(A_HEADS, ts, npg + 1, PAGE)
    rows_b = _bias_of(tab_b, rel_s).reshape(B_HEADS, ts, npg + 1, PAGE)
    bias_a_s = jnp.pad(jnp.moveaxis(rows_a, 2, 0), ((0, 0), (0, 0), (0, SUBLANES - ts), (0, 0)))
    bias_a_s = bias_a_s.reshape(npg + 1, A_HEADS * SUBLANES, PAGE)
    kq_ok = rel_s.reshape(ts, npg + 1, PAGE) >= 0
    same = jnp.eye(B_HEADS, dtype=bool)
    bias_b_s = jnp.where(same[:, None, None, None, :] & kq_ok[None, :, :, :, None],
                         rows_b[..., None], NEG)
    bias_b_s = jnp.moveaxis(bias_b_s, 2, 0).reshape(npg + 1, B_HEADS * ts, PAGE * B_HEADS)
    bias_b_s = jnp.concatenate([bias_b_s, bias_b_s], axis=1)
    bias_b_new = bias_b_s[npg]
    bias_b_s = bias_b_s[:npg].reshape(npg // pp, pp, 2 * B_HEADS * ts, PAGE * B_HEADS)
    bias_b_s = jnp.swapaxes(bias_b_s, 1, 2).reshape(npg // pp, 2 * B_HEADS * ts, pp * PAGE * B_HEADS)

    ca_kt = jnp.transpose(cache_a_k, (0, 1, 3, 4, 2)).reshape(-1, n_pool, LANES, PAGE)
    ca_vt = jnp.transpose(cache_a_v, (0, 1, 3, 4, 2)).reshape(-1, n_pool, LANES, PAGE)
    ca_kit = jnp.transpose(cache_a_kidx, (0, 1, 3, 2))
    cb_k = cache_b_k.reshape(-1, n_pool, PAGE * B_HEADS, 2 * HEAD_DIM)
    cb_v = cache_b_v.reshape(-1, n_pool, PAGE * B_HEADS, B_VDIM)
    cc_kt = jnp.transpose(cache_c_k, (0, 1, 3, 4, 2)).reshape(-1, n_pool, C_HEADS * HEAD_DIM, PAGE)
    cc_vt = jnp.transpose(cache_c_v, (0, 1, 3, 4, 2)).reshape(-1, n_pool, C_HEADS * HEAD_DIM, PAGE)
    cc_lft = jnp.transpose(cache_c_logf, (0, 1, 3, 2))

    even_rows_p, even_rows_s, odd_rows_p, odd_rows_s, conv_p, conv_s = [], [], [], [], [], []
    for l in range(depth):
        if l % 2 == 0:
            e = l // 2
            lam_init = 0.8 - 0.6 * math.exp(-0.3 * l)
            ws = _split(w_in_even[e], EVEN_SPLIT)
            ws[5] = _pad_cols(ws[5], LANES)
            ws = [w.astype(BF16) for w in ws]
            wo = w_out_even[e].astype(BF16)
            wo_parts = [wo[:A_HEADS * HEAD_DIM], wo[A_HEADS * HEAD_DIM:]]
            lamp = jnp.stack([lambda_q1[e], lambda_k1[e], lambda_q2[e], lambda_k2[e]]).astype(F32)
            subln = diff_subln[e].reshape(1, B_VDIM)

            wq_a, wk_a, wv_a, wq_i, wk_i, ww_i, wq_b, wk_b, wv_b = ws
            items = [(wq_a, "f32"), (wk_a, "bf16"), (wk_a.T, "col"), (wv_a.T, "col"), (wq_i, "f32"),
                     (wk_i, "bf16"), (wk_i.T, "col"), (ww_i, "f32"), (wq_b, "f32"), (wk_b, "f32"), (wv_b, "f32")]
            q_a, k_a16, k_at, v_at, q_i, k_i16, k_it, w_i, q_b, k_b, v_b = _rms_proj(
                xp, norm_mix_pre[l], items, tm_p, seq=(bp, tp))
            r3 = lambda a: a.reshape(bp, tp, a.shape[1])
            o_a = _dsa_p(r3(q_a), r3(k_a16), v_at, r3(q_i), r3(k_i16), r3(w_i), bias_a_p, nsel_p)
            o_b = _flash2("diff", r3(q_b), r3(k_b), r3(v_b), (bias_b_p, lamp, subln), tb_attn, lam_init)
            xp = _proj_post(xp, norm_mix_post[l], [o_a.reshape(mp, -1), o_b.reshape(mp, -1)], wo_parts, tm_p)
            kv_rows = lambda a: jnp.transpose(a.reshape(bp, A_KV_HEADS, HEAD_DIM, tp), (0, 3, 1, 2))
            even_rows_p.append((kv_rows(k_at), kv_rows(v_at), jnp.transpose(k_it, (0, 2, 1)),
                                k_b.reshape(bp, tp, B_HEADS, 2 * HEAD_DIM), v_b.reshape(bp, tp, B_HEADS, B_VDIM)))

            q_a, k_a, v_a, q_i, k_i, w_i, q_b, k_b, v_b = _rms_proj(
                xs, norm_mix_pre[l], [(w, "f32") for w in ws], tm_s)
            s3 = lambda a: a.reshape(bs, ts, a.shape[1])
            qa4 = (q_a * (HEAD_DIM ** -0.5)).reshape(bs, ts, A_HEADS, HEAD_DIM)
            qa4 = jnp.pad(jnp.moveaxis(qa4, 1, 2), ((0, 0), (0, 0), (0, SUBLANES - ts), (0, 0)))
            zeros = jnp.zeros_like(qa4)
            qa_bd = jnp.concatenate([jnp.concatenate([qa4[:, :A_GROUP], zeros[:, :A_GROUP]], axis=-1),
                                     jnp.concatenate([zeros[:, A_GROUP:], qa4[:, A_GROUP:]], axis=-1)], axis=1)
            qa_bd = qa_bd.reshape(bs, A_HEADS * SUBLANES, LANES)
            qi4 = jnp.moveaxis(q_i.reshape(bs, ts, IDX_HEADS, IDX_DIM), 1, 2)
            qi32 = jnp.pad(qi4, ((0, 0), (0, 0), (0, SUBLANES - ts), (0, 0))).reshape(bs, IDX_HEADS * SUBLANES, IDX_DIM)
            w4 = jnp.moveaxis(w_i[:, :IDX_HEADS].reshape(bs, ts, IDX_HEADS), 1, 2) * (IDX_DIM ** -0.5 * IDX_HEADS ** -0.5)
            w32 = jnp.pad(w4, ((0, 0), (0, 0), (0, SUBLANES - ts))).reshape(bs, IDX_HEADS * SUBLANES, 1)
            w32 = jnp.broadcast_to(w32, (bs, IDX_HEADS * SUBLANES, PAGE))
            keys = _dsa_index_s(qi32, w32, s3(k_i), ca_kit, e, page_table, ts, _pages_per_step(npg, 16))
            keys_t = jnp.transpose(keys, (1, 3, 0, 2)).reshape((npg + 1) * PAGE, bs * SUBLANES)
            thr, need = _select(keys_t, nsel_s)
            thr = jnp.broadcast_to(thr.reshape(bs, SUBLANES, 1), (bs, SUBLANES, PAGE))
            need = jnp.broadcast_to(need.reshape(bs, SUBLANES, 1), (bs, SUBLANES, PAGE))
            o_a = _dsa_attn_s(qa_bd, s3(k_a), s3(v_a), keys, thr, need, ca_kt, ca_vt,
                              e, page_table, bias_a_s, ts, pp)
            o_b = _diff_s(s3(q_b), k_b.reshape(bs, ts * B_HEADS, 2 * HEAD_DIM), v_b.reshape(bs, ts * B_HEADS, B_VDIM),
                          cb_k, cb_v, e, page_table, bias_b_s, bias_b_new, lamp, subln, lam_init, pp)
            xs = _proj_post(xs, norm_mix_post[l], [o_a.reshape(ms, -1), o_b.reshape(ms, -1)], wo_parts, tm_s)
            even_rows_s.append((k_a.reshape(bs, ts, A_KV_HEADS, HEAD_DIM), v_a.reshape(bs, ts, A_KV_HEADS, HEAD_DIM),
                                k_i.reshape(bs, ts, IDX_DIM), k_b.reshape(bs, ts, B_HEADS, 2 * HEAD_DIM),
                                v_b.reshape(bs, ts, B_HEADS, B_VDIM)))
        else:
            o = l // 2
            ws = _split(w_in_odd[o], ODD_SPLIT)
            ws[3] = _pad_cols(ws[3], LANES)
            ws = [w.astype(BF16) for w in ws]
            bfp = _pad_cols(b_forget[o].reshape(1, C_HEADS), LANES)
            wo = [w_out_odd[o].astype(BF16)]

            wq, wk, wv, wf = ws
            items = [(wq, "f32"), (wk, "bf16"), (wk.T, "col"), (wv.T, "col"), (wf, "f32")]
            q, k16, k_t, v_t, lf = _rms_proj(xp, norm_mix_pre[l], items, tm_p, logsig_bias=bfp, seq=(bp, tp))
            r3 = lambda a: a.reshape(bp, tp, a.shape[1])
            cum, cumt = _cumsum(r3(lf), tb_attn)
            o_c = _flash2("fox", r3(q), r3(k16), v_t, (cum, cumt), tb_attn)
            xp = _proj_post(xp, norm_mix_post[l], [o_c.reshape(mp, -1)], wo, tm_p)
            kv_rows = lambda a: jnp.transpose(a.reshape(bp, C_HEADS, HEAD_DIM, tp), (0, 3, 1, 2))
            odd_rows_p.append((kv_rows(k_t), kv_rows(v_t), lf[:, :C_HEADS].reshape(bp, tp, C_HEADS)))

            q, k, v, lf = _rms_proj(xs, norm_mix_pre[l], [(w, "f32") for w in ws], tm_s, logsig_bias=bfp)
            s3 = lambda a: a.reshape(bs, ts, a.shape[1])
            o_c = _fox_s(s3(q), s3(k), s3(v), s3(lf), cc_kt, cc_vt, cc_lft, o, page_table, pp)
            xs = _proj_post(xs, norm_mix_post[l], [o_c.reshape(ms, -1)], wo, tm_s)
            odd_rows_s.append((k.reshape(bs, ts, C_HEADS, HEAD_DIM), v.reshape(bs, ts, C_HEADS, HEAD_DIM),
                               lf[:, :C_HEADS].reshape(bs, ts, C_HEADS)))

        wgu = w_gate_up[l].astype(BF16)
        wg, wu = wgu[:, :dff], wgu[:, dff:]
        wd = w_down[l].astype(BF16)
        tiles = tp // tm_p
        xp, tail = _ffn(xp, norm_ffn_pre[l], norm_ffn_post[l], wg, wu, w_conv[l], b_conv[l], wd,
                        tm_p, tf, tiles)
        conv_p.append(tail.reshape(bp, tiles, 8, dff)[:, tiles - 1, 8 - (CONV_W - 1):, :])
        st = state_conv[l]
        prev1 = jnp.concatenate([st[:, 1:2], jnp.zeros((bs, ts - 1, dff), F32)], axis=1).reshape(ms, dff)
        prev2 = jnp.concatenate([st, jnp.zeros((bs, ts - 2, dff), F32)], axis=1).reshape(ms, dff)
        xs, gfull = _ffn(xs, norm_ffn_pre[l], norm_ffn_post[l], wg, wu, w_conv[l], b_conv[l], wd,
                         tm_s, tf, ts, prev=(prev1, prev2))
        conv_s.append(gfull.reshape(bs, ts, dff)[:, ts - (CONV_W - 1):, :])

    def stack(rows, i):
        return jnp.stack([r[i] for r in rows])

    ev_p = [stack(even_rows_p, i) for i in range(5)]
    ev_s = [stack(even_rows_s, i) for i in range(5)]
    od_p = [stack(odd_rows_p, i) for i in range(3)]
    od_s = [stack(odd_rows_s, i) for i in range(3)]
    return (xp.reshape(bp, tp, d), xs.reshape(bs, ts, d),
            ev_p[0], ev_s[0], ev_p[1], ev_s[1], ev_p[2], ev_s[2], ev_p[3], ev_s[3], ev_p[4], ev_s[4],
            od_p[0], od_s[0], od_p[1], od_s[1], od_p[2], od_s[2],
            jnp.stack(conv_p), jnp.stack(conv_s))
```

```python
import functools
import math

import jax
import jax.numpy as jnp
import numpy as np
from jax import lax
from jax.experimental import pallas as pl
from jax.experimental.pallas import tpu as pltpu

F32 = jnp.float32
BF16 = jnp.bfloat16
HIGHEST = lax.Precision.HIGHEST

D_MODEL = 1024
HEAD_DIM = 64
A_HEADS = 8
A_KV_HEADS = 2
A_GROUP = A_HEADS // A_KV_HEADS
IDX_HEADS = 4
IDX_DIM = 64
TOPK_MAX = 256
B_HEADS = 4
B_VDIM = 128
C_HEADS = 16
N_BUCKETS = 32
MAX_DISTANCE = 128
D_FF = 2816
CONV_W = 3
EPS = 1e-6
PAGE = 128

LANES = 128
SUBLANES = 8
NEG = -1e30
INT_MIN = -2147483648
NEG_INF_CODE = 0x007FFFFF
VMEM_LIMIT = 48 * 1024 * 1024

EVEN_SPLIT = (512, 128, 128, 256, 64, 4, 512, 512, 512)
ODD_SPLIT = (1024, 1024, 1024, 16)

NT_DIMS = (((1,), (1,)), ((), ()))


def _cparams(sem):
    return pltpu.CompilerParams(dimension_semantics=sem, vmem_limit_bytes=VMEM_LIMIT)


def _rms(x, g):
    return x * lax.rsqrt(jnp.mean(x * x, axis=-1, keepdims=True) + EPS) * g


def _log_sigmoid(x):
    return -(jnp.maximum(-x, 0.0) + jnp.log1p(jnp.exp(-jnp.abs(x))))


def _gelu_tanh(x):
    c = math.sqrt(2.0 / math.pi)
    return x * (0.5 * (1.0 + jnp.tanh(c * (x + 0.044715 * (x * x * x)))))


def _dot(a, b):
    return jnp.dot(a, b, preferred_element_type=F32)


def _dot_nt(a, b):
    return lax.dot_general(a, b, NT_DIMS, preferred_element_type=F32)


def _dot_hi(a, b):
    return jnp.dot(a, b, preferred_element_type=F32, precision=HIGHEST)


def _softmax_rows(z, m_old, l_old, acc_old, pv):
    m_new = jnp.maximum(m_old, jnp.max(z, axis=1, keepdims=True))
    alpha = jnp.exp(m_old - m_new)
    p = jnp.exp(z - m_new)
    l_new = alpha * l_old + jnp.sum(p, axis=1, keepdims=True)
    acc_new = alpha * acc_old + pv(p.astype(BF16))
    return m_new, l_new, acc_new


def _softmax_cols(zt, m_old, l_old, acc_old, vt, shift=None):
    m_tile = jnp.max(zt, axis=0, keepdims=True)
    if shift is None:
        m_new = jnp.maximum(m_old, m_tile)
        p = jnp.exp(zt - m_new)
    else:
        m_new = jnp.maximum(m_old, m_tile + shift)
        p = jnp.exp(zt - (m_new - shift))
    alpha = jnp.exp(m_old - m_new)
    l_new = alpha * l_old + jnp.sum(p, axis=0, keepdims=True)
    acc_new = alpha * acc_old + _dot(vt, p.astype(BF16))
    return m_new, l_new, acc_new


def _ordered_value(code):
    key = code ^ jnp.int32(INT_MIN)
    return pltpu.bitcast(jnp.where(key < 0, key ^ jnp.int32(0x7FFFFFFF), key), F32)


def _kth_largest(count_ge, n_sel, shape):
    def below_neg_inf(code):
        return jnp.logical_and(code >= 0, code < jnp.int32(NEG_INF_CODE))

    def step(it, code):
        trial = code | lax.shift_left(jnp.int32(1), jnp.int32(31) - it)
        ok = jnp.logical_or(below_neg_inf(trial), count_ge(_ordered_value(trial)) >= n_sel)
        return jnp.where(ok, trial, code)

    code = lax.fori_loop(0, 32, step, jnp.zeros(shape, jnp.int32))
    return _ordered_value(jnp.where(below_neg_inf(code), jnp.int32(NEG_INF_CODE), code))


def _rms_proj_kernel(x_ref, g_ref, *refs, kinds, logsig_last):
    n_w = len(kinds)
    w_refs = refs[:n_w]
    if logsig_last:
        bias_ref = refs[n_w]
        o_refs = refs[n_w + 1:]
    else:
        o_refs = refs[n_w:]
    h = _rms(x_ref[...], g_ref[...]).astype(BF16)
    for idx, kind in enumerate(kinds):
        if kind == "col":
            o_refs[idx][0] = _dot_nt(w_refs[idx][...], h)
            continue
        y = _dot(h, w_refs[idx][...])
        if logsig_last and idx == n_w - 1:
            y = _log_sigmoid(y + bias_ref[...])
        o_refs[idx][...] = y.astype(o_refs[idx].dtype)


def _rms_proj(x2d, g, items, tm, logsig_bias=None, seq=None):
    m, d = x2d.shape
    kinds = tuple(kind for _, kind in items)
    in_specs = [pl.BlockSpec((tm, d), lambda i: (i, 0)), pl.BlockSpec((1, d), lambda i: (0, 0))]
    in_specs += [pl.BlockSpec(w.shape, lambda i: (0, 0)) for w, _ in items]
    args = [x2d, g.reshape(1, d)] + [w for w, _ in items]
    if logsig_bias is not None:
        in_specs.append(pl.BlockSpec(logsig_bias.shape, lambda i: (0, 0)))
        args.append(logsig_bias)
    out_shape, out_specs = [], []
    for w, kind in items:
        if kind == "col":
            nb, t = seq
            tiles = t // tm
            out_shape.append(jax.ShapeDtypeStruct((nb, w.shape[0], t), F32))
            out_specs.append(pl.BlockSpec((1, w.shape[0], tm), lambda i, tiles=tiles: (i // tiles, 0, i % tiles)))
        else:
            out_shape.append(jax.ShapeDtypeStruct((m, w.shape[1]), F32 if kind == "f32" else BF16))
            out_specs.append(pl.BlockSpec((tm, w.shape[1]), lambda i: (i, 0)))
    return pl.pallas_call(
        functools.partial(_rms_proj_kernel, kinds=kinds, logsig_last=logsig_bias is not None),
        grid=(m // tm,), in_specs=in_specs, out_specs=out_specs, out_shape=out_shape,
        compiler_params=_cparams(("parallel",)), name="rms_proj")(*args)


def _proj_post_kernel(x_ref, g_ref, *refs, n_in):
    o_refs = refs[:n_in]
    w_refs = refs[n_in:2 * n_in]
    out_ref = refs[2 * n_in]
    acc = None
    for o, w in zip(o_refs, w_refs):
        t = _dot(o[...].astype(BF16), w[...])
        acc = t if acc is None else acc + t
    out_ref[...] = x_ref[...] + _rms(acc, g_ref[...])


def _proj_post(x2d, g, o_list, w_list, tm):
    m, d = x2d.shape
    n_in = len(o_list)
    in_specs = [pl.BlockSpec((tm, d), lambda i: (i, 0)), pl.BlockSpec((1, d), lambda i: (0, 0))]
    in_specs += [pl.BlockSpec((tm, o.shape[1]), lambda i: (i, 0)) for o in o_list]
    in_specs += [pl.BlockSpec(w.shape, lambda i: (0, 0)) for w in w_list]
    return pl.pallas_call(
        functools.partial(_proj_post_kernel, n_in=n_in),
        grid=(m // tm,), in_specs=in_specs, out_specs=pl.BlockSpec((tm, d), lambda i: (i, 0)),
        out_shape=jax.ShapeDtypeStruct((m, d), F32),
        compiler_params=_cparams(("parallel",)), name="proj_post")(x2d, g.reshape(1, d), *o_list, *w_list)


def _ffn_kernel(*refs, tm, tiles_per_seq, sample, sub):
    if sample:
        (x_ref, gpre_ref, gpost_ref, wg_ref, wu_ref, wc_ref, bc_ref, wd_ref, p1_ref, p2_ref,
         y_ref, gout_ref, h_ref, acc_ref, gs_ref) = refs
    else:
        (x_ref, gpre_ref, gpost_ref, wg_ref, wu_ref, wc_ref, bc_ref, wd_ref,
         y_ref, gout_ref, h_ref, acc_ref, gs_ref, halo_ref) = refs
    i = pl.program_id(0)
    c = pl.program_id(1)

    @pl.when(c == 0)
    def _():
        h_ref[...] = _rms(x_ref[...], gpre_ref[...]).astype(BF16)
        acc_ref[...] = jnp.zeros_like(acc_ref)

    h = h_ref[...]
    tf = wg_ref.shape[1]
    if sample:
        gs_ref[0:8, :] = jnp.zeros((8, tf), F32)
    else:
        first = (i % tiles_per_seq) == 0

        @pl.when(first)
        def _():
            gs_ref[0:8, :] = jnp.zeros((8, tf), F32)

        @pl.when(jnp.logical_not(first))
        def _():
            gs_ref[0:8, :] = halo_ref[c]

    if sample:
        t = lax.broadcasted_iota(jnp.int32, (tm, 1), 0) % tiles_per_seq
    part = None
    bounds = [(lo, min(lo + sub, tf)) for lo in range(0, tf, sub)]
    nxt = (_dot(h, wg_ref[:, bounds[0][0]:bounds[0][1]]), _dot(h, wu_ref[:, bounds[0][0]:bounds[0][1]]))
    for idx, (lo, hi) in enumerate(bounds):
        g, u = nxt
        if idx + 1 < len(bounds):
            nlo, nhi = bounds[idx + 1]
            nxt = (_dot(h, wg_ref[:, nlo:nhi]), _dot(h, wu_ref[:, nlo:nhi]))
        gs_ref[8:tm + 8, lo:hi] = g
        g1 = gs_ref[7:tm + 7, lo:hi]
        g2 = gs_ref[6:tm + 6, lo:hi]
        if sample:
            g1 = jnp.where(t >= 1, g1, 0.0) + p1_ref[:, lo:hi]
            g2 = jnp.where(t >= 2, g2, 0.0) + p2_ref[:, lo:hi]
            gout_ref[:, lo:hi] = g
        w = wc_ref[:, lo:hi]
        gc = bc_ref[:, lo:hi] + w[0:1, :] * g2
        gc = gc + w[1:2, :] * g1
        gc = gc + w[2:3, :] * g
        a = _gelu_tanh(gc) * u
        d = _dot(a.astype(BF16), wd_ref[lo:hi, :])
        part = d if part is None else part + d
    acc_ref[...] += part
    if not sample:
        tail = gs_ref[tm:tm + 8, :]
        halo_ref[c] = tail
        gout_ref[0] = tail

    @pl.when(c == pl.num_programs(1) - 1)
    def _():
        y_ref[...] = x_ref[...] + _rms(acc_ref[...], gpost_ref[...])


def _ffn(x2d, gpre, gpost, wg, wu, wc, bc, wd, tm, tf, tiles_per_seq, prev=None):
    m, d = x2d.shape
    dff = wg.shape[1]
    nc = dff // tf
    sample = prev is not None
    wmode = pl.Buffered(1) if nc == 1 else None
    in_specs = [
        pl.BlockSpec((tm, d), lambda i, c: (i, 0)),
        pl.BlockSpec((1, d), lambda i, c: (0, 0)),
        pl.BlockSpec((1, d), lambda i, c: (0, 0)),
        pl.BlockSpec((d, tf), lambda i, c: (0, c), pipeline_mode=wmode),
        pl.BlockSpec((d, tf), lambda i, c: (0, c), pipeline_mode=wmode),
        pl.BlockSpec((CONV_W, tf), lambda i, c: (0, c)),
        pl.BlockSpec((1, tf), lambda i, c: (0, c)),
        pl.BlockSpec((tf, d), lambda i, c: (c, 0), pipeline_mode=wmode),
    ]
    args = [x2d, gpre.reshape(1, d), gpost.reshape(1, d), wg, wu, wc, bc.reshape(1, dff), wd]
    scratch = [pltpu.VMEM((tm, d), BF16), pltpu.VMEM((tm, d), F32), pltpu.VMEM((tm + 8, tf), F32)]
    if sample:
        in_specs += [pl.BlockSpec((tm, tf), lambda i, c: (i, c))] * 2
        args += list(prev)
        gout_shape = jax.ShapeDtypeStruct((m, dff), F32)
        gout_spec = pl.BlockSpec((tm, tf), lambda i, c: (i, c))
    else:
        gout_shape = jax.ShapeDtypeStruct((m // tm, 8, dff), F32)
        gout_spec = pl.BlockSpec((1, 8, tf), lambda i, c: (i, 0, c))
        scratch.append(pltpu.VMEM((nc, 8, tf), F32))
    return pl.pallas_call(
        functools.partial(_ffn_kernel, tm=tm, tiles_per_seq=tiles_per_seq, sample=sample, sub=6 * LANES),
        grid=(m // tm, nc), in_specs=in_specs,
        out_specs=[pl.BlockSpec((tm, d), lambda i, c: (i, 0)), gout_spec],
        out_shape=[jax.ShapeDtypeStruct((m, d), F32), gout_shape],
        scratch_shapes=scratch,
        compiler_params=_cparams(("arbitrary", "arbitrary")), name="conv_ffn")(*args)


def _cumsum_kernel(lf_ref, cum_ref, cumt_ref, *, tb, nchunk):
    row = lax.broadcasted_iota(jnp.int32, (tb, tb), 0)
    col = lax.broadcasted_iota(jnp.int32, (tb, tb), 1)
    lower = jnp.where(col <= row, 1.0, 0.0).astype(F32)
    carry = jnp.zeros((1, LANES), F32)
    for c in range(nchunk):
        x = lf_ref[0, c * tb:(c + 1) * tb, :]
        ct = _dot_hi(lower, x) + carry
        cumt_ref[0, c * tb:(c + 1) * tb, :] = ct
        carry = ct[tb - 1:tb, :]
        cum_ref[0, c] = ct.T[0:C_HEADS, :]


def _cumsum(lf3d, tb):
    b, t, _ = lf3d.shape
    nchunk = t // tb
    return pl.pallas_call(
        functools.partial(_cumsum_kernel, tb=tb, nchunk=nchunk),
        grid=(b,), in_specs=[pl.BlockSpec((1, t, LANES), lambda i: (i, 0, 0))],
        out_specs=[pl.BlockSpec((1, nchunk, C_HEADS, tb), lambda i: (i, 0, 0, 0)),
                   pl.BlockSpec((1, t, LANES), lambda i: (i, 0, 0))],
        out_shape=[jax.ShapeDtypeStruct((b, nchunk, C_HEADS, tb), F32),
                   jax.ShapeDtypeStruct((b, t, LANES), F32)],
        compiler_params=_cparams(("parallel",)), name="logf_cumsum")(lf3d)


def _flash2_kernel(*refs, mode, tb, nk, lam_init):
    if mode == "fox":
        q_ref, k_ref, v_ref, cum_ref, cumt_ref, o_ref, vt_ref, ck_ref = refs
    else:
        q_ref, k_ref, v_ref, bias_ref, lamp_ref, subln_ref, o_ref, vt_ref, kb_ref = refs
    g = pl.program_id(1)
    qi = pl.program_id(2)
    lane = lax.broadcasted_iota(jnp.int32, (tb, LANES), 1)
    krow = lax.broadcasted_iota(jnp.int32, (tb, tb), 0)
    qcol = lax.broadcasted_iota(jnp.int32, (tb, tb), 1)
    tri = krow <= qcol

    @pl.when(qi == 0)
    def _():
        if mode == "fox":
            for c in range(nk):
                vt_ref[c] = v_ref[0, :, c * tb:(c + 1) * tb].astype(BF16)
        else:
            kb_ref[...] = k_ref[0].astype(BF16)
            for c in range(nk):
                vt_ref[c] = v_ref[0, c * tb:(c + 1) * tb, :].T.astype(BF16)
        if mode == "fox":
            hrow = lax.broadcasted_iota(jnp.int32, (LANES, LANES), 0)
            for a in range(2):
                onehot = jnp.where(hrow == 2 * g + a, 1.0, 0.0).astype(F32)
                ck_ref[a] = _dot_hi(cumt_ref[0], onehot)

    q = q_ref[0] * (HEAD_DIM ** -0.5)
    qall = jnp.concatenate([jnp.where(lane < HEAD_DIM, q, 0.0), jnp.where(lane >= HEAD_DIM, q, 0.0)],
                           axis=0).astype(BF16)
    if mode == "fox":
        cq = jnp.concatenate([cum_ref[0, qi, pl.ds(2 * g + a, 1), :] for a in range(2)], axis=1)
    rep = tb // LANES

    def qk(c):
        krows = k_ref[0, c * tb:(c + 1) * tb, :] if mode == "fox" else kb_ref[c * tb:(c + 1) * tb, :]
        return _dot_nt(krows, qall)

    def attend(nfull):
        m = jnp.full((1, 2 * tb), NEG, F32)
        l = jnp.zeros((1, 2 * tb), F32)
        acc = jnp.zeros((LANES, 2 * tb), F32)
        zt = qk(0)
        for c in range(nfull + 1):
            zt_next = qk(c + 1) if c < nfull else None
            if mode == "fox":
                ck0 = ck_ref[0, c * tb:(c + 1) * tb, :]
                ck1 = ck_ref[1, c * tb:(c + 1) * tb, :]
                zt = zt - jnp.concatenate([ck0] * rep + [ck1] * rep, axis=1)
                shift = cq
            else:
                bt = bias_ref[0, min(nfull - c, 2)]
                zt = zt + jnp.concatenate([bt, bt], axis=1)
                shift = None
            if c == nfull:
                zt = jnp.where(jnp.concatenate([tri, tri], axis=1), zt, NEG)
            m, l, acc = _softmax_cols(zt, m, l, acc, vt_ref[c], shift)
            zt = zt_next
        o = acc / l
        o0 = o[:, 0:tb].T
        o1 = o[:, tb:2 * tb].T
        if mode == "fox":
            o_ref[0] = jnp.where(lane < HEAD_DIM, o0, o1)
        else:
            lp = lamp_ref[...]
            lam = (jnp.exp(jnp.sum(lp[0:1] * lp[1:2], axis=1, keepdims=True))
                   - jnp.exp(jnp.sum(lp[2:3] * lp[3:4], axis=1, keepdims=True)) + lam_init)
            o_ref[0] = _rms(o0 - lam * o1, subln_ref[...]) * (1.0 - lam_init)

    for blk in range(nk):
        pl.when(qi == blk)(functools.partial(attend, blk))


def _flash2(mode, q, k, v, extra, tb, lam_init=0.0):
    b, t, width = q.shape
    ng = width // LANES
    nq = t // tb
    in_specs = [pl.BlockSpec((1, tb, LANES), lambda i, g, j: (i, j, g)),
                pl.BlockSpec((1, t, LANES), lambda i, g, j: (i, 0, g)),
                pl.BlockSpec((1, t, LANES), lambda i, g, j: (i, 0, g))]
    scratch = [pltpu.VMEM((nq, LANES, tb), BF16)]
    if mode == "fox":
        cum, cumt = extra
        in_specs[2] = pl.BlockSpec((1, LANES, t), lambda i, g, j: (i, g, 0))
        in_specs += [pl.BlockSpec((1, nq, C_HEADS, tb), lambda i, g, j: (i, 0, 0, 0)),
                     pl.BlockSpec((1, t, LANES), lambda i, g, j: (i, 0, 0))]
        scratch.append(pltpu.VMEM((2, t, LANES), F32))
    else:
        bias, lamp, subln = extra
        in_specs += [pl.BlockSpec((1, 3, tb, tb), lambda i, g, j: (g, 0, 0, 0)),
                     pl.BlockSpec(lamp.shape, lambda i, g, j: (0, 0)),
                     pl.BlockSpec(subln.shape, lambda i, g, j: (0, 0))]
        scratch.append(pltpu.VMEM((t, LANES), BF16))
    return pl.pallas_call(
        functools.partial(_flash2_kernel, mode=mode, tb=tb, nk=nq, lam_init=lam_init),
        grid=(b, ng, nq), in_specs=in_specs,
        out_specs=pl.BlockSpec((1, tb, LANES), lambda i, g, j: (i, j, g)),
        out_shape=jax.ShapeDtypeStruct((b, t, width), F32),
        scratch_shapes=scratch,
        compiler_params=_cparams(("arbitrary", "arbitrary", "arbitrary")),
        name="attn_" + mode)(q, k, v, *extra)


def _dsa_p_kernel(qa_ref, ka_ref, vat_in_ref, qi_ref, ki_ref, wi_ref, bias_ref, o_ref,
                  vat_ref, key_ref, qm_ref, *, n_sel, nk):
    tb = LANES
    qi = pl.program_id(1)
    krow = lax.broadcasted_iota(jnp.int32, (tb, tb), 0)
    qcol = lax.broadcasted_iota(jnp.int32, (tb, tb), 1)
    tri = krow <= qcol

    @pl.when(qi == 0)
    def _():
        for c in range(nk // 2):
            vat_ref[c] = vat_in_ref[0, :, c * 2 * tb:(c + 1) * 2 * tb].astype(BF16)

    qidx = qi_ref[0]
    qh = jnp.concatenate([qidx[:, h * IDX_DIM:(h + 1) * IDX_DIM] for h in range(IDX_HEADS)],
                         axis=0).astype(BF16)
    wt = (wi_ref[0] * (IDX_DIM ** -0.5 * IDX_HEADS ** -0.5)).T
    wall = jnp.concatenate([wt[h:h + 1, :] for h in range(IDX_HEADS)], axis=1)

    npairs = (qi + 2) // 2

    def chunk_valid(c):
        return jnp.logical_or(c < qi, jnp.logical_and(c == qi, tri))

    def p1(p, carry):
        off = pl.multiple_of(p * 2 * tb, 2 * tb)
        sh = jnp.maximum(_dot_nt(ki_ref[0, pl.ds(off, 2 * tb), :], qh), 0.0) * wall
        s = sh[:, 0:tb]
        for h in range(1, IDX_HEADS):
            s = s + sh[:, h * tb:(h + 1) * tb]
        valid = jnp.concatenate([chunk_valid(2 * p), chunk_valid(2 * p + 1)], axis=0)
        key_ref[pl.ds(2 * p, 2)] = jnp.where(valid, s, -jnp.inf).reshape(2, tb, tb)
        return carry

    lax.fori_loop(0, npairs, p1, 0)

    def count(pred):
        def cb(p, acc):
            hit = jnp.where(pred(key_ref[pl.ds(2 * p, 2)]), 1.0, 0.0)
            return acc + (hit[0] + hit[1])
        acc = lax.fori_loop(0, npairs, cb, jnp.zeros((tb, tb), F32))
        return jnp.sum(acc, axis=0, keepdims=True)

    thr = _kth_largest(lambda trial: count(lambda kk: kk >= trial), n_sel, (1, tb))
    need = n_sel - count(lambda kk: kk > thr)
    lstrict = jnp.where(qcol < krow, 1.0, 0.0).astype(BF16)

    lane = qcol
    qa = qa_ref[0] * (HEAD_DIM ** -0.5)
    for j in range(A_HEADS):
        blk = qa[:, (j // 2) * LANES:(j // 2 + 1) * LANES]
        grp = j // A_GROUP
        if j % 2 != grp:
            blk = pltpu.roll(blk, HEAD_DIM, axis=1)
        keep = (lane < HEAD_DIM) if grp == 0 else (lane >= HEAD_DIM)
        qm_ref[j] = jnp.where(keep, blk, 0.0).astype(BF16)

    qall = qm_ref[...].reshape(A_HEADS * tb, LANES)

    def chunk_mask(c, carry):
        kk = key_ref[c]
        eq = kk == thr
        eqf = jnp.where(eq, 1.0, 0.0)
        prefix = _dot(lstrict, eqf.astype(BF16)) + carry
        sel = jnp.logical_or(kk > thr, jnp.logical_and(eq, prefix < need))
        sel = jnp.logical_and(sel, chunk_valid(c))
        am = jnp.where(sel, 0.0, NEG)
        return jnp.concatenate([am] * A_HEADS, axis=1), carry + jnp.sum(eqf, axis=0, keepdims=True)

    def qk(p):
        return _dot_nt(ka_ref[0, p * 2 * tb:(p + 1) * 2 * tb, :], qall)

    def attend(count):
        carry = jnp.zeros((1, tb), F32)
        m = jnp.full((1, A_HEADS * tb), NEG, F32)
        l = jnp.zeros((1, A_HEADS * tb), F32)
        acc = jnp.zeros((LANES, A_HEADS * tb), F32)
        zt = qk(0)
        for p in range(count):
            zt_next = qk(p + 1) if p + 1 < count else None
            am_a, carry = chunk_mask(2 * p, carry)
            am_b, carry = chunk_mask(2 * p + 1, carry)
            bias = jnp.concatenate([bias_ref[jnp.clip(qi - 2 * p, 0, 2)],
                                    bias_ref[jnp.clip(qi - 2 * p - 1, 0, 2)]], axis=0)
            zt = zt + bias + jnp.concatenate([am_a, am_b], axis=0)
            m, l, acc = _softmax_cols(zt, m, l, acc, vat_ref[p])
            zt = zt_next
        oall = acc / l
        for c2 in range(A_HEADS // 2):
            grp = (2 * c2) // A_GROUP
            lo = oall[:, (2 * c2) * tb:(2 * c2 + 1) * tb].T
            hi = oall[:, (2 * c2 + 1) * tb:(2 * c2 + 2) * tb].T
            if grp == 0:
                hi = pltpu.roll(hi, HEAD_DIM, axis=1)
            else:
                lo = pltpu.roll(lo, HEAD_DIM, axis=1)
            o_ref[0, :, c2 * LANES:(c2 + 1) * LANES] = jnp.where(lane < HEAD_DIM, lo, hi)

    for count in range(1, nk // 2 + 1):
        pl.when(npairs == count)(functools.partial(attend, count))


def _dsa_p(q_a, k_a, v_at, q_i, k_i, w_i, bias, n_sel):
    b, t, _ = q_a.shape
    tb = LANES
    nq = t // tb
    in_specs = [pl.BlockSpec((1, tb, q_a.shape[2]), lambda i, j: (i, j, 0)),
                pl.BlockSpec((1, t, LANES), lambda i, j: (i, 0, 0)),
                pl.BlockSpec((1, LANES, t), lambda i, j: (i, 0, 0)),
                pl.BlockSpec((1, tb, q_i.shape[2]), lambda i, j: (i, j, 0)),
                pl.BlockSpec((1, t, IDX_DIM), lambda i, j: (i, 0, 0)),
                pl.BlockSpec((1, tb, LANES), lambda i, j: (i, j, 0)),
                pl.BlockSpec(bias.shape, lambda i, j: (0, 0, 0))]
    return pl.pallas_call(
        functools.partial(_dsa_p_kernel, n_sel=n_sel, nk=nq),
        grid=(b, nq), in_specs=in_specs,
        out_specs=pl.BlockSpec((1, tb, q_a.shape[2]), lambda i, j: (i, j, 0)),
        out_shape=jax.ShapeDtypeStruct(q_a.shape, F32),
        scratch_shapes=[pltpu.VMEM((nq // 2, LANES, 2 * tb), BF16),
                        pltpu.VMEM((nq, tb, tb), F32),
                        pltpu.VMEM((A_HEADS, tb, LANES), BF16)],
        compiler_params=_cparams(("arbitrary", "arbitrary")), name="attn_dsa")(q_a, k_a, v_at, q_i, k_i, w_i, bias)


def _fox_s_kernel(pt_ref, q_ref, kn_ref, vn_ref, lfn_ref, *refs, ts, pp):
    kt_refs = refs[:pp]
    vt_refs = refs[pp:2 * pp]
    lft_refs = refs[2 * pp:3 * pp]
    o_ref = refs[3 * pp]
    qbd_ref, kpad_ref, vpad_ref, lfpad_ref, m_ref, l_ref, acc_ref, carry_ref, cq_ref = refs[3 * pp + 1:]
    b = pl.program_id(0)
    s = pl.program_id(1)
    nrow = ts * C_HEADS
    width = C_HEADS * HEAD_DIM
    row = lax.broadcasted_iota(jnp.int32, (PAGE, PAGE), 0)
    col = lax.broadcasted_iota(jnp.int32, (PAGE, PAGE), 1)

    @pl.when(jnp.logical_and(b == 0, s == 0))
    def _():
        kpad_ref[...] = jnp.zeros_like(kpad_ref)
        vpad_ref[...] = jnp.zeros_like(vpad_ref)

    def update(z, pv):
        m_new, l_new, acc_new = _softmax_rows(z, m_ref[...], l_ref[...], acc_ref[...], pv)
        m_ref[...] = m_new
        l_ref[...] = l_new
        acc_ref[...] = acc_new

    @pl.when(s == 0)
    def _():
        hmask = (lax.broadcasted_iota(jnp.int32, (C_HEADS, width), 1) // HEAD_DIM
                 == lax.broadcasted_iota(jnp.int32, (C_HEADS, width), 0))
        q = q_ref[0] * (HEAD_DIM ** -0.5)
        for i in range(ts):
            qbd_ref[i * C_HEADS:(i + 1) * C_HEADS, :] = jnp.where(
                hmask, jnp.broadcast_to(q[i:i + 1, :], (C_HEADS, width)), 0.0).astype(BF16)
        kpad_ref[0:ts, :] = kn_ref[0]
        vpad_ref[0:ts, :] = vn_ref[0]
        lfpad_ref[...] = jnp.zeros_like(lfpad_ref)
        lfpad_ref[0:ts, :] = lfn_ref[0]
        m_ref[...] = jnp.full(m_ref.shape, NEG, F32)
        l_ref[...] = jnp.zeros_like(l_ref)
        acc_ref[...] = jnp.zeros_like(acc_ref)
        carry_ref[...] = jnp.zeros_like(carry_ref)
        lft = lfpad_ref[...].T[0:C_HEADS, :]
        incl = jnp.where(row <= col, 1.0, 0.0).astype(F32)
        cnew = _dot_hi(lft, incl)
        for i in range(ts):
            cq_ref[i * C_HEADS:(i + 1) * C_HEADS, :] = jnp.broadcast_to(cnew[:, i:i + 1], (C_HEADS, PAGE))
        z = _dot_nt(qbd_ref[...], kpad_ref[...].astype(BF16))
        z = z + cq_ref[...] - jnp.concatenate([cnew] * ts, axis=0)
        rr = lax.broadcasted_iota(jnp.int32, (nrow, PAGE), 0)
        cc = lax.broadcasted_iota(jnp.int32, (nrow, PAGE), 1)
        z = jnp.where(cc * C_HEADS <= rr, z, NEG)
        vnew = vpad_ref[...].astype(BF16)
        update(z, lambda p: _dot(p, vnew))

    later = jnp.where(row > col, 1.0, 0.0).astype(F32)
    carry = carry_ref[...]
    sufs = []
    for k in range(pp):
        lft = lft_refs[k][...]
        sufs.append(_dot_hi(lft, later) + carry)
        carry = carry + jnp.sum(lft, axis=1, keepdims=True)
    carry_ref[...] = carry
    suf = jnp.concatenate(sufs, axis=1)
    kt = jnp.concatenate([kt_refs[k][...].astype(BF16) for k in range(pp)], axis=1)
    vt = jnp.concatenate([vt_refs[k][...].astype(BF16) for k in range(pp)], axis=1)
    z = _dot(qbd_ref[...], kt)
    z = z + jnp.concatenate([cq_ref[...]] * pp, axis=1) + jnp.concatenate([suf] * ts, axis=0)
    update(z, lambda p: _dot_nt(p, vt))

    @pl.when(s == pl.num_programs(1) - 1)
    def _():
        hmask = (lax.broadcasted_iota(jnp.int32, (C_HEADS, width), 1) // HEAD_DIM
                 == lax.broadcasted_iota(jnp.int32, (C_HEADS, width), 0))
        o = acc_ref[...] / l_ref[...]
        for i in range(ts):
            blk = jnp.where(hmask, o[i * C_HEADS:(i + 1) * C_HEADS, :], 0.0)
            o_ref[0, i:i + 1, :] = jnp.sum(blk, axis=0, keepdims=True)


def _fox_s(q, kn, vn, lfn, cache_kt, cache_vt, cache_lft, layer, page_table, pp):
    b, ts, width = q.shape
    npg = page_table.shape[1]
    nrow = ts * C_HEADS

    def page_idx(k):
        return lambda i, s, pt: (layer, pt[i, npg - 1 - (s * pp + k)], 0, 0)

    def seq_idx(i, s, pt):
        return (i, 0, 0)

    in_specs = [pl.BlockSpec((1, ts, width), seq_idx), pl.BlockSpec((1, ts, width), seq_idx),
                pl.BlockSpec((1, ts, width), seq_idx), pl.BlockSpec((1, ts, LANES), seq_idx)]
    in_specs += [pl.BlockSpec((None, None, width, PAGE), page_idx(k)) for k in range(pp)]
    in_specs += [pl.BlockSpec((None, None, width, PAGE), page_idx(k)) for k in range(pp)]
    in_specs += [pl.BlockSpec((None, None, C_HEADS, PAGE), page_idx(k)) for k in range(pp)]
    grid_spec = pltpu.PrefetchScalarGridSpec(
        num_scalar_prefetch=1, grid=(b, npg // pp), in_specs=in_specs,
        out_specs=pl.BlockSpec((1, ts, width), seq_idx),
        scratch_shapes=[pltpu.VMEM((nrow, width), BF16), pltpu.VMEM((PAGE, width), F32),
                        pltpu.VMEM((PAGE, width), F32), pltpu.VMEM((PAGE, LANES), F32),
                        pltpu.VMEM((nrow, 1), F32), pltpu.VMEM((nrow, 1), F32),
                        pltpu.VMEM((nrow, width), F32), pltpu.VMEM((C_HEADS, PAGE), F32),
                        pltpu.VMEM((nrow, PAGE), F32)])
    return pl.pallas_call(
        functools.partial(_fox_s_kernel, ts=ts, pp=pp), grid_spec=grid_spec,
        out_shape=jax.ShapeDtypeStruct((b, ts, width), F32),
        compiler_params=_cparams(("arbitrary", "arbitrary")), name="attn_fox_sample")(
            page_table, q, kn, vn, lfn, *([cache_kt] * pp), *([cache_vt] * pp), *([cache_lft] * pp))


def _diff_s_kernel(pt_ref, q_ref, kn_ref, vn_ref, *refs, ts, pp, npg, lam_init):
    k_refs = refs[:pp]
    v_refs = refs[pp:2 * pp]
    (bias_ref, biasn_ref, lamp_ref, subln_ref, o_ref,
     qx_ref, kpad_ref, vpad_ref, m_ref, l_ref, acc_ref) = refs[2 * pp:]
    b = pl.program_id(0)
    s = pl.program_id(1)
    hrows = B_HEADS * ts
    lane = lax.broadcasted_iota(jnp.int32, (ts, LANES), 1)

    @pl.when(jnp.logical_and(b == 0, s == 0))
    def _():
        kpad_ref[...] = jnp.zeros_like(kpad_ref)
        vpad_ref[...] = jnp.zeros_like(vpad_ref)

    @pl.when(s == 0)
    def _():
        q = q_ref[0] * (HEAD_DIM ** -0.5)
        for n in range(B_HEADS):
            blk = q[:, n * LANES:(n + 1) * LANES]
            qx_ref[n * ts:(n + 1) * ts, :] = jnp.where(lane < HEAD_DIM, blk, 0.0)
            qx_ref[hrows + n * ts:hrows + (n + 1) * ts, :] = jnp.where(lane >= HEAD_DIM, blk, 0.0)
        m_ref[...] = jnp.full(m_ref.shape, NEG, F32)
        l_ref[...] = jnp.zeros_like(l_ref)
        acc_ref[...] = jnp.zeros_like(acc_ref)

    def step(kall, vall, bias):
        z = _dot_nt(qx_ref[...].astype(BF16), kall) + bias
        m_new, l_new, acc_new = _softmax_rows(z, m_ref[...], l_ref[...], acc_ref[...], lambda p: _dot(p, vall))
        m_ref[...] = m_new
        l_ref[...] = l_new
        acc_ref[...] = acc_new

    step(jnp.concatenate([k_refs[k][...].astype(BF16) for k in range(pp)], axis=0),
         jnp.concatenate([v_refs[k][...].astype(BF16) for k in range(pp)], axis=0), bias_ref[s])

    @pl.when(s == pl.num_programs(1) - 1)
    def _():
        kpad_ref[0:ts * B_HEADS, :] = kn_ref[0]
        vpad_ref[0:ts * B_HEADS, :] = vn_ref[0]
        step(kpad_ref[...].astype(BF16), vpad_ref[...].astype(BF16), biasn_ref[...])
        lp = lamp_ref[...]
        lam = (jnp.exp(jnp.sum(lp[0:1] * lp[1:2], axis=1, keepdims=True))
               - jnp.exp(jnp.sum(lp[2:3] * lp[3:4], axis=1, keepdims=True)) + lam_init)
        o = acc_ref[...] / l_ref[...]
        od = o[0:hrows] - lam * o[hrows:2 * hrows]
        for n in range(B_HEADS):
            o_ref[0, :, n * B_VDIM:(n + 1) * B_VDIM] = (
                _rms(od[n * ts:(n + 1) * ts], subln_ref[...]) * (1.0 - lam_init))


def _diff_s(q, kn16, vn16, cache_k, cache_v, layer, page_table, bias, bias_new, lamp, subln, lam_init, pp):
    b, ts, width = q.shape
    npg = page_table.shape[1]
    nrow = 2 * B_HEADS * ts
    krows = PAGE * B_HEADS

    def page_idx(k):
        return lambda i, s, pt: (layer, pt[i, s * pp + k], 0, 0)

    def seq_idx(i, s, pt):
        return (i, 0, 0)

    in_specs = [pl.BlockSpec((1, ts, width), seq_idx), pl.BlockSpec((1,) + kn16.shape[1:], seq_idx),
                pl.BlockSpec((1,) + vn16.shape[1:], seq_idx)]
    in_specs += [pl.BlockSpec((None, None, krows, LANES), page_idx(k)) for k in range(pp)] * 2
    in_specs += [pl.BlockSpec(bias.shape, lambda i, s, pt: (0, 0, 0)),
                 pl.BlockSpec(bias_new.shape, lambda i, s, pt: (0, 0)),
                 pl.BlockSpec(lamp.shape, lambda i, s, pt: (0, 0)),
                 pl.BlockSpec(subln.shape, lambda i, s, pt: (0, 0))]
    grid_spec = pltpu.PrefetchScalarGridSpec(
        num_scalar_prefetch=1, grid=(b, npg // pp), in_specs=in_specs,
        out_specs=pl.BlockSpec((1, ts, width), seq_idx),
        scratch_shapes=[pltpu.VMEM((nrow, LANES), F32), pltpu.VMEM((krows, LANES), F32),
                        pltpu.VMEM((krows, LANES), F32), pltpu.VMEM((nrow, 1), F32),
                        pltpu.VMEM((nrow, 1), F32), pltpu.VMEM((nrow, LANES), F32)])
    return pl.pallas_call(
        functools.partial(_diff_s_kernel, ts=ts, pp=pp, npg=npg, lam_init=lam_init), grid_spec=grid_spec,
        out_shape=jax.ShapeDtypeStruct((b, ts, width), F32),
        compiler_params=_cparams(("arbitrary", "arbitrary")), name="attn_diff_sample")(
            page_table, q, kn16, vn16, *([cache_k] * pp), *([cache_v] * pp), bias, bias_new, lamp, subln)


def _dsa_index_s_kernel(pt_ref, qi_ref, wi_ref, kin_ref, *refs, ts, pp, nchunk):
    kit_refs = refs[:pp]
    key_ref, pad_ref = refs[pp:]
    b = pl.program_id(0)
    s = pl.program_id(1)
    last = nchunk - 1
    rpad = SUBLANES

    @pl.when(jnp.logical_and(b == 0, s == 0))
    def _():
        pad_ref[...] = jnp.zeros_like(pad_ref)

    def index_scores(kit):
        s32 = jnp.maximum(_dot(qi_ref[0].astype(BF16), kit.astype(BF16)), 0.0) * wi_ref[0]
        sc = s32[0:rpad]
        for h in range(1, IDX_HEADS):
            sc = sc + s32[h * rpad:(h + 1) * rpad]
        return sc

    for k in range(pp):
        key_ref[0, s * pp + k] = index_scores(kit_refs[k][...])

    @pl.when(s == pl.num_programs(1) - 1)
    def _():
        pad_ref[0:ts, 0:IDX_DIM] = kin_ref[0]
        rr = lax.broadcasted_iota(jnp.int32, (rpad, PAGE), 0)
        cc = lax.broadcasted_iota(jnp.int32, (rpad, PAGE), 1)
        causal = jnp.logical_and(cc <= rr, cc < ts)
        key_ref[0, last] = jnp.where(causal, index_scores(pad_ref[...].T[0:IDX_DIM, :]), -jnp.inf)


def _dsa_index_s(qi32, w32, kin, cache_kit, layer, page_table, ts, pp):
    b = qi32.shape[0]
    npg = page_table.shape[1]
    nchunk = npg + 1

    def page_idx(k):
        return lambda i, s, pt: (layer, pt[i, s * pp + k], 0, 0)

    def seq_idx(i, s, pt):
        return (i, 0, 0)

    in_specs = [pl.BlockSpec((1,) + qi32.shape[1:], seq_idx), pl.BlockSpec((1,) + w32.shape[1:], seq_idx),
                pl.BlockSpec((1, ts, IDX_DIM), seq_idx)]
    in_specs += [pl.BlockSpec((None, None, IDX_DIM, PAGE), page_idx(k)) for k in range(pp)]
    grid_spec = pltpu.PrefetchScalarGridSpec(
        num_scalar_prefetch=1, grid=(b, npg // pp), in_specs=in_specs,
        out_specs=pl.BlockSpec((1, nchunk, SUBLANES, PAGE), lambda i, s, pt: (i, 0, 0, 0)),
        scratch_shapes=[pltpu.VMEM((PAGE, LANES), F32)])
    return pl.pallas_call(
        functools.partial(_dsa_index_s_kernel, ts=ts, pp=pp, nchunk=nchunk), grid_spec=grid_spec,
        out_shape=jax.ShapeDtypeStruct((b, nchunk, SUBLANES, PAGE), F32),
        compiler_params=_cparams(("arbitrary", "arbitrary")), name="dsa_index_sample")(
            page_table, qi32, w32, kin, *([cache_kit] * pp))


def _select_kernel(keys_ref, thr_ref, need_ref, *, n_sel, nchunk):
    r = keys_ref.shape[1]
    wid = min(r, LANES)

    def count(pred, ref):
        out = []
        for lo in range(0, r, wid):
            refv = ref[:, lo:lo + wid]

            def cb(c, acc, lo=lo, refv=refv):
                off = pl.multiple_of(c * PAGE, PAGE)
                return acc + jnp.where(pred(keys_ref[pl.ds(off, PAGE), lo:lo + wid], refv), 1.0, 0.0)
            acc = lax.fori_loop(0, nchunk, cb, jnp.zeros((PAGE, wid), F32))
            out.append(jnp.sum(acc, axis=0, keepdims=True))
        return jnp.concatenate(out, axis=1)

    thr = _kth_largest(lambda trial: count(lambda kk, t: kk >= t, trial), n_sel, (1, r))
    thr_ref[...] = thr
    need_ref[...] = n_sel - count(lambda kk, t: kk > t, thr)


def _select(keys_t, n_sel):
    nkeys, r = keys_t.shape
    return pl.pallas_call(
        functools.partial(_select_kernel, n_sel=n_sel, nchunk=nkeys // PAGE),
        out_shape=[jax.ShapeDtypeStruct((1, r), F32), jax.ShapeDtypeStruct((1, r), F32)],
        compiler_params=pltpu.CompilerParams(vmem_limit_bytes=VMEM_LIMIT), name="dsa_select_sample")(keys_t)


def _dsa_attn_s_kernel(pt_ref, qa_ref, kn_ref, vn_ref, key_ref, thr_ref, need_ref, *refs, ts, pp, nchunk):
    kt_refs = refs[:pp]
    vt_refs = refs[pp:2 * pp]
    bias_ref, o_ref, kst_ref, vst_ref, pad_ref = refs[2 * pp:]
    b = pl.program_id(0)
    s = pl.program_id(1)
    last = nchunk - 1
    rpad = SUBLANES

    @pl.when(jnp.logical_and(b == 0, s == 0))
    def _():
        pad_ref[...] = jnp.zeros_like(pad_ref)

    for k in range(pp):
        kst_ref[s * pp + k] = kt_refs[k][...]
        vst_ref[s * pp + k] = vt_refs[k][...]

    @pl.when(s == pl.num_programs(1) - 1)
    def _():
        pad_ref[0, 0:ts, :] = kn_ref[0]
        pad_ref[1, 0:ts, :] = vn_ref[0]
        kst_ref[last] = pad_ref[0].T
        vst_ref[last] = pad_ref[1].T
        rr = lax.broadcasted_iota(jnp.int32, (rpad, PAGE), 0)
        cc = lax.broadcasted_iota(jnp.int32, (rpad, PAGE), 1)
        causal = jnp.logical_and(cc <= rr, cc < ts)
        thr = thr_ref[0]
        need = need_ref[0]
        r2 = lax.broadcasted_iota(jnp.int32, (PAGE, PAGE), 0)
        c2 = lax.broadcasted_iota(jnp.int32, (PAGE, PAGE), 1)
        ustrict = jnp.where(r2 < c2, 1.0, 0.0).astype(BF16)
        nrow = A_HEADS * rpad
        qbd = qa_ref[0].astype(BF16)
        carry = jnp.zeros((rpad, 1), F32)
        zs = []
        for c in range(nchunk):
            kk = key_ref[0, c]
            eq = kk == thr
            eqf = jnp.where(eq, 1.0, 0.0)
            prefix = _dot(eqf.astype(BF16), ustrict) + carry
            sel = jnp.logical_or(kk > thr, jnp.logical_and(eq, prefix < need))
            if c == last:
                sel = jnp.logical_and(sel, causal)
            am = jnp.where(sel, 0.0, NEG)
            carry = carry + jnp.sum(eqf, axis=1, keepdims=True)
            zs.append(_dot(qbd, kst_ref[c].astype(BF16)) + bias_ref[c] + jnp.concatenate([am] * A_HEADS, axis=0))
        zmax = zs[0]
        for c in range(1, nchunk):
            zmax = jnp.maximum(zmax, zs[c])
        m_row = jnp.max(zmax, axis=1, keepdims=True)
        psum = jnp.zeros((nrow, LANES), F32)
        acc = jnp.zeros((nrow, LANES), F32)
        for c in range(nchunk):
            p = jnp.exp(zs[c] - m_row)
            psum = psum + p
            acc = acc + _dot_nt(p.astype(BF16), vst_ref[c].astype(BF16))
        o = acc / jnp.sum(psum, axis=1, keepdims=True)
        lane = lax.broadcasted_iota(jnp.int32, (rpad, LANES), 1)
        for blk in range(A_HEADS // 2):
            grp = (2 * blk) // A_GROUP
            lo = o[(2 * blk) * rpad:(2 * blk + 1) * rpad]
            hi = o[(2 * blk + 1) * rpad:(2 * blk + 2) * rpad]
            if grp == 0:
                hi = pltpu.roll(hi, HEAD_DIM, axis=1)
            else:
                lo = pltpu.roll(lo, HEAD_DIM, axis=1)
            res = jnp.where(lane < HEAD_DIM, lo, hi)
            o_ref[0, :, blk * LANES:(blk + 1) * LANES] = res[0:ts]


def _dsa_attn_s(qa_bd, kn, vn, keys, thr, need, cache_kt, cache_vt, layer, page_table, bias, ts, pp):
    b = qa_bd.shape[0]
    npg = page_table.shape[1]
    nchunk = npg + 1
    width = A_HEADS * HEAD_DIM

    def page_idx(k):
        return lambda i, s, pt: (layer, pt[i, s * pp + k], 0, 0)

    def seq_idx(i, s, pt):
        return (i, 0, 0)

    in_specs = [pl.BlockSpec((1,) + qa_bd.shape[1:], seq_idx),
                pl.BlockSpec((1, ts, LANES), seq_idx), pl.BlockSpec((1, ts, LANES), seq_idx),
                pl.BlockSpec((1, nchunk, SUBLANES, PAGE), lambda i, s, pt: (i, 0, 0, 0)),
                pl.BlockSpec((1, SUBLANES, PAGE), seq_idx), pl.BlockSpec((1, SUBLANES, PAGE), seq_idx)]
    in_specs += [pl.BlockSpec((None, None, LANES, PAGE), page_idx(k)) for k in range(pp)] * 2
    in_specs += [pl.BlockSpec(bias.shape, lambda i, s, pt: (0, 0, 0))]
    grid_spec = pltpu.PrefetchScalarGridSpec(
        num_scalar_prefetch=1, grid=(b, npg // pp), in_specs=in_specs,
        out_specs=pl.BlockSpec((1, ts, width), seq_idx),
        scratch_shapes=[pltpu.VMEM((nchunk, LANES, PAGE), F32), pltpu.VMEM((nchunk, LANES, PAGE), F32),
                        pltpu.VMEM((2, PAGE, LANES), F32)])
    return pl.pallas_call(
        functools.partial(_dsa_attn_s_kernel, ts=ts, pp=pp, nchunk=nchunk), grid_spec=grid_spec,
        out_shape=jax.ShapeDtypeStruct((b, ts, width), F32),
        compiler_params=_cparams(("arbitrary", "arbitrary")), name="attn_dsa_sample")(
            page_table, qa_bd, kn, vn, keys, thr, need, *([cache_kt] * pp), *([cache_vt] * pp), bias)


def _bucket_starts():
    exact = N_BUCKETS // 2
    starts = list(range(exact))
    for j in range(N_BUCKETS - exact):
        starts.append(min(n for n in range(exact, MAX_DISTANCE + 1)
                          if int(math.log(n / exact) / math.log(MAX_DISTANCE / exact) * (N_BUCKETS - exact)) >= j))
    return starts


def _bias_of(tab, rel):
    n = jnp.maximum(rel, 0)
    bucket = jnp.zeros(rel.shape, jnp.int32)
    for start in _bucket_starts()[1:]:
        bucket = bucket + (n >= start).astype(jnp.int32)
    ones = (1,) * rel.ndim
    hit = bucket[None, None] == jnp.arange(N_BUCKETS, dtype=jnp.int32).reshape((N_BUCKETS, 1) + ones)
    return jnp.sum(jnp.where(hit, tab.astype(F32).reshape(tab.shape + ones), 0.0), axis=0)


def _bias_tiles_t(tab, tb):
    j = jnp.arange(tb, dtype=jnp.int32)[:, None]
    i = jnp.arange(tb, dtype=jnp.int32)[None, :]
    return _bias_of(tab, jnp.stack([d * tb + i - j for d in range(3)]))


def _split(w, sizes):
    offs = np.cumsum((0,) + tuple(sizes))
    return [w[:, int(offs[i]):int(offs[i + 1])] for i in range(len(sizes))]


def _pad_cols(w, n):
    return jnp.pad(w, ((0, 0), (0, n - w.shape[1])))


def _row_tile(m, want):
    t = min(m, want)
    while m % t:
        t //= 2
    return t


def _pages_per_step(npg, want):
    pp = min(npg, want)
    while npg % pp:
        pp -= 1
    return pp


def kernel(x_prompt, x_sample, cache_a_k, cache_a_v, cache_a_kidx, cache_b_k, cache_b_v, cache_c_k, cache_c_v,
           cache_c_logf, state_conv, page_table, rel_bias_table, w_in_even, w_out_even, lambda_q1, lambda_k1,
           lambda_q2, lambda_k2, diff_subln, w_in_odd, b_forget, w_out_odd, norm_mix_pre, norm_mix_post,
           norm_ffn_pre, norm_ffn_post, w_gate_up, w_conv, b_conv, w_down):
    bp, tp, d = x_prompt.shape
    bs, ts, _ = x_sample.shape
    depth = w_gate_up.shape[0]
    n_pool = cache_a_k.shape[1]
    npg = page_table.shape[1]
    past = npg * PAGE
    mp, ms = bp * tp, bs * ts
    tb_dsa = LANES
    tb_attn = min(512, tp)
    nsel_p = min(TOPK_MAX, tp // 4)
    nsel_s = min(TOPK_MAX, (past + ts) // 4)
    tm_p = _row_tile(mp, 512)
    tm_s = _row_tile(ms, 512)
    tf = D_FF
    dff = D_FF
    pp = _pages_per_step(npg, 8)

    xp = x_prompt.reshape(mp, d)
    xs = x_sample.reshape(ms, d)
    tab_a = rel_bias_table[:, :A_HEADS]
    tab_b = rel_bias_table[:, A_HEADS:]
    bias_a_p = _bias_tiles_t(tab_a, tb_dsa)
    bias_a_p = jnp.transpose(bias_a_p, (1, 2, 0, 3)).reshape(3, tb_dsa, A_HEADS * tb_dsa)
    bias_b_p = _bias_tiles_t(tab_b, tb_attn)
    rel_s = (past + jnp.arange(ts, dtype=jnp.int32))[:, None] - jnp.arange(past + PAGE, dtype=jnp.int32)[None, :]
    rows_a = _bias_of(tab_a, rel_s).reshape(A_HEADS, ts, npg + 1, PAGE)
    rows_b = _bias_of(tab_b, rel_s).reshape(B_HEADS, ts, npg + 1, PAGE)
    bias_a_s = jnp.pad(jnp.moveaxis(rows_a, 2, 0), ((0, 0), (0, 0), (0, SUBLANES - ts), (0, 0)))
    bias_a_s = bias_a_s.reshape(npg + 1, A_HEADS * SUBLANES, PAGE)
    kq_ok = rel_s.reshape(ts, npg + 1, PAGE) >= 0
    same = jnp.eye(B_HEADS, dtype=bool)
    bias_b_s = jnp.where(same[:, None, None, None, :] & kq_ok[None, :, :, :, None],
                         rows_b[..., None], NEG)
    bias_b_s = jnp.moveaxis(bias_b_s, 2, 0).reshape(npg + 1, B_HEADS * ts, PAGE * B_HEADS)
    bias_b_s = jnp.concatenate([bias_b_s, bias_b_s], axis=1)
    bias_b_new = bias_b_s[npg]
    bias_b_s = bias_b_s[:npg].reshape(npg // pp, pp, 2 * B_HEADS * ts, PAGE * B_HEADS)
    bias_b_s = jnp.swapaxes(bias_b_s, 1, 2).reshape(npg // pp, 2 * B_HEADS * ts, pp * PAGE * B_HEADS)

    ca_kt = jnp.transpose(cache_a_k, (0, 1, 3, 4, 2)).reshape(-1, n_pool, LANES, PAGE)
    ca_vt = jnp.transpose(cache_a_v, (0, 1, 3, 4, 2)).reshape(-1, n_pool, LANES, PAGE)
    ca_kit = jnp.transpose(cache_a_kidx, (0, 1, 3, 2))
    cb_k = cache_b_k.reshape(-1, n_pool, PAGE * B_HEADS, 2 * HEAD_DIM)
    cb_v = cache_b_v.reshape(-1, n_pool, PAGE * B_HEADS, B_VDIM)
    cc_kt = jnp.transpose(cache_c_k, (0, 1, 3, 4, 2)).reshape(-1, n_pool, C_HEADS * HEAD_DIM, PAGE)
    cc_vt = jnp.transpose(cache_c_v, (0, 1, 3, 4, 2)).reshape(-1, n_pool, C_HEADS * HEAD_DIM, PAGE)
    cc_lft = jnp.transpose(cache_c_logf, (0, 1, 3, 2))

    even_rows_p, even_rows_s, odd_rows_p, odd_rows_s, conv_p, conv_s = [], [], [], [], [], []
    for l in range(depth):
        if l % 2 == 0:
            e = l // 2
            lam_init = 0.8 - 0.6 * math.exp(-0.3 * l)
            ws = _split(w_in_even[e], EVEN_SPLIT)
            ws[5] = _pad_cols(ws[5], LANES)
            ws = [w.astype(BF16) for w in ws]
            wo = w_out_even[e].astype(BF16)
            wo_parts = [wo[:A_HEADS * HEAD_DIM], wo[A_HEADS * HEAD_DIM:]]
            lamp = jnp.stack([lambda_q1[e], lambda_k1[e], lambda_q2[e], lambda_k2[e]]).astype(F32)
            subln = diff_subln[e].reshape(1, B_VDIM)

            wq_a, wk_a, wv_a, wq_i, wk_i, ww_i, wq_b, wk_b, wv_b = ws
            items = [(wq_a, "f32"), (wk_a, "bf16"), (wk_a.T, "col"), (wv_a.T, "col"), (wq_i, "f32"),
                     (wk_i, "bf16"), (wk_i.T, "col"), (ww_i, "f32"), (wq_b, "f32"), (wk_b, "f32"), (wv_b, "f32")]
            q_a, k_a16, k_at, v_at, q_i, k_i16, k_it, w_i, q_b, k_b, v_b = _rms_proj(
                xp, norm_mix_pre[l], items, tm_p, seq=(bp, tp))
            r3 = lambda a: a.reshape(bp, tp, a.shape[1])
            o_a = _dsa_p(r3(q_a), r3(k_a16), v_at, r3(q_i), r3(k_i16), r3(w_i), bias_a_p, nsel_p)
            o_b = _flash2("diff", r3(q_b), r3(k_b), r3(v_b), (bias_b_p, lamp, subln), tb_attn, lam_init)
            xp = _proj_post(xp, norm_mix_post[l], [o_a.reshape(mp, -1), o_b.reshape(mp, -1)], wo_parts, tm_p)
            kv_rows = lambda a: jnp.transpose(a.reshape(bp, A_KV_HEADS, HEAD_DIM, tp), (0, 3, 1, 2))
            even_rows_p.append((kv_rows(k_at), kv_rows(v_at), jnp.transpose(k_it, (0, 2, 1)),
                                k_b.reshape(bp, tp, B_HEADS, 2 * HEAD_DIM), v_b.reshape(bp, tp, B_HEADS, B_VDIM)))

            q_a, k_a, v_a, q_i, k_i, w_i, q_b, k_b, v_b = _rms_proj(
                xs, norm_mix_pre[l], [(w, "f32") for w in ws], tm_s)
            s3 = lambda a: a.reshape(bs, ts, a.shape[1])
            qa4 = (q_a * (HEAD_DIM ** -0.5)).reshape(bs, ts, A_HEADS, HEAD_DIM)
            qa4 = jnp.pad(jnp.moveaxis(qa4, 1, 2), ((0, 0), (0, 0), (0, SUBLANES - ts), (0, 0)))
            zeros = jnp.zeros_like(qa4)
            qa_bd = jnp.concatenate([jnp.concatenate([qa4[:, :A_GROUP], zeros[:, :A_GROUP]], axis=-1),
                                     jnp.concatenate([zeros[:, A_GROUP:], qa4[:, A_GROUP:]], axis=-1)], axis=1)
            qa_bd = qa_bd.reshape(bs, A_HEADS * SUBLANES, LANES)
            qi4 = jnp.moveaxis(q_i.reshape(bs, ts, IDX_HEADS, IDX_DIM), 1, 2)
            qi32 = jnp.pad(qi4, ((0, 0), (0, 0), (0, SUBLANES - ts), (0, 0))).reshape(bs, IDX_HEADS * SUBLANES, IDX_DIM)
            w4 = jnp.moveaxis(w_i[:, :IDX_HEADS].reshape(bs, ts, IDX_HEADS), 1, 2) * (IDX_DIM ** -0.5 * IDX_HEADS ** -0.5)
            w32 = jnp.pad(w4, ((0, 0), (0, 0), (0, SUBLANES - ts))).reshape(bs, IDX_HEADS * SUBLANES, 1)
            w32 = jnp.broadcast_to(w32, (bs, IDX_HEADS * SUBLANES, PAGE))
            keys = _dsa_index_s(qi32, w32, s3(k_i), ca_kit, e, page_table, ts, _pages_per_step(npg, 16))
            keys_t = jnp.transpose(keys, (1, 3, 0, 2)).reshape((npg + 1) * PAGE, bs * SUBLANES)
            thr, need = _select(keys_t, nsel_s)
            thr = jnp.broadcast_to(thr.reshape(bs, SUBLANES, 1), (bs, SUBLANES, PAGE))
            need = jnp.broadcast_to(need.reshape(bs, SUBLANES, 1), (bs, SUBLANES, PAGE))
            o_a = _dsa_attn_s(qa_bd, s3(k_a), s3(v_a), keys, thr, need, ca_kt, ca_vt,
                              e, page_table, bias_a_s, ts, pp)
            o_b = _diff_s(s3(q_b), k_b.reshape(bs, ts * B_HEADS, 2 * HEAD_DIM), v_b.reshape(bs, ts * B_HEADS, B_VDIM),
                          cb_k, cb_v, e, page_table, bias_b_s, bias_b_new, lamp, subln, lam_init, pp)
            xs = _proj_post(xs, norm_mix_post[l], [o_a.reshape(ms, -1), o_b.reshape(ms, -1)], wo_parts, tm_s)
            even_rows_s.append((k_a.reshape(bs, ts, A_KV_HEADS, HEAD_DIM), v_a.reshape(bs, ts, A_KV_HEADS, HEAD_DIM),
                                k_i.reshape(bs, ts, IDX_DIM), k_b.reshape(bs, ts, B_HEADS, 2 * HEAD_DIM),
                                v_b.reshape(bs, ts, B_HEADS, B_VDIM)))
        else:
            o = l // 2
            ws = _split(w_in_odd[o], ODD_SPLIT)
            ws[3] = _pad_cols(ws[3], LANES)
            ws = [w.astype(BF16) for w in ws]
            bfp = _pad_cols(b_forget[o].reshape(1, C_HEADS), LANES)
            wo = [w_out_odd[o].astype(BF16)]

            wq, wk, wv, wf = ws
            items = [(wq, "f32"), (wk, "bf16"), (wk.T, "col"), (wv.T, "col"), (wf, "f32")]
            q, k16, k_t, v_t, lf = _rms_proj(xp, norm_mix_pre[l], items, tm_p, logsig_bias=bfp, seq=(bp, tp))
            r3 = lambda a: a.reshape(bp, tp, a.shape[1])
            cum, cumt = _cumsum(r3(lf), tb_attn)
            o_c = _flash2("fox", r3(q), r3(k16), v_t, (cum, cumt), tb_attn)
            xp = _proj_post(xp, norm_mix_post[l], [o_c.reshape(mp, -1)], wo, tm_p)
            kv_rows = lambda a: jnp.transpose(a.reshape(bp, C_HEADS, HEAD_DIM, tp), (0, 3, 1, 2))
            odd_rows_p.append((kv_rows(k_t), kv_rows(v_t), lf[:, :C_HEADS].reshape(bp, tp, C_HEADS)))

            q, k, v, lf = _rms_proj(xs, norm_mix_pre[l], [(w, "f32") for w in ws], tm_s, logsig_bias=bfp)
            s3 = lambda a: a.reshape(bs, ts, a.shape[1])
            o_c = _fox_s(s3(q), s3(k), s3(v), s3(lf), cc_kt, cc_vt, cc_lft, o, page_table, pp)
            xs = _proj_post(xs, norm_mix_post[l], [o_c.reshape(ms, -1)], wo, tm_s)
            odd_rows_s.append((k.reshape(bs, ts, C_HEADS, HEAD_DIM), v.reshape(bs, ts, C_HEADS, HEAD_DIM),
                               lf[:, :C_HEADS].reshape(bs, ts, C_HEADS)))

        wgu = w_gate_up[l].astype(BF16)
        wg, wu = wgu[:, :dff], wgu[:, dff:]
        wd = w_down[l].astype(BF16)
        tiles = tp // tm_p
        xp, tail = _ffn(xp, norm_ffn_pre[l], norm_ffn_post[l], wg, wu, w_conv[l], b_conv[l], wd,
                        tm_p, tf, tiles)
        conv_p.append(tail.reshape(bp, tiles, 8, dff)[:, tiles - 1, 8 - (CONV_W - 1):, :])
        st = state_conv[l]
        prev1 = jnp.concatenate([st[:, 1:2], jnp.zeros((bs, ts - 1, dff), F32)], axis=1).reshape(ms, dff)
        prev2 = jnp.concatenate([st, jnp.zeros((bs, ts - 2, dff), F32)], axis=1).reshape(ms, dff)
        xs, gfull = _ffn(xs, norm_ffn_pre[l], norm_ffn_post[l], wg, wu, w_conv[l], b_conv[l], wd,
                         tm_s, tf, ts, prev=(prev1, prev2))
        conv_s.append(gfull.reshape(bs, ts, dff)[:, ts - (CONV_W - 1):, :])

    def stack(rows, i):
        return jnp.stack([r[i] for r in rows])

    ev_p = [stack(even_rows_p, i) for i in range(5)]
    ev_s = [stack(even_rows_s, i) for i in range(5)]
    od_p = [stack(odd_rows_p, i) for i in range(3)]
    od_s = [stack(odd_rows_s, i) for i in range(3)]
    return (xp.reshape(bp, tp, d), xs.reshape(bs, ts, d),
            ev_p[0], ev_s[0], ev_p[1], ev_s[1], ev_p[2], ev_s[2], ev_p[3], ev_s[3], ev_p[4], ev_s[4],
            od_p[0], od_s[0], od_p[1], od_s[1], od_p[2], od_s[2],
            jnp.stack(conv_p), jnp.stack(conv_s))
```

```python
import functools
import math

import jax
import jax.numpy as jnp
import numpy as np
from jax import lax
from jax.experimental import pallas as pl
from jax.experimental.pallas import tpu as pltpu

F32 = jnp.float32
BF16 = jnp.bfloat16
HIGHEST = lax.Precision.HIGHEST

D_MODEL = 1024
HEAD_DIM = 64
A_HEADS = 8
A_KV_HEADS = 2
A_GROUP = A_HEADS // A_KV_HEADS
IDX_HEADS = 4
IDX_DIM = 64
TOPK_MAX = 256
B_HEADS = 4
B_VDIM = 128
C_HEADS = 16
N_BUCKETS = 32
MAX_DISTANCE = 128
D_FF = 2816
CONV_W = 3
EPS = 1e-6
PAGE = 128

LANES = 128
SUBLANES = 8
NEG = -1e30
INT_MIN = -2147483648
NEG_INF_CODE = 0x007FFFFF
VMEM_LIMIT = 48 * 1024 * 1024

EVEN_SPLIT = (512, 128, 128, 256, 64, 4, 512, 512, 512)
ODD_SPLIT = (1024, 1024, 1024, 16)

NT_DIMS = (((1,), (1,)), ((), ()))


def _cparams(sem):
    return pltpu.CompilerParams(dimension_semantics=sem, vmem_limit_bytes=VMEM_LIMIT)


def _rms(x, g):
    return x * lax.rsqrt(jnp.mean(x * x, axis=-1, keepdims=True) + EPS) * g


def _log_sigmoid(x):
    return -(jnp.maximum(-x, 0.0) + jnp.log1p(jnp.exp(-jnp.abs(x))))


def _gelu_tanh(x):
    c = math.sqrt(2.0 / math.pi)
    return x * (0.5 * (1.0 + jnp.tanh(c * (x + 0.044715 * (x * x * x)))))


def _dot(a, b):
    return jnp.dot(a, b, preferred_element_type=F32)


def _dot_nt(a, b):
    return lax.dot_general(a, b, NT_DIMS, preferred_element_type=F32)


def _dot_hi(a, b):
    return jnp.dot(a, b, preferred_element_type=F32, precision=HIGHEST)


def _softmax_rows(z, m_old, l_old, acc_old, pv):
    m_new = jnp.maximum(m_old, jnp.max(z, axis=1, keepdims=True))
    alpha = jnp.exp(m_old - m_new)
    p = jnp.exp(z - m_new)
    l_new = alpha * l_old + jnp.sum(p, axis=1, keepdims=True)
    acc_new = alpha * acc_old + pv(p.astype(BF16))
    return m_new, l_new, acc_new


def _softmax_cols(zt, m_old, l_old, acc_old, vt, shift=None):
    m_tile = jnp.max(zt, axis=0, keepdims=True)
    if shift is None:
        m_new = jnp.maximum(m_old, m_tile)
        p = jnp.exp(zt - m_new)
    else:
        m_new = jnp.maximum(m_old, m_tile + shift)
        p = jnp.exp(zt - (m_new - shift))
    alpha = jnp.exp(m_old - m_new)
    l_new = alpha * l_old + jnp.sum(p, axis=0, keepdims=True)
    acc_new = alpha * acc_old + _dot(vt, p.astype(BF16))
    return m_new, l_new, acc_new


def _ordered_value(code):
    key = code ^ jnp.int32(INT_MIN)
    return pltpu.bitcast(jnp.where(key < 0, key ^ jnp.int32(0x7FFFFFFF), key), F32)


def _kth_largest(count_ge, n_sel, shape):
    def below_neg_inf(code):
        return jnp.logical_and(code >= 0, code < jnp.int32(NEG_INF_CODE))

    def step(it, code):
        trial = code | lax.shift_left(jnp.int32(1), jnp.int32(31) - it)
        ok = jnp.logical_or(below_neg_inf(trial), count_ge(_ordered_value(trial)) >= n_sel)
        return jnp.where(ok, trial, code)

    code = lax.fori_loop(0, 32, step, jnp.zeros(shape, jnp.int32))
    return _ordered_value(jnp.where(below_neg_inf(code), jnp.int32(NEG_INF_CODE), code))


def _rms_proj_kernel(x_ref, g_ref, *refs, kinds, logsig_last):
    n_w = len(kinds)
    w_refs = refs[:n_w]
    if logsig_last:
        bias_ref = refs[n_w]
        o_refs = refs[n_w + 1:]
    else:
        o_refs = refs[n_w:]
    h = _rms(x_ref[...], g_ref[...]).astype(BF16)
    for idx, kind in enumerate(kinds):
        if kind == "col":
            o_refs[idx][0] = _dot_nt(w_refs[idx][...], h)
            continue
        y = _dot(h, w_refs[idx][...])
        if logsig_last and idx == n_w - 1:
            y = _log_sigmoid(y + bias_ref[...])
        o_refs[idx][...] = y.astype(o_refs[idx].dtype)


def _rms_proj(x2d, g, items, tm, logsig_bias=None, seq=None):
    m, d = x2d.shape
    kinds = tuple(kind for _, kind in items)
    in_specs = [pl.BlockSpec((tm, d), lambda i: (i, 0)), pl.BlockSpec((1, d), lambda i: (0, 0))]
    in_specs += [pl.BlockSpec(w.shape, lambda i: (0, 0)) for w, _ in items]
    args = [x2d, g.reshape(1, d)] + [w for w, _ in items]
    if logsig_bias is not None:
        in_specs.append(pl.BlockSpec(logsig_bias.shape, lambda i: (0, 0)))
        args.append(logsig_bias)
    out_shape, out_specs = [], []
    for w, kind in items:
        if kind == "col":
            nb, t = seq
            tiles = t // tm
            out_shape.append(jax.ShapeDtypeStruct((nb, w.shape[0], t), F32))
            out_specs.append(pl.BlockSpec((1, w.shape[0], tm), lambda i, tiles=tiles: (i // tiles, 0, i % tiles)))
        else:
            out_shape.append(jax.ShapeDtypeStruct((m, w.shape[1]), F32 if kind == "f32" else BF16))
            out_specs.append(pl.BlockSpec((tm, w.shape[1]), lambda i: (i, 0)))
    return pl.pallas_call(
        functools.partial(_rms_proj_kernel, kinds=kinds, logsig_last=logsig_bias is not None),
        grid=(m // tm,), in_specs=in_specs, out_specs=out_specs, out_shape=out_shape,
        compiler_params=_cparams(("parallel",)), name="rms_proj")(*args)


def _proj_post_kernel(x_ref, g_ref, *refs, n_in):
    o_refs = refs[:n_in]
    w_refs = refs[n_in:2 * n_in]
    out_ref = refs[2 * n_in]
    acc = None
    for o, w in zip(o_refs, w_refs):
        t = _dot(o[...].astype(BF16), w[...])
        acc = t if acc is None else acc + t
    out_ref[...] = x_ref[...] + _rms(acc, g_ref[...])


def _proj_post(x2d, g, o_list, w_list, tm):
    m, d = x2d.shape
    n_in = len(o_list)
    in_specs = [pl.BlockSpec((tm, d), lambda i: (i, 0)), pl.BlockSpec((1, d), lambda i: (0, 0))]
    in_specs += [pl.BlockSpec((tm, o.shape[1]), lambda i: (i, 0)) for o in o_list]
    in_specs += [pl.BlockSpec(w.shape, lambda i: (0, 0)) for w in w_list]
    return pl.pallas_call(
        functools.partial(_proj_post_kernel, n_in=n_in),
        grid=(m // tm,), in_specs=in_specs, out_specs=pl.BlockSpec((tm, d), lambda i: (i, 0)),
        out_shape=jax.ShapeDtypeStruct((m, d), F32),
        compiler_params=_cparams(("parallel",)), name="proj_post")(x2d, g.reshape(1, d), *o_list, *w_list)


def _ffn_kernel(*refs, tm, tiles_per_seq, sample, sub):
    if sample:
        (x_ref, gpre_ref, gpost_ref, wg_ref, wu_ref, wc_ref, bc_ref, wd_ref, p1_ref, p2_ref,
         y_ref, gout_ref, h_ref, acc_ref, gs_ref) = refs
    else:
        (x_ref, gpre_ref, gpost_ref, wg_ref, wu_ref, wc_ref, bc_ref, wd_ref,
         y_ref, gout_ref, h_ref, acc_ref, gs_ref, halo_ref) = refs
    i = pl.program_id(0)
    c = pl.program_id(1)

    @pl.when(c == 0)
    def _():
        h_ref[...] = _rms(x_ref[...], gpre_ref[...]).astype(BF16)
        acc_ref[...] = jnp.zeros_like(acc_ref)

    h = h_ref[...]
    tf = wg_ref.shape[1]
    if sample:
        gs_ref[0:8, :] = jnp.zeros((8, tf), F32)
    else:
        first = (i % tiles_per_seq) == 0

        @pl.when(first)
        def _():
            gs_ref[0:8, :] = jnp.zeros((8, tf), F32)

        @pl.when(jnp.logical_not(first))
        def _():
            gs_ref[0:8, :] = halo_ref[c]

    if sample:
        t = lax.broadcasted_iota(jnp.int32, (tm, 1), 0) % tiles_per_seq
    part = None
    bounds = [(lo, min(lo + sub, tf)) for lo in range(0, tf, sub)]
    nxt = (_dot(h, wg_ref[:, bounds[0][0]:bounds[0][1]]), _dot(h, wu_ref[:, bounds[0][0]:bounds[0][1]]))
    for idx, (lo, hi) in enumerate(bounds):
        g, u = nxt
        if idx + 1 < len(bounds):
            nlo, nhi = bounds[idx + 1]
            nxt = (_dot(h, wg_ref[:, nlo:nhi]), _dot(h, wu_ref[:, nlo:nhi]))
        gs_ref[8:tm + 8, lo:hi] = g
        g1 = gs_ref[7:tm + 7, lo:hi]
        g2 = gs_ref[6:tm + 6, lo:hi]
        if sample:
            g1 = jnp.where(t >= 1, g1, 0.0) + p1_ref[:, lo:hi]
            g2 = jnp.where(t >= 2, g2, 0.0) + p2_ref[:, lo:hi]
            gout_ref[:, lo:hi] = g
        w = wc_ref[:, lo:hi]
        gc = bc_ref[:, lo:hi] + w[0:1, :] * g2
        gc = gc + w[1:2, :] * g1
        gc = gc + w[2:3, :] * g
        a = _gelu_tanh(gc) * u
        d = _dot(a.astype(BF16), wd_ref[lo:hi, :])
        part = d if part is None else part + d
    acc_ref[...] += part
    if not sample:
        tail = gs_ref[tm:tm + 8, :]
        halo_ref[c] = tail
        gout_ref[0] = tail

    @pl.when(c == pl.num_programs(1) - 1)
    def _():
        y_ref[...] = x_ref[...] + _rms(acc_ref[...], gpost_ref[...])


def _ffn(x2d, gpre, gpost, wg, wu, wc, bc, wd, tm, tf, tiles_per_seq, prev=None):
    m, d = x2d.shape
    dff = wg.shape[1]
    nc = dff // tf
    sample = prev is not None
    wmode = pl.Buffered(1) if nc == 1 else None
    in_specs = [
        pl.BlockSpec((tm, d), lambda i, c: (i, 0)),
        pl.BlockSpec((1, d), lambda i, c: (0, 0)),
        pl.BlockSpec((1, d), lambda i, c: (0, 0)),
        pl.BlockSpec((d, tf), lambda i, c: (0, c), pipeline_mode=wmode),
        pl.BlockSpec((d, tf), lambda i, c: (0, c), pipeline_mode=wmode),
        pl.BlockSpec((CONV_W, tf), lambda i, c: (0, c)),
        pl.BlockSpec((1, tf), lambda i, c: (0, c)),
        pl.BlockSpec((tf, d), lambda i, c: (c, 0), pipeline_mode=wmode),
    ]
    args = [x2d, gpre.reshape(1, d), gpost.reshape(1, d), wg, wu, wc, bc.reshape(1, dff), wd]
    scratch = [pltpu.VMEM((tm, d), BF16), pltpu.VMEM((tm, d), F32), pltpu.VMEM((tm + 8, tf), F32)]
    if sample:
        in_specs += [pl.BlockSpec((tm, tf), lambda i, c: (i, c))] * 2
        args += list(prev)
        gout_shape = jax.ShapeDtypeStruct((m, dff), F32)
        gout_spec = pl.BlockSpec((tm, tf), lambda i, c: (i, c))
    else:
        gout_shape = jax.ShapeDtypeStruct((m // tm, 8, dff), F32)
        gout_spec = pl.BlockSpec((1, 8, tf), lambda i, c: (i, 0, c))
        scratch.append(pltpu.VMEM((nc, 8, tf), F32))
    return pl.pallas_call(
        functools.partial(_ffn_kernel, tm=tm, tiles_per_seq=tiles_per_seq, sample=sample, sub=6 * LANES),
        grid=(m // tm, nc), in_specs=in_specs,
        out_specs=[pl.BlockSpec((tm, d), lambda i, c: (i, 0)), gout_spec],
        out_shape=[jax.ShapeDtypeStruct((m, d), F32), gout_shape],
        scratch_shapes=scratch,
        compiler_params=_cparams(("arbitrary", "arbitrary")), name="conv_ffn")(*args)


def _cumsum_kernel(lf_ref, cum_ref, cumt_ref, *, tb, nchunk):
    row = lax.broadcasted_iota(jnp.int32, (tb, tb), 0)
    col = lax.broadcasted_iota(jnp.int32, (tb, tb), 1)
    lower = jnp.where(col <= row, 1.0, 0.0).astype(F32)
    carry = jnp.zeros((1, LANES), F32)
    for c in range(nchunk):
        x = lf_ref[0, c * tb:(c + 1) * tb, :]
        ct = _dot_hi(lower, x) + carry
        cumt_ref[0, c * tb:(c + 1) * tb, :] = ct
        carry = ct[tb - 1:tb, :]
        cum_ref[0, c] = ct.T[0:C_HEADS, :]


def _cumsum(lf3d, tb):
    b, t, _ = lf3d.shape
    nchunk = t // tb
    return pl.pallas_call(
        functools.partial(_cumsum_kernel, tb=tb, nchunk=nchunk),
        grid=(b,), in_specs=[pl.BlockSpec((1, t, LANES), lambda i: (i, 0, 0))],
        out_specs=[pl.BlockSpec((1, nchunk, C_HEADS, tb), lambda i: (i, 0, 0, 0)),
                   pl.BlockSpec((1, t, LANES), lambda i: (i, 0, 0))],
        out_shape=[jax.ShapeDtypeStruct((b, nchunk, C_HEADS, tb), F32),
                   jax.ShapeDtypeStruct((b, t, LANES), F32)],
        compiler_params=_cparams(("parallel",)), name="logf_cumsum")(lf3d)


def _flash2_kernel(*refs, mode, tb, nk, lam_init):
    if mode == "fox":
        q_ref, k_ref, v_ref, cum_ref, cumt_ref, o_ref, vt_ref, ck_ref = refs
    else:
        q_ref, k_ref, v_ref, bias_ref, lamp_ref, subln_ref, o_ref, vt_ref, kb_ref = refs
    g = pl.program_id(1)
    qi = pl.program_id(2)
    lane = lax.broadcasted_iota(jnp.int32, (tb, LANES), 1)
    krow = lax.broadcasted_iota(jnp.int32, (tb, tb), 0)
    qcol = lax.broadcasted_iota(jnp.int32, (tb, tb), 1)
    tri = krow <= qcol

    @pl.when(qi == 0)
    def _():
        if mode == "fox":
            for c in range(nk):
                vt_ref[c] = v_ref[0, :, c * tb:(c + 1) * tb].astype(BF16)
        else:
            kb_ref[...] = k_ref[0].astype(BF16)
            for c in range(nk):
                vt_ref[c] = v_ref[0, c * tb:(c + 1) * tb, :].T.astype(BF16)
        if mode == "fox":
            hrow = lax.broadcasted_iota(jnp.int32, (LANES, LANES), 0)
            for a in range(2):
                onehot = jnp.where(hrow == 2 * g + a, 1.0, 0.0).astype(F32)
                ck_ref[a] = _dot_hi(cumt_ref[0], onehot)

    q = q_ref[0] * (HEAD_DIM ** -0.5)
    qall = jnp.concatenate([jnp.where(lane < HEAD_DIM, q, 0.0), jnp.where(lane >= HEAD_DIM, q, 0.0)],
                           axis=0).astype(BF16)
    if mode == "fox":
        cq = jnp.concatenate([cum_ref[0, qi, pl.ds(2 * g + a, 1), :] for a in range(2)], axis=1)
    rep = tb // LANES

    def qk(c):
        krows = k_ref[0, c * tb:(c + 1) * tb, :] if mode == "fox" else kb_ref[c * tb:(c + 1) * tb, :]
        return _dot_nt(krows, qall)

    def attend(nfull):
        m = jnp.full((1, 2 * tb), NEG, F32)
        l = jnp.zeros((1, 2 * tb), F32)
        acc = jnp.zeros((LANES, 2 * tb), F32)
        zt = qk(0)
        for c in range(nfull + 1):
            zt_next = qk(c + 1) if c < nfull else None
            if mode == "fox":
                ck0 = ck_ref[0, c * tb:(c + 1) * tb, :]
                ck1 = ck_ref[1, c * tb:(c + 1) * tb, :]
                zt = zt - jnp.concatenate([ck0] * rep + [ck1] * rep, axis=1)
                shift = cq
            else:
                bt = bias_ref[0, min(nfull - c, 2)]
                zt = zt + jnp.concatenate([bt, bt], axis=1)
                shift = None
            if c == nfull:
                zt = jnp.where(jnp.concatenate([tri, tri], axis=1), zt, NEG)
            m, l, acc = _softmax_cols(zt, m, l, acc, vt_ref[c], shift)
            zt = zt_next
        o = acc / l
        o0 = o[:, 0:tb].T
        o1 = o[:, tb:2 * tb].T
        if mode == "fox":
            o_ref[0] = jnp.where(lane < HEAD_DIM, o0, o1)
        else:
            lp = lamp_ref[...]
            lam = (jnp.exp(jnp.sum(lp[0:1] * lp[1:2], axis=1, keepdims=True))
                   - jnp.exp(jnp.sum(lp[2:3] * lp[3:4], axis=1, keepdims=True)) + lam_init)
            o_ref[0] = _rms(o0 - lam * o1, subln_ref[...]) * (1.0 - lam_init)

    for blk in range(nk):
        pl.when(qi == blk)(functools.partial(attend, blk))


def _flash2(mode, q, k, v, extra, tb, lam_init=0.0):
    b, t, width = q.shape
    ng = width // LANES
    nq = t // tb
    in_specs = [pl.BlockSpec((1, tb, LANES), lambda i, g, j: (i, j, g)),
                pl.BlockSpec((1, t, LANES), lambda i, g, j: (i, 0, g)),
                pl.BlockSpec((1, t, LANES), lambda i, g, j: (i, 0, g))]
    scratch = [pltpu.VMEM((nq, LANES, tb), BF16)]
    if mode == "fox":
        cum, cumt = extra
        in_specs[2] = pl.BlockSpec((1, LANES, t), lambda i, g, j: (i, g, 0))
        in_specs += [pl.BlockSpec((1, nq, C_HEADS, tb), lambda i, g, j: (i, 0, 0, 0)),
                     pl.BlockSpec((1, t, LANES), lambda i, g, j: (i, 0, 0))]
        scratch.append(pltpu.VMEM((2, t, LANES), F32))
    else:
        bias, lamp, subln = extra
        in_specs += [pl.BlockSpec((1, 3, tb, tb), lambda i, g, j: (g, 0, 0, 0)),
                     pl.BlockSpec(lamp.shape, lambda i, g, j: (0, 0)),
                     pl.BlockSpec(subln.shape, lambda i, g, j: (0, 0))]
        scratch.append(pltpu.VMEM((t, LANES), BF16))
    return pl.pallas_call(
        functools.partial(_flash2_kernel, mode=mode, tb=tb, nk=nq, lam_init=lam_init),
        grid=(b, ng, nq), in_specs=in_specs,
        out_specs=pl.BlockSpec((1, tb, LANES), lambda i, g, j: (i, j, g)),
        out_shape=jax.ShapeDtypeStruct((b, t, width), F32),
        scratch_shapes=scratch,
        compiler_params=_cparams(("arbitrary", "arbitrary", "arbitrary")),
        name="attn_" + mode)(q, k, v, *extra)


def _dsa_p_kernel(qa_ref, ka_ref, vat_in_ref, qi_ref, ki_ref, wi_ref, bias_ref, o_ref,
                  vat_ref, key_ref, qm_ref, *, n_sel, nk):
    tb = LANES
    qi = pl.program_id(1)
    krow = lax.broadcasted_iota(jnp.int32, (tb, tb), 0)
    qcol = lax.broadcasted_iota(jnp.int32, (tb, tb), 1)
    tri = krow <= qcol

    @pl.when(qi == 0)
    def _():
        for c in range(nk // 2):
            vat_ref[c] = vat_in_ref[0, :, c * 2 * tb:(c + 1) * 2 * tb].astype(BF16)

    qidx = qi_ref[0]
    qh = jnp.concatenate([qidx[:, h * IDX_DIM:(h + 1) * IDX_DIM] for h in range(IDX_HEADS)],
                         axis=0).astype(BF16)
    wt = (wi_ref[0] * (IDX_DIM ** -0.5 * IDX_HEADS ** -0.5)).T
    wall = jnp.concatenate([wt[h:h + 1, :] for h in range(IDX_HEADS)], axis=1)

    npairs = (qi + 2) // 2

    def chunk_valid(c):
        return jnp.logical_or(c < qi, jnp.logical_and(c == qi, tri))

    def p1(p, carry):
        off = pl.multiple_of(p * 2 * tb, 2 * tb)
        sh = jnp.maximum(_dot_nt(ki_ref[0, pl.ds(off, 2 * tb), :], qh), 0.0) * wall
        s = sh[:, 0:tb]
        for h in range(1, IDX_HEADS):
            s = s + sh[:, h * tb:(h + 1) * tb]
        valid = jnp.concatenate([chunk_valid(2 * p), chunk_valid(2 * p + 1)], axis=0)
        key_ref[pl.ds(2 * p, 2)] = jnp.where(valid, s, -jnp.inf).reshape(2, tb, tb)
        return carry

    lax.fori_loop(0, npairs, p1, 0)

    def count(pred):
        def cb(p, acc):
            hit = jnp.where(pred(key_ref[pl.ds(2 * p, 2)]), 1.0, 0.0)
            return acc + (hit[0] + hit[1])
        acc = lax.fori_loop(0, npairs, cb, jnp.zeros((tb, tb), F32))
        return jnp.sum(acc, axis=0, keepdims=True)

    thr = _kth_largest(lambda trial: count(lambda kk: kk >= trial), n_sel, (1, tb))
    need = n_sel - count(lambda kk: kk > thr)
    lstrict = jnp.where(qcol < krow, 1.0, 0.0).astype(BF16)

    lane = qcol
    qa = qa_ref[0] * (HEAD_DIM ** -0.5)
    for j in range(A_HEADS):
        blk = qa[:, (j // 2) * LANES:(j // 2 + 1) * LANES]
        grp = j // A_GROUP
        if j % 2 != grp:
            blk = pltpu.roll(blk, HEAD_DIM, axis=1)
        keep = (lane < HEAD_DIM) if grp == 0 else (lane >= HEAD_DIM)
        qm_ref[j] = jnp.where(keep, blk, 0.0).astype(BF16)

    qall = qm_ref[...].reshape(A_HEADS * tb, LANES)

    def chunk_mask(c, carry):
        kk = key_ref[c]
        eq = kk == thr
        eqf = jnp.where(eq, 1.0, 0.0)
        prefix = _dot(lstrict, eqf.astype(BF16)) + carry
        sel = jnp.logical_or(kk > thr, jnp.logical_and(eq, prefix < need))
        sel = jnp.logical_and(sel, chunk_valid(c))
        am = jnp.where(sel, 0.0, NEG)
        return jnp.concatenate([am] * A_HEADS, axis=1), carry + jnp.sum(eqf, axis=0, keepdims=True)

    def qk(p):
        return _dot_nt(ka_ref[0, p * 2 * tb:(p + 1) * 2 * tb, :], qall)

    def attend(count):
        carry = jnp.zeros((1, tb), F32)
        m = jnp.full((1, A_HEADS * tb), NEG, F32)
        l = jnp.zeros((1, A_HEADS * tb), F32)
        acc = jnp.zeros((LANES, A_HEADS * tb), F32)
        zt = qk(0)
        for p in range(count):
            zt_next = qk(p + 1) if p + 1 < count else None
            am_a, carry = chunk_mask(2 * p, carry)
            am_b, carry = chunk_mask(2 * p + 1, carry)
            bias = jnp.concatenate([bias_ref[jnp.clip(qi - 2 * p, 0, 2)],
                                    bias_ref[jnp.clip(qi - 2 * p - 1, 0, 2)]], axis=0)
            zt = zt + bias + jnp.concatenate([am_a, am_b], axis=0)
            m, l, acc = _softmax_cols(zt, m, l, acc, vat_ref[p])
            zt = zt_next
        oall = acc / l
        for c2 in range(A_HEADS // 2):
            grp = (2 * c2) // A_GROUP
            lo = oall[:, (2 * c2) * tb:(2 * c2 + 1) * tb].T
            hi = oall[:, (2 * c2 + 1) * tb:(2 * c2 + 2) * tb].T
            if grp == 0:
                hi = pltpu.roll(hi, HEAD_DIM, axis=1)
            else:
                lo = pltpu.roll(lo, HEAD_DIM, axis=1)
            o_ref[0, :, c2 * LANES:(c2 + 1) * LANES] = jnp.where(lane < HEAD_DIM, lo, hi)

    for count in range(1, nk // 2 + 1):
        pl.when(npairs == count)(functools.partial(attend, count))


def _dsa_p(q_a, k_a, v_at, q_i, k_i, w_i, bias, n_sel):
    b, t, _ = q_a.shape
    tb = LANES
    nq = t // tb
    in_specs = [pl.BlockSpec((1, tb, q_a.shape[2]), lambda i, j: (i, j, 0)),
                pl.BlockSpec((1, t, LANES), lambda i, j: (i, 0, 0)),
                pl.BlockSpec((1, LANES, t), lambda i, j: (i, 0, 0)),
                pl.BlockSpec((1, tb, q_i.shape[2]), lambda i, j: (i, j, 0)),
                pl.BlockSpec((1, t, IDX_DIM), lambda i, j: (i, 0, 0)),
                pl.BlockSpec((1, tb, LANES), lambda i, j: (i, j, 0)),
                pl.BlockSpec(bias.shape, lambda i, j: (0, 0, 0))]
    return pl.pallas_call(
        functools.partial(_dsa_p_kernel, n_sel=n_sel, nk=nq),
        grid=(b, nq), in_specs=in_specs,
        out_specs=pl.BlockSpec((1, tb, q_a.shape[2]), lambda i, j: (i, j, 0)),
        out_shape=jax.ShapeDtypeStruct(q_a.shape, F32),
        scratch_shapes=[pltpu.VMEM((nq // 2, LANES, 2 * tb), BF16),
                        pltpu.VMEM((nq, tb, tb), F32),
                        pltpu.VMEM((A_HEADS, tb, LANES), BF16)],
        compiler_params=_cparams(("arbitrary", "arbitrary")), name="attn_dsa")(q_a, k_a, v_at, q_i, k_i, w_i, bias)


def _fox_s_kernel(pt_ref, q_ref, kn_ref, vn_ref, lfn_ref, *refs, ts, pp):
    kt_refs = refs[:pp]
    vt_refs = refs[pp:2 * pp]
    lft_refs = refs[2 * pp:3 * pp]
    o_ref = refs[3 * pp]
    qbd_ref, kpad_ref, vpad_ref, lfpad_ref, m_ref, l_ref, acc_ref, carry_ref, cq_ref = refs[3 * pp + 1:]
    b = pl.program_id(0)
    s = pl.program_id(1)
    nrow = ts * C_HEADS
    width = C_HEADS * HEAD_DIM
    row = lax.broadcasted_iota(jnp.int32, (PAGE, PAGE), 0)
    col = lax.broadcasted_iota(jnp.int32, (PAGE, PAGE), 1)

    @pl.when(jnp.logical_and(b == 0, s == 0))
    def _():
        kpad_ref[...] = jnp.zeros_like(kpad_ref)
        vpad_ref[...] = jnp.zeros_like(vpad_ref)

    def update(z, pv):
        m_new, l_new, acc_new = _softmax_rows(z, m_ref[...], l_ref[...], acc_ref[...], pv)
        m_ref[...] = m_new
        l_ref[...] = l_new
        acc_ref[...] = acc_new

    @pl.when(s == 0)
    def _():
        hmask = (lax.broadcasted_iota(jnp.int32, (C_HEADS, width), 1) // HEAD_DIM
                 == lax.broadcasted_iota(jnp.int32, (C_HEADS, width), 0))
        q = q_ref[0] * (HEAD_DIM ** -0.5)
        for i in range(ts):
            qbd_ref[i * C_HEADS:(i + 1) * C_HEADS, :] = jnp.where(
                hmask, jnp.broadcast_to(q[i:i + 1, :], (C_HEADS, width)), 0.0).astype(BF16)
        kpad_ref[0:ts, :] = kn_ref[0]
        vpad_ref[0:ts, :] = vn_ref[0]
        lfpad_ref[...] = jnp.zeros_like(lfpad_ref)
        lfpad_ref[0:ts, :] = lfn_ref[0]
        m_ref[...] = jnp.full(m_ref.shape, NEG, F32)
        l_ref[...] = jnp.zeros_like(l_ref)
        acc_ref[...] = jnp.zeros_like(acc_ref)
        carry_ref[...] = jnp.zeros_like(carry_ref)
        lft = lfpad_ref[...].T[0:C_HEADS, :]
        incl = jnp.where(row <= col, 1.0, 0.0).astype(F32)
        cnew = _dot_hi(lft, incl)
        for i in range(ts):
            cq_ref[i * C_HEADS:(i + 1) * C_HEADS, :] = jnp.broadcast_to(cnew[:, i:i + 1], (C_HEADS, PAGE))
        z = _dot_nt(qbd_ref[...], kpad_ref[...].astype(BF16))
        z = z + cq_ref[...] - jnp.concatenate([cnew] * ts, axis=0)
        rr = lax.broadcasted_iota(jnp.int32, (nrow, PAGE), 0)
        cc = lax.broadcasted_iota(jnp.int32, (nrow, PAGE), 1)
        z = jnp.where(cc * C_HEADS <= rr, z, NEG)
        vnew = vpad_ref[...].astype(BF16)
        update(z, lambda p: _dot(p, vnew))

    later = jnp.where(row > col, 1.0, 0.0).astype(F32)
    carry = carry_ref[...]
    sufs = []
    for k in range(pp):
        lft = lft_refs[k][...]
        sufs.append(_dot_hi(lft, later) + carry)
        carry = carry + jnp.sum(lft, axis=1, keepdims=True)
    carry_ref[...] = carry
    suf = jnp.concatenate(sufs, axis=1)
    kt = jnp.concatenate([kt_refs[k][...].astype(BF16) for k in range(pp)], axis=1)
    vt = jnp.concatenate([vt_refs[k][...].astype(BF16) for k in range(pp)], axis=1)
    z = _dot(qbd_ref[...], kt)
    z = z + jnp.concatenate([cq_ref[...]] * pp, axis=1) + jnp.concatenate([suf] * ts, axis=0)
    update(z, lambda p: _dot_nt(p, vt))

    @pl.when(s == pl.num_programs(1) - 1)
    def _():
        hmask = (lax.broadcasted_iota(jnp.int32, (C_HEADS, width), 1) // HEAD_DIM
                 == lax.broadcasted_iota(jnp.int32, (C_HEADS, width), 0))
        o = acc_ref[...] / l_ref[...]
        for i in range(ts):
            blk = jnp.where(hmask, o[i * C_HEADS:(i + 1) * C_HEADS, :], 0.0)
            o_ref[0, i:i + 1, :] = jnp.sum(blk, axis=0, keepdims=True)


def _fox_s(q, kn, vn, lfn, cache_kt, cache_vt, cache_lft, layer, page_table, pp):
    b, ts, width = q.shape
    npg = page_table.shape[1]
    nrow = ts * C_HEADS

    def page_idx(k):
        return lambda i, s, pt: (layer, pt[i, npg - 1 - (s * pp + k)], 0, 0)

    def seq_idx(i, s, pt):
        return (i, 0, 0)

    in_specs = [pl.BlockSpec((1, ts, width), seq_idx), pl.BlockSpec((1, ts, width), seq_idx),
                pl.BlockSpec((1, ts, width), seq_idx), pl.BlockSpec((1, ts, LANES), seq_idx)]
    in_specs += [pl.BlockSpec((None, None, width, PAGE), page_idx(k)) for k in range(pp)]
    in_specs += [pl.BlockSpec((None, None, width, PAGE), page_idx(k)) for k in range(pp)]
    in_specs += [pl.BlockSpec((None, None, C_HEADS, PAGE), page_idx(k)) for k in range(pp)]
    grid_spec = pltpu.PrefetchScalarGridSpec(
        num_scalar_prefetch=1, grid=(b, npg // pp), in_specs=in_specs,
        out_specs=pl.BlockSpec((1, ts, width), seq_idx),
        scratch_shapes=[pltpu.VMEM((nrow, width), BF16), pltpu.VMEM((PAGE, width), F32),
                        pltpu.VMEM((PAGE, width), F32), pltpu.VMEM((PAGE, LANES), F32),
                        pltpu.VMEM((nrow, 1), F32), pltpu.VMEM((nrow, 1), F32),
                        pltpu.VMEM((nrow, width), F32), pltpu.VMEM((C_HEADS, PAGE), F32),
                        pltpu.VMEM((nrow, PAGE), F32)])
    return pl.pallas_call(
        functools.partial(_fox_s_kernel, ts=ts, pp=pp), grid_spec=grid_spec,
        out_shape=jax.ShapeDtypeStruct((b, ts, width), F32),
        compiler_params=_cparams(("arbitrary", "arbitrary")), name="attn_fox_sample")(
            page_table, q, kn, vn, lfn, *([cache_kt] * pp), *([cache_vt] * pp), *([cache_lft] * pp))


def _diff_s_kernel(pt_ref, q_ref, kn_ref, vn_ref, *refs, ts, pp, npg, lam_init):
    k_refs = refs[:pp]
    v_refs = refs[pp:2 * pp]
    (bias_ref, biasn_ref, lamp_ref, subln_ref, o_ref,
     qx_ref, kpad_ref, vpad_ref, m_ref, l_ref, acc_ref) = refs[2 * pp:]
    b = pl.program_id(0)
    s = pl.program_id(1)
    hrows = B_HEADS * ts
    lane = lax.broadcasted_iota(jnp.int32, (ts, LANES), 1)

    @pl.when(jnp.logical_and(b == 0, s == 0))
    def _():
        kpad_ref[...] = jnp.zeros_like(kpad_ref)
        vpad_ref[...] = jnp.zeros_like(vpad_ref)

    @pl.when(s == 0)
    def _():
        q = q_ref[0] * (HEAD_DIM ** -0.5)
        for n in range(B_HEADS):
            blk = q[:, n * LANES:(n + 1) * LANES]
            qx_ref[n * ts:(n + 1) * ts, :] = jnp.where(lane < HEAD_DIM, blk, 0.0)
            qx_ref[hrows + n * ts:hrows + (n + 1) * ts, :] = jnp.where(lane >= HEAD_DIM, blk, 0.0)
        m_ref[...] = jnp.full(m_ref.shape, NEG, F32)
        l_ref[...] = jnp.zeros_like(l_ref)
        acc_ref[...] = jnp.zeros_like(acc_ref)

    def step(kall, vall, bias):
        z = _dot_nt(qx_ref[...].astype(BF16), kall) + bias
        m_new, l_new, acc_new = _softmax_rows(z, m_ref[...], l_ref[...], acc_ref[...], lambda p: _dot(p, vall))
        m_ref[...] = m_new
        l_ref[...] = l_new
        acc_ref[...] = acc_new

    step(jnp.concatenate([k_refs[k][...].astype(BF16) for k in range(pp)], axis=0),
         jnp.concatenate([v_refs[k][...].astype(BF16) for k in range(pp)], axis=0), bias_ref[s])

    @pl.when(s == pl.num_programs(1) - 1)
    def _():
        kpad_ref[0:ts * B_HEADS, :] = kn_ref[0]
        vpad_ref[0:ts * B_HEADS, :] = vn_ref[0]
        step(kpad_ref[...].astype(BF16), vpad_ref[...].astype(BF16), biasn_ref[...])
        lp = lamp_ref[...]
        lam = (jnp.exp(jnp.sum(lp[0:1] * lp[1:2], axis=1, keepdims=True))
               - jnp.exp(jnp.sum(lp[2:3] * lp[3:4], axis=1, keepdims=True)) + lam_init)
        o = acc_ref[...] / l_ref[...]
        od = o[0:hrows] - lam * o[hrows:2 * hrows]
        for n in range(B_HEADS):
            o_ref[0, :, n * B_VDIM:(n + 1) * B_VDIM] = (
                _rms(od[n * ts:(n + 1) * ts], subln_ref[...]) * (1.0 - lam_init))


def _diff_s(q, kn16, vn16, cache_k, cache_v, layer, page_table, bias, bias_new, lamp, subln, lam_init, pp):
    b, ts, width = q.shape
    npg = page_table.shape[1]
    nrow = 2 * B_HEADS * ts
    krows = PAGE * B_HEADS

    def page_idx(k):
        return lambda i, s, pt: (layer, pt[i, s * pp + k], 0, 0)

    def seq_idx(i, s, pt):
        return (i, 0, 0)

    in_specs = [pl.BlockSpec((1, ts, width), seq_idx), pl.BlockSpec((1,) + kn16.shape[1:], seq_idx),
                pl.BlockSpec((1,) + vn16.shape[1:], seq_idx)]
    in_specs += [pl.BlockSpec((None, None, krows, LANES), page_idx(k)) for k in range(pp)] * 2
    in_specs += [pl.BlockSpec(bias.shape, lambda i, s, pt: (0, 0, 0)),
                 pl.BlockSpec(bias_new.shape, lambda i, s, pt: (0, 0)),
                 pl.BlockSpec(lamp.shape, lambda i, s, pt: (0, 0)),
                 pl.BlockSpec(subln.shape, lambda i, s, pt: (0, 0))]
    grid_spec = pltpu.PrefetchScalarGridSpec(
        num_scalar_prefetch=1, grid=(b, npg // pp), in_specs=in_specs,
        out_specs=pl.BlockSpec((1, ts, width), seq_idx),
        scratch_shapes=[pltpu.VMEM((nrow, LANES), F32), pltpu.VMEM((krows, LANES), F32),
                        pltpu.VMEM((krows, LANES), F32), pltpu.VMEM((nrow, 1), F32),
                        pltpu.VMEM((nrow, 1), F32), pltpu.VMEM((nrow, LANES), F32)])
    return pl.pallas_call(
        functools.partial(_diff_s_kernel, ts=ts, pp=pp, npg=npg, lam_init=lam_init), grid_spec=grid_spec,
        out_shape=jax.ShapeDtypeStruct((b, ts, width), F32),
        compiler_params=_cparams(("arbitrary", "arbitrary")), name="attn_diff_sample")(
            page_table, q, kn16, vn16, *([cache_k] * pp), *([cache_v] * pp), bias, bias_new, lamp, subln)


def _dsa_index_s_kernel(pt_ref, qi_ref, wi_ref, kin_ref, *refs, ts, pp, nchunk):
    kit_refs = refs[:pp]
    key_ref, pad_ref = refs[pp:]
    b = pl.program_id(0)
    s = pl.program_id(1)
    last = nchunk - 1
    rpad = SUBLANES

    @pl.when(jnp.logical_and(b == 0, s == 0))
    def _():
        pad_ref[...] = jnp.zeros_like(pad_ref)

    def index_scores(kit):
        s32 = jnp.maximum(_dot(qi_ref[0].astype(BF16), kit.astype(BF16)), 0.0) * wi_ref[0]
        sc = s32[0:rpad]
        for h in range(1, IDX_HEADS):
            sc = sc + s32[h * rpad:(h + 1) * rpad]
        return sc

    for k in range(pp):
        key_ref[0, s * pp + k] = index_scores(kit_refs[k][...])

    @pl.when(s == pl.num_programs(1) - 1)
    def _():
        pad_ref[0:ts, 0:IDX_DIM] = kin_ref[0]
        rr = lax.broadcasted_iota(jnp.int32, (rpad, PAGE), 0)
        cc = lax.broadcasted_iota(jnp.int32, (rpad, PAGE), 1)
        causal = jnp.logical_and(cc <= rr, cc < ts)
        key_ref[0, last] = jnp.where(causal, index_scores(pad_ref[...].T[0:IDX_DIM, :]), -jnp.inf)


def _dsa_index_s(qi32, w32, kin, cache_kit, layer, page_table, ts, pp):
    b = qi32.shape[0]
    npg = page_table.shape[1]
    nchunk = npg + 1

    def page_idx(k):
        return lambda i, s, pt: (layer, pt[i, s * pp + k], 0, 0)

    def seq_idx(i, s, pt):
        return (i, 0, 0)

    in_specs = [pl.BlockSpec((1,) + qi32.shape[1:], seq_idx), pl.BlockSpec((1,) + w32.shape[1:], seq_idx),
                pl.BlockSpec((1, ts, IDX_DIM), seq_idx)]
    in_specs += [pl.BlockSpec((None, None, IDX_DIM, PAGE), page_idx(k)) for k in range(pp)]
    grid_spec = pltpu.PrefetchScalarGridSpec(
        num_scalar_prefetch=1, grid=(b, npg // pp), in_specs=in_specs,
        out_specs=pl.BlockSpec((1, nchunk, SUBLANES, PAGE), lambda i, s, pt: (i, 0, 0, 0)),
        scratch_shapes=[pltpu.VMEM((PAGE, LANES), F32)])
    return pl.pallas_call(
        functools.partial(_dsa_index_s_kernel, ts=ts, pp=pp, nchunk=nchunk), grid_spec=grid_spec,
        out_shape=jax.ShapeDtypeStruct((b, nchunk, SUBLANES, PAGE), F32),
        compiler_params=_cparams(("arbitrary", "arbitrary")), name="dsa_index_sample")(
            page_table, qi32, w32, kin, *([cache_kit] * pp))


def _select_kernel(keys_ref, thr_ref, need_ref, *, n_sel, nchunk):
    r = keys_ref.shape[1]
    wid = min(r, LANES)

    def count(pred, ref):
        out = []
        for lo in range(0, r, wid):
            refv = ref[:, lo:lo + wid]

            def cb(c, acc, lo=lo, refv=refv):
                off = pl.multiple_of(c * PAGE, PAGE)
                return acc + jnp.where(pred(keys_ref[pl.ds(off, PAGE), lo:lo + wid], refv), 1.0, 0.0)
            acc = lax.fori_loop(0, nchunk, cb, jnp.zeros((PAGE, wid), F32))
            out.append(jnp.sum(acc, axis=0, keepdims=True))
        return jnp.concatenate(out, axis=1)

    thr = _kth_largest(lambda trial: count(lambda kk, t: kk >= t, trial), n_sel, (1, r))
    thr_ref[...] = thr
    need_ref[...] = n_sel - count(lambda kk, t: kk > t, thr)


def _select(keys_t, n_sel):
    nkeys, r = keys_t.shape
    return pl.pallas_call(
        functools.partial(_select_kernel, n_sel=n_sel, nchunk=nkeys // PAGE),
        out_shape=[jax.ShapeDtypeStruct((1, r), F32), jax.ShapeDtypeStruct((1, r), F32)],
        compiler_params=pltpu.CompilerParams(vmem_limit_bytes=VMEM_LIMIT), name="dsa_select_sample")(keys_t)


def _dsa_attn_s_kernel(pt_ref, qa_ref, kn_ref, vn_ref, key_ref, thr_ref, need_ref, *refs, ts, pp, nchunk):
    kt_refs = refs[:pp]
    vt_refs = refs[pp:2 * pp]
    bias_ref, o_ref, kst_ref, vst_ref, pad_ref = refs[2 * pp:]
    b = pl.program_id(0)
    s = pl.program_id(1)
    last = nchunk - 1
    rpad = SUBLANES

    @pl.when(jnp.logical_and(b == 0, s == 0))
    def _():
        pad_ref[...] = jnp.zeros_like(pad_ref)

    for k in range(pp):
        kst_ref[s * pp + k] = kt_refs[k][...]
        vst_ref[s * pp + k] = vt_refs[k][...]

    @pl.when(s == pl.num_programs(1) - 1)
    def _():
        pad_ref[0, 0:ts, :] = kn_ref[0]
        pad_ref[1, 0:ts, :] = vn_ref[0]
        kst_ref[last] = pad_ref[0].T
        vst_ref[last] = pad_ref[1].T
        rr = lax.broadcasted_iota(jnp.int32, (rpad, PAGE), 0)
        cc = lax.broadcasted_iota(jnp.int32, (rpad, PAGE), 1)
        causal = jnp.logical_and(cc <= rr, cc < ts)
        thr = thr_ref[0]
        need = need_ref[0]
        r2 = lax.broadcasted_iota(jnp.int32, (PAGE, PAGE), 0)
        c2 = lax.broadcasted_iota(jnp.int32, (PAGE, PAGE), 1)
        ustrict = jnp.where(r2 < c2, 1.0, 0.0).astype(BF16)
        nrow = A_HEADS * rpad
        qbd = qa_ref[0].astype(BF16)
        carry = jnp.zeros((rpad, 1), F32)
        zs = []
        for c in range(nchunk):
            kk = key_ref[0, c]
            eq = kk == thr
            eqf = jnp.where(eq, 1.0, 0.0)
            prefix = _dot(eqf.astype(BF16), ustrict) + carry
            sel = jnp.logical_or(kk > thr, jnp.logical_and(eq, prefix < need))
            if c == last:
                sel = jnp.logical_and(sel, causal)
            am = jnp.where(sel, 0.0, NEG)
            carry = carry + jnp.sum(eqf, axis=1, keepdims=True)
            zs.append(_dot(qbd, kst_ref[c].astype(BF16)) + bias_ref[c] + jnp.concatenate([am] * A_HEADS, axis=0))
        zmax = zs[0]
        for c in range(1, nchunk):
            zmax = jnp.maximum(zmax, zs[c])
        m_row = jnp.max(zmax, axis=1, keepdims=True)
        psum = jnp.zeros((nrow, LANES), F32)
        acc = jnp.zeros((nrow, LANES), F32)
        for c in range(nchunk):
            p = jnp.exp(zs[c] - m_row)
            psum = psum + p
            acc = acc + _dot_nt(p.astype(BF16), vst_ref[c].astype(BF16))
        o = acc / jnp.sum(psum, axis=1, keepdims=True)
        lane = lax.broadcasted_iota(jnp.int32, (rpad, LANES), 1)
        for blk in range(A_HEADS // 2):
            grp = (2 * blk) // A_GROUP
            lo = o[(2 * blk) * rpad:(2 * blk + 1) * rpad]
            hi = o[(2 * blk + 1) * rpad:(2 * blk + 2) * rpad]
            if grp == 0:
                hi = pltpu.roll(hi, HEAD_DIM, axis=1)
            else:
                lo = pltpu.roll(lo, HEAD_DIM, axis=1)
            res = jnp.where(lane < HEAD_DIM, lo, hi)
            o_ref[0, :, blk * LANES:(blk + 1) * LANES] = res[0:ts]


def _dsa_attn_s(qa_bd, kn, vn, keys, thr, need, cache_kt, cache_vt, layer, page_table, bias, ts, pp):
    b = qa_bd.shape[0]
    npg = page_table.shape[1]
    nchunk = npg + 1
    width = A_HEADS * HEAD_DIM

    def page_idx(k):
        return lambda i, s, pt: (layer, pt[i, s * pp + k], 0, 0)

    def seq_idx(i, s, pt):
        return (i, 0, 0)

    in_specs = [pl.BlockSpec((1,) + qa_bd.shape[1:], seq_idx),
                pl.BlockSpec((1, ts, LANES), seq_idx), pl.BlockSpec((1, ts, LANES), seq_idx),
                pl.BlockSpec((1, nchunk, SUBLANES, PAGE), lambda i, s, pt: (i, 0, 0, 0)),
                pl.BlockSpec((1, SUBLANES, PAGE), seq_idx), pl.BlockSpec((1, SUBLANES, PAGE), seq_idx)]
    in_specs += [pl.BlockSpec((None, None, LANES, PAGE), page_idx(k)) for k in range(pp)] * 2
    in_specs += [pl.BlockSpec(bias.shape, lambda i, s, pt: (0, 0, 0))]
    grid_spec = pltpu.PrefetchScalarGridSpec(
        num_scalar_prefetch=1, grid=(b, npg // pp), in_specs=in_specs,
        out_specs=pl.BlockSpec((1, ts, width), seq_idx),
        scratch_shapes=[pltpu.VMEM((nchunk, LANES, PAGE), F32), pltpu.VMEM((nchunk, LANES, PAGE), F32),
                        pltpu.VMEM((2, PAGE, LANES), F32)])
    return pl.pallas_call(
        functools.partial(_dsa_attn_s_kernel, ts=ts, pp=pp, nchunk=nchunk), grid_spec=grid_spec,
        out_shape=jax.ShapeDtypeStruct((b, ts, width), F32),
        compiler_params=_cparams(("arbitrary", "arbitrary")), name="attn_dsa_sample")(
            page_table, qa_bd, kn, vn, keys, thr, need, *([cache_kt] * pp), *([cache_vt] * pp), bias)


def _bucket_starts():
    exact = N_BUCKETS // 2
    starts = list(range(exact))
    for j in range(N_BUCKETS - exact):
        starts.append(min(n for n in range(exact, MAX_DISTANCE + 1)
                          if int(math.log(n / exact) / math.log(MAX_DISTANCE / exact) * (N_BUCKETS - exact)) >= j))
    return starts


def _bias_of(tab, rel):
    n = jnp.maximum(rel, 0)
    bucket = jnp.zeros(rel.shape, jnp.int32)
    for start in _bucket_starts()[1:]:
        bucket = bucket + (n >= start).astype(jnp.int32)
    ones = (1,) * rel.ndim
    hit = bucket[None, None] == jnp.arange(N_BUCKETS, dtype=jnp.int32).reshape((N_BUCKETS, 1) + ones)
    return jnp.sum(jnp.where(hit, tab.astype(F32).reshape(tab.shape + ones), 0.0), axis=0)


def _bias_tiles_t(tab, tb):
    j = jnp.arange(tb, dtype=jnp.int32)[:, None]
    i = jnp.arange(tb, dtype=jnp.int32)[None, :]
    return _bias_of(tab, jnp.stack([d * tb + i - j for d in range(3)]))


def _split(w, sizes):
    offs = np.cumsum((0,) + tuple(sizes))
    return [w[:, int(offs[i]):int(offs[i + 1])] for i in range(len(sizes))]


def _pad_cols(w, n):
    return jnp.pad(w, ((0, 0), (0, n - w.shape[1])))


def _row_tile(m, want):
    t = min(m, want)
    while m % t:
        t //= 2
    return t


def _pages_per_step(npg, want):
    pp = min(npg, want)
    while npg % pp:
        pp -= 1
    return pp


def kernel(x_prompt, x_sample, cache_a_k, cache_a_v, cache_a_kidx, cache_b_k, cache_b_v, cache_c_k, cache_c_v,
           cache_c_logf, state_conv, page_table, rel_bias_table, w_in_even, w_out_even, lambda_q1, lambda_k1,
           lambda_q2, lambda_k2, diff_subln, w_in_odd, b_forget, w_out_odd, norm_mix_pre, norm_mix_post,
           norm_ffn_pre, norm_ffn_post, w_gate_up, w_conv, b_conv, w_down):
    bp, tp, d = x_prompt.shape
    bs, ts, _ = x_sample.shape
    depth = w_gate_up.shape[0]
    n_pool = cache_a_k.shape[1]
    npg = page_table.shape[1]
    past = npg * PAGE
    mp, ms = bp * tp, bs * ts
    tb_dsa = LANES
    tb_attn = min(512, tp)
    nsel_p = min(TOPK_MAX, tp // 4)
    nsel_s = min(TOPK_MAX, (past + ts) // 4)
    tm_p = _row_tile(mp, 512)
    tm_s = _row_tile(ms, 512)
    tf = D_FF
    dff = D_FF
    pp = _pages_per_step(npg, 8)
    pp_ab = _pages_per_step(npg, 16)

    xp = x_prompt.reshape(mp, d)
    xs = x_sample.reshape(ms, d)
    tab_a = rel_bias_table[:, :A_HEADS]
    tab_b = rel_bias_table[:, A_HEADS:]
    bias_a_p = _bias_tiles_t(tab_a, tb_dsa)
    bias_a_p = jnp.transpose(bias_a_p, (1, 2, 0, 3)).reshape(3, tb_dsa, A_HEADS * tb_dsa)
    bias_b_p = _bias_tiles_t(tab_b, tb_attn)
    rel_s = (past + jnp.arange(ts, dtype=jnp.int32))[:, None] - jnp.arange(past + PAGE, dtype=jnp.int32)[None, :]
    rows_a = _bias_of(tab_a, rel_s).reshape(A_HEADS, ts, npg + 1, PAGE)
    rows_b = _bias_of(tab_b, rel_s).reshape(B_HEADS, ts, npg + 1, PAGE)
    bias_a_s = jnp.pad(jnp.moveaxis(rows_a, 2, 0), ((0, 0), (0, 0), (0, SUBLANES - ts), (0, 0)))
    bias_a_s = bias_a_s.reshape(npg + 1, A_HEADS * SUBLANES, PAGE)
    kq_ok = rel_s.reshape(ts, npg + 1, PAGE) >= 0
    same = jnp.eye(B_HEADS, dtype=bool)
    bias_b_s = jnp.where(same[:, None, None, None, :] & kq_ok[None, :, :, :, None],
                         rows_b[..., None], NEG)
    bias_b_s = jnp.moveaxis(bias_b_s, 2, 0).reshape(npg + 1, B_HEADS * ts, PAGE * B_HEADS)
    bias_b_s = jnp.concatenate([bias_b_s, bias_b_s], axis=1)
    bias_b_new = bias_b_s[npg]
    bias_b_s = bias_b_s[:npg].reshape(npg // pp_ab, pp_ab, 2 * B_HEADS * ts, PAGE * B_HEADS)
    bias_b_s = jnp.swapaxes(bias_b_s, 1, 2).reshape(npg // pp_ab, 2 * B_HEADS * ts, pp_ab * PAGE * B_HEADS)

    ca_kt = jnp.transpose(cache_a_k, (0, 1, 3, 4, 2)).reshape(-1, n_pool, LANES, PAGE)
    ca_vt = jnp.transpose(cache_a_v, (0, 1, 3, 4, 2)).reshape(-1, n_pool, LANES, PAGE)
    ca_kit = jnp.transpose(cache_a_kidx, (0, 1, 3, 2))
    cb_k = cache_b_k.reshape(-1, n_pool, PAGE * B_HEADS, 2 * HEAD_DIM)
    cb_v = cache_b_v.reshape(-1, n_pool, PAGE * B_HEADS, B_VDIM)
    cc_kt = jnp.transpose(cache_c_k, (0, 1, 3, 4, 2)).reshape(-1, n_pool, C_HEADS * HEAD_DIM, PAGE)
    cc_vt = jnp.transpose(cache_c_v, (0, 1, 3, 4, 2)).reshape(-1, n_pool, C_HEADS * HEAD_DIM, PAGE)
    cc_lft = jnp.transpose(cache_c_logf, (0, 1, 3, 2))

    even_rows_p, even_rows_s, odd_rows_p, odd_rows_s, conv_p, conv_s = [], [], [], [], [], []
    for l in range(depth):
        if l % 2 == 0:
            e = l // 2
            lam_init = 0.8 - 0.6 * math.exp(-0.3 * l)
            ws = _split(w_in_even[e], EVEN_SPLIT)
            ws[5] = _pad_cols(ws[5], LANES)
            ws = [w.astype(BF16) for w in ws]
            wo = w_out_even[e].astype(BF16)
            wo_parts = [wo[:A_HEADS * HEAD_DIM], wo[A_HEADS * HEAD_DIM:]]
            lamp = jnp.stack([lambda_q1[e], lambda_k1[e], lambda_q2[e], lambda_k2[e]]).astype(F32)
            subln = diff_subln[e].reshape(1, B_VDIM)

            wq_a, wk_a, wv_a, wq_i, wk_i, ww_i, wq_b, wk_b, wv_b = ws
            items = [(wq_a, "f32"), (wk_a, "bf16"), (wk_a.T, "col"), (wv_a.T, "col"), (wq_i, "f32"),
                     (wk_i, "bf16"), (wk_i.T, "col"), (ww_i, "f32"), (wq_b, "f32"), (wk_b, "f32"), (wv_b, "f32")]
            q_a, k_a16, k_at, v_at, q_i, k_i16, k_it, w_i, q_b, k_b, v_b = _rms_proj(
                xp, norm_mix_pre[l], items, tm_p, seq=(bp, tp))
            r3 = lambda a: a.reshape(bp, tp, a.shape[1])
            o_a = _dsa_p(r3(q_a), r3(k_a16), v_at, r3(q_i), r3(k_i16), r3(w_i), bias_a_p, nsel_p)
            o_b = _flash2("diff", r3(q_b), r3(k_b), r3(v_b), (bias_b_p, lamp, subln), tb_attn, lam_init)
            xp = _proj_post(xp, norm_mix_post[l], [o_a.reshape(mp, -1), o_b.reshape(mp, -1)], wo_parts, tm_p)
            kv_rows = lambda a: jnp.transpose(a.reshape(bp, A_KV_HEADS, HEAD_DIM, tp), (0, 3, 1, 2))
            even_rows_p.append((kv_rows(k_at), kv_rows(v_at), jnp.transpose(k_it, (0, 2, 1)),
                                k_b.reshape(bp, tp, B_HEADS, 2 * HEAD_DIM), v_b.reshape(bp, tp, B_HEADS, B_VDIM)))

            q_a, k_a, v_a, q_i, k_i, w_i, q_b, k_b, v_b = _rms_proj(
                xs, norm_mix_pre[l], [(w, "f32") for w in ws], tm_s)
            s3 = lambda a: a.reshape(bs, ts, a.shape[1])
            qa4 = (q_a * (HEAD_DIM ** -0.5)).reshape(bs, ts, A_HEADS, HEAD_DIM)
            qa4 = jnp.pad(jnp.moveaxis(qa4, 1, 2), ((0, 0), (0, 0), (0, SUBLANES - ts), (0, 0)))
            zeros = jnp.zeros_like(qa4)
            qa_bd = jnp.concatenate([jnp.concatenate([qa4[:, :A_GROUP], zeros[:, :A_GROUP]], axis=-1),
                                     jnp.concatenate([zeros[:, A_GROUP:], qa4[:, A_GROUP:]], axis=-1)], axis=1)
            qa_bd = qa_bd.reshape(bs, A_HEADS * SUBLANES, LANES)
            qi4 = jnp.moveaxis(q_i.reshape(bs, ts, IDX_HEADS, IDX_DIM), 1, 2)
            qi32 = jnp.pad(qi4, ((0, 0), (0, 0), (0, SUBLANES - ts), (0, 0))).reshape(bs, IDX_HEADS * SUBLANES, IDX_DIM)
            w4 = jnp.moveaxis(w_i[:, :IDX_HEADS].reshape(bs, ts, IDX_HEADS), 1, 2) * (IDX_DIM ** -0.5 * IDX_HEADS ** -0.5)
            w32 = jnp.pad(w4, ((0, 0), (0, 0), (0, SUBLANES - ts))).reshape(bs, IDX_HEADS * SUBLANES, 1)
            w32 = jnp.broadcast_to(w32, (bs, IDX_HEADS * SUBLANES, PAGE))
            keys = _dsa_index_s(qi32, w32, s3(k_i), ca_kit, e, page_table, ts, pp_ab)
            keys_t = jnp.transpose(keys, (1, 3, 0, 2)).reshape((npg + 1) * PAGE, bs * SUBLANES)
            thr, need = _select(keys_t, nsel_s)
            thr = jnp.broadcast_to(thr.reshape(bs, SUBLANES, 1), (bs, SUBLANES, PAGE))
            need = jnp.broadcast_to(need.reshape(bs, SUBLANES, 1), (bs, SUBLANES, PAGE))
            o_a = _dsa_attn_s(qa_bd, s3(k_a), s3(v_a), keys, thr, need, ca_kt, ca_vt,
                              e, page_table, bias_a_s, ts, pp_ab)
            o_b = _diff_s(s3(q_b), k_b.reshape(bs, ts * B_HEADS, 2 * HEAD_DIM), v_b.reshape(bs, ts * B_HEADS, B_VDIM),
                          cb_k, cb_v, e, page_table, bias_b_s, bias_b_new, lamp, subln, lam_init, pp_ab)
            xs = _proj_post(xs, norm_mix_post[l], [o_a.reshape(ms, -1), o_b.reshape(ms, -1)], wo_parts, tm_s)
            even_rows_s.append((k_a.reshape(bs, ts, A_KV_HEADS, HEAD_DIM), v_a.reshape(bs, ts, A_KV_HEADS, HEAD_DIM),
                                k_i.reshape(bs, ts, IDX_DIM), k_b.reshape(bs, ts, B_HEADS, 2 * HEAD_DIM),
                                v_b.reshape(bs, ts, B_HEADS, B_VDIM)))
        else:
            o = l // 2
            ws = _split(w_in_odd[o], ODD_SPLIT)
            ws[3] = _pad_cols(ws[3], LANES)
            ws = [w.astype(BF16) for w in ws]
            bfp = _pad_cols(b_forget[o].reshape(1, C_HEADS), LANES)
            wo = [w_out_odd[o].astype(BF16)]

            wq, wk, wv, wf = ws
            items = [(wq, "f32"), (wk, "bf16"), (wk.T, "col"), (wv.T, "col"), (wf, "f32")]
            q, k16, k_t, v_t, lf = _rms_proj(xp, norm_mix_pre[l], items, tm_p, logsig_bias=bfp, seq=(bp, tp))
            r3 = lambda a: a.reshape(bp, tp, a.shape[1])
            cum, cumt = _cumsum(r3(lf), tb_attn)
            o_c = _flash2("fox", r3(q), r3(k16), v_t, (cum, cumt), tb_attn)
            xp = _proj_post(xp, norm_mix_post[l], [o_c.reshape(mp, -1)], wo, tm_p)
            kv_rows = lambda a: jnp.transpose(a.reshape(bp, C_HEADS, HEAD_DIM, tp), (0, 3, 1, 2))
            odd_rows_p.append((kv_rows(k_t), kv_rows(v_t), lf[:, :C_HEADS].reshape(bp, tp, C_HEADS)))

            q, k, v, lf = _rms_proj(xs, norm_mix_pre[l], [(w, "f32") for w in ws], tm_s, logsig_bias=bfp)
            s3 = lambda a: a.reshape(bs, ts, a.shape[1])
            o_c = _fox_s(s3(q), s3(k), s3(v), s3(lf), cc_kt, cc_vt, cc_lft, o, page_table, pp)
            xs = _proj_post(xs, norm_mix_post[l], [o_c.reshape(ms, -1)], wo, tm_s)
            odd_rows_s.append((k.reshape(bs, ts, C_HEADS, HEAD_DIM), v.reshape(bs, ts, C_HEADS, HEAD_DIM),
                               lf[:, :C_HEADS].reshape(bs, ts, C_HEADS)))

        wgu = w_gate_up[l].astype(BF16)
        wg, wu = wgu[:, :dff], wgu[:, dff:]
        wd = w_down[l].astype(BF16)
        tiles = tp // tm_p
        xp, tail = _ffn(xp, norm_ffn_pre[l], norm_ffn_post[l], wg, wu, w_conv[l], b_conv[l], wd,
                        tm_p, tf, tiles)
        conv_p.append(tail.reshape(bp, tiles, 8, dff)[:, tiles - 1, 8 - (CONV_W - 1):, :])
        st = state_conv[l]
        prev1 = jnp.concatenate([st[:, 1:2], jnp.zeros((bs, ts - 1, dff), F32)], axis=1).reshape(ms, dff)
        prev2 = jnp.concatenate([st, jnp.zeros((bs, ts - 2, dff), F32)], axis=1).reshape(ms, dff)
        xs, gfull = _ffn(xs, norm_ffn_pre[l], norm_ffn_post[l], wg, wu, w_conv[l], b_conv[l], wd,
                         tm_s, tf, ts, prev=(prev1, prev2))
        conv_s.append(gfull.reshape(bs, ts, dff)[:, ts - (CONV_W - 1):, :])

    def stack(rows, i):
        return jnp.stack([r[i] for r in rows])

    ev_p = [stack(even_rows_p, i) for i in range(5)]
    ev_s = [stack(even_rows_s, i) for i in range(5)]
    od_p = [stack(odd_rows_p, i) for i in range(3)]
    od_s = [stack(odd_rows_s, i) for i in range(3)]
    return (xp.reshape(bp, tp, d), xs.reshape(bs, ts, d),
            ev_p[0], ev_s[0], ev_p[1], ev_s[1], ev_p[2], ev_s[2], ev_p[3], ev_s[3], ev_p[4], ev_s[4],
            od_p[0], od_s[0], od_p[1], od_s[1], od_p[2], od_s[2],
            jnp.stack(conv_p), jnp.stack(conv_s))
```
